```python
import math
import numpy as np
import jax
import jax.numpy as jnp
from jax import lax

D_MODEL = 4096
BATCH = 4
SEQ = 2048
DEPTH = 4
DEC_BATCH = 8
DEC_SEQ = 1
PAST_LEN = 8192
PAGE_SIZE = 128

HEAD_DIM = 128
MIX_A = D_MODEL // 2
N_HEADS_A = MIX_A // HEAD_DIM
DILATIONS = ((128, 1), (512, 4), (2048, 16))
A_WINDOW = max(w for w, _ in DILATIONS)
BAND = 128
MIX_B = D_MODEL - MIX_A
POOL_WINDOWS = (2, 4, 8, 16)
N_POOL_GROUPS = len(POOL_WINDOWS)
POOL_GROUP = MIX_B // N_POOL_GROUPS
POOL_HIST = max(POOL_WINDOWS) - 1
MIX_C = D_MODEL // 2
N_HEADS_C = MIX_C // HEAD_DIM
MIX_D = D_MODEL - MIX_C
N_HEADS_D = 4
HEAD_DIM_D = MIX_D // N_HEADS_D
Q_BLOCK = 128
MLSTM_CHUNK = 128
N_REL_BUCKETS = 32
REL_MAX_DISTANCE = A_WINDOW
LN_EPS = 1e-5
DEEPNORM_ALPHA = (2 * DEPTH) ** 0.25
DEEPNORM_BETA = (8 * DEPTH) ** -0.25
N_EVEN = (DEPTH + 1) // 2
N_ODD = DEPTH // 2
EVEN_SPLITS = (MIX_A, MIX_A, MIX_A, MIX_A, MIX_B, MIX_B)
ODD_SPLITS = (MIX_C, MIX_C, MIX_C, MIX_C, MIX_D, MIX_D, MIX_D, MIX_D, MIX_D, N_HEADS_D, N_HEADS_D)

kernel_name = 'hybrid_dilated_pool_stickbreak_mlstm_step'


def _split(z, sizes):
    cuts = [int(c) for c in np.cumsum(sizes)[:-1]]
    return jnp.split(z, cuts, axis=-1)


def _layer_norm(x, g, b):
    xf = x.astype(jnp.float32)
    mu = jnp.mean(xf, axis=-1, keepdims=True)
    var = jnp.mean(jnp.square(xf - mu), axis=-1, keepdims=True)
    y = (xf - mu) * lax.rsqrt(var + LN_EPS)
    return (y * g.astype(jnp.float32) + b.astype(jnp.float32)).astype(x.dtype)


def _rel_bucket(dist):
    exact = N_REL_BUCKETS // 2
    log_ratio = jnp.log(jnp.maximum(dist, exact).astype(jnp.float32) / exact) / math.log(REL_MAX_DISTANCE / exact)
    large = jnp.minimum(exact + (log_ratio * (N_REL_BUCKETS - exact)).astype(jnp.int32), N_REL_BUCKETS - 1)
    return jnp.where(dist < exact, dist, large)


def _logsumexp(s):
    mx = jnp.max(s, axis=-1, keepdims=True)
    return (mx + jnp.log(jnp.sum(jnp.exp(s - mx), axis=-1, keepdims=True)))[..., 0]


def _merge_by_denominator(outs, lses):
    w = jax.nn.softmax(jnp.stack(lses), axis=0)
    o = jnp.stack(outs).astype(jnp.float32)
    return jnp.sum(w[..., None] * o, axis=0)


def _dilated_attn_prompt(q, k, v, rel_bias):
    b_, s_, h_, dh = q.shape
    scale = 1.0 / math.sqrt(dh)
    i_loc = jnp.arange(BAND)[:, None]
    j_loc = jnp.arange(2 * BAND)[None, :]
    m_loc = i_loc + BAND - j_loc
    outs, lses = [], []
    for window, dil in DILATIONS:
        steps = window // dil
        sub_len = s_ // dil
        n_blk = -(-sub_len // BAND)
        pad = n_blk * BAND - sub_len

        def to_sub(t):
            t = t.reshape(b_, sub_len, dil, h_, dh).transpose(0, 2, 1, 3, 4)
            t = jnp.pad(t, ((0, 0), (0, 0), (0, pad), (0, 0), (0, 0)))
            return t.reshape(b_, dil, n_blk, BAND, h_, dh)

        def with_prev(t):
            prev = jnp.concatenate([jnp.zeros_like(t[:, :, :1]), t[:, :, :-1]], axis=2)
            return jnp.concatenate([prev, t], axis=3)

        qs = to_sub(q)
        ks = with_prev(to_sub(k))
        vs = with_prev(to_sub(v))
        blk = jnp.arange(n_blk)[:, None, None]
        valid = (m_loc >= 0) & (m_loc <= steps) & ((blk > 0) | (j_loc >= BAND))
        bias = rel_bias[_rel_bucket(dil * jnp.clip(m_loc, 0, steps))]
        s = jnp.einsum('bcnqhd,bcnkhd->bcnhqk', qs, ks).astype(jnp.float32) * scale
        s = s + jnp.transpose(bias, (2, 0, 1)).astype(jnp.float32)
        s = jnp.where(valid[None, None, :, None], s, -jnp.inf)
        lse = _logsumexp(s)
        p = jnp.exp(s - lse[..., None]).astype(v.dtype)
        o = jnp.einsum('bcnhqk,bcnkhd->bcnqhd', p, vs)
        o = o.reshape(b_, dil, n_blk * BAND, h_, dh)[:, :, :sub_len]
        outs.append(o.transpose(0, 2, 1, 3, 4).reshape(b_, s_, h_, dh))
        lse = jnp.swapaxes(lse, 3, 4).reshape(b_, dil, n_blk * BAND, h_)[:, :, :sub_len]
        lses.append(lse.transpose(0, 2, 1, 3).reshape(b_, s_, h_))
    return _merge_by_denominator(outs, lses)


def _dilated_attn_sample(q, k_all, v_all, n_hist, past_len, rel_bias):
    t_new, dh = q.shape[1], q.shape[-1]
    scale = 1.0 / math.sqrt(dh)
    i_new = jnp.arange(t_new)[:, None]
    outs, lses = [], []
    for window, dil in DILATIONS:
        steps = window // dil
        m = jnp.arange(steps + 1)[None, :]
        idx = n_hist + i_new - dil * m
        valid = (past_len + i_new - dil * m >= 0) & (idx >= 0)
        idx = jnp.maximum(idx, 0)
        kg = k_all[:, idx]
        vg = v_all[:, idx]
        bias = rel_bias[_rel_bucket(dil * m[0])]
        s = jnp.einsum('bthd,btmhd->bhtm', q, kg).astype(jnp.float32) * scale
        s = s + bias.T[:, None, :].astype(jnp.float32)
        s = jnp.where(valid[None, None], s, -jnp.inf)
        lse = _logsumexp(s)
        p = jnp.exp(s - lse[..., None]).astype(v_all.dtype)
        outs.append(jnp.einsum('bhtm,btmhd->bthd', p, vg))
        lses.append(jnp.swapaxes(lse, 1, 2))
    return _merge_by_denominator(outs, lses)


def _pool_mix(u_all, n_new, pool_w, pool_scale):
    b_, l_, c_ = u_all.shape
    c = jnp.cumsum(u_all.astype(jnp.float32), axis=1)
    c = jnp.concatenate([jnp.zeros((b_, 1, c_), jnp.float32), c], axis=1)
    c = c.reshape(b_, l_ + 1, N_POOL_GROUPS, POOL_GROUP)
    ends = jnp.arange(l_ - n_new, l_) + 1
    wins = jnp.array(POOL_WINDOWS, dtype=jnp.int32)
    starts = jnp.maximum(ends[:, None] - wins[None, :], 0)
    cnt = (ends[:, None] - starts).astype(jnp.float32)
    grp = jnp.arange(N_POOL_GROUPS)[None, :]
    s_end = c[:, ends]
    s_start = c[:, starts, grp]
    x_new = u_all[:, l_ - n_new:].reshape(b_, n_new, N_POOL_GROUPS, POOL_GROUP).astype(jnp.float32)
    pooled = (s_end - s_start) / cnt[None, :, :, None] - x_new
    mixed = jnp.einsum('btgc,gce->btge', pooled.astype(pool_w.dtype), pool_w)
    return mixed.reshape(b_, n_new, c_) * pool_scale


def _stick_breaking(q, k, v, q_pos, k_pos, sb_bias):
    z = jnp.einsum('bqhd,bkhd->bhqk', q, k).astype(jnp.float32) / math.sqrt(q.shape[-1])
    z = z + sb_bias.astype(jnp.float32)[None, :, None, None]
    mask = k_pos[None, :] < q_pos[:, None]
    log_rest = jnp.where(mask, jax.nn.log_sigmoid(-z), 0.0)
    suffix = lax.cumsum(log_rest, axis=3, reverse=True) - log_rest
    a = jnp.where(mask, jnp.exp(jax.nn.log_sigmoid(z) + suffix), 0.0)
    return jnp.einsum('bhqk,bkhd->bqhd', a.astype(v.dtype), v)


def _mlstm_chunk(carry, inp):
    c0, n0, m0 = carry
    q, k, v, ii, lf = inp
    l_ = q.shape[2]
    b = jnp.cumsum(lf, axis=-1)
    m_t = b + jnp.maximum(m0[..., None], lax.cummax(ii - b, axis=2))
    inter = jnp.exp(b + m0[..., None] - m_t)
    causal = jnp.tril(jnp.ones((l_, l_), dtype=bool))
    log_d = b[..., :, None] - b[..., None, :] + ii[..., None, :] - m_t[..., :, None]
    dmat = jnp.exp(jnp.where(causal, log_d, -jnp.inf))
    sc = jnp.einsum('bhtd,bhsd->bhts', q, k) * dmat
    num = inter[..., None] * jnp.einsum('bhtd,bhde->bhte', q, c0) + jnp.einsum('bhts,bhse->bhte', sc, v)
    den = inter * jnp.einsum('bhtd,bhd->bht', q, n0) + jnp.sum(sc, axis=-1)
    h = num / jnp.maximum(jnp.abs(den), jnp.exp(-m_t))[..., None]
    m_last = m_t[..., -1]
    decay = jnp.exp(b[..., -1] + m0 - m_last)
    wk = jnp.exp(b[..., -1:] - b + ii - m_last[..., None])
    c_new = decay[..., None, None] * c0 + jnp.einsum('bhs,bhsd,bhse->bhde', wk, k, v)
    n_new = decay[..., None] * n0 + jnp.einsum('bhs,bhsd->bhd', wk, k)
    return (c_new, n_new, m_last), h


def _mlstm_inputs(qd, kd, vd, ig, fg, gate_bias):
    b_, t_, _ = qd.shape

    def heads(a):
        return a.reshape(b_, t_, N_HEADS_D, HEAD_DIM_D).transpose(0, 2, 1, 3).astype(jnp.float32)

    q = heads(qd)
    k = heads(kd) / math.sqrt(HEAD_DIM_D)
    v = heads(vd)
    ii = (ig.astype(jnp.float32) + gate_bias[0].astype(jnp.float32)).transpose(0, 2, 1)
    lf = jax.nn.log_sigmoid(fg.astype(jnp.float32) + gate_bias[1].astype(jnp.float32)).transpose(0, 2, 1)
    return q, k, v, ii, lf


def _mlstm_out(h, o_pre, dtype):
    b_, _, t_, _ = h.shape
    h = h.transpose(0, 2, 1, 3).reshape(b_, t_, MIX_D)
    return (h * jax.nn.sigmoid(o_pre.astype(jnp.float32))).astype(dtype)


def _even_prompt(x, w_in, pool_w, pool_scale, w_out, rel_bias):
    b_, s_, _ = x.shape
    q, k, v, ga, u, gb = _split(jnp.einsum('bsd,de->bse', x, w_in), EVEN_SPLITS)
    q = q.reshape(b_, s_, N_HEADS_A, HEAD_DIM)
    k = k.reshape(b_, s_, N_HEADS_A, HEAD_DIM)
    v = v.reshape(b_, s_, N_HEADS_A, HEAD_DIM)
    oa = _dilated_attn_prompt(q, k, v, rel_bias).reshape(b_, s_, MIX_A).astype(x.dtype)
    ob = _pool_mix(u, s_, pool_w, pool_scale).astype(x.dtype)
    mixed = jnp.concatenate([oa * jax.nn.silu(ga), ob * jax.nn.silu(gb)], axis=-1)
    y = jnp.einsum('bse,ed->bsd', mixed, w_out)
    keep_a = min(A_WINDOW, s_)
    keep_u = min(POOL_HIST, s_)
    return y, k[:, s_ - keep_a:], v[:, s_ - keep_a:], u[:, s_ - keep_u:]


def _even_sample(x, buf_k, buf_v, buf_u, past_len, w_in, pool_w, pool_scale, w_out, rel_bias):
    b_, t_, _ = x.shape
    q, k, v, ga, u, gb = _split(jnp.einsum('bsd,de->bse', x, w_in), EVEN_SPLITS)
    q = q.reshape(b_, t_, N_HEADS_A, HEAD_DIM)
    k_all = jnp.concatenate([buf_k.astype(k.dtype), k.reshape(b_, t_, N_HEADS_A, HEAD_DIM)], axis=1)
    v_all = jnp.concatenate([buf_v.astype(v.dtype), v.reshape(b_, t_, N_HEADS_A, HEAD_DIM)], axis=1)
    oa = _dilated_attn_sample(q, k_all, v_all, buf_k.shape[1], past_len, rel_bias)
    oa = oa.reshape(b_, t_, MIX_A).astype(x.dtype)
    u_all = jnp.concatenate([buf_u.astype(u.dtype), u], axis=1)
    ob = _pool_mix(u_all, t_, pool_w, pool_scale).astype(x.dtype)
    mixed = jnp.concatenate([oa * jax.nn.silu(ga), ob * jax.nn.silu(gb)], axis=-1)
    y = jnp.einsum('bse,ed->bsd', mixed, w_out)
    return y, k_all[:, t_:], v_all[:, t_:], u_all[:, t_:]


def _odd_prompt(x, w_in, sb_bias, gate_bias, w_out):
    b_, s_, _ = x.shape
    qc, kc, vc, gc, qd, kd, vd, od, gd, ig, fg = _split(jnp.einsum('bsd,de->bse', x, w_in), ODD_SPLITS)
    qc = qc.reshape(b_, s_, N_HEADS_C, HEAD_DIM)
    kc = kc.reshape(b_, s_, N_HEADS_C, HEAD_DIM)
    vc = vc.reshape(b_, s_, N_HEADS_C, HEAD_DIM)
    blocks = []
    for i in range(s_ // Q_BLOCK):
        lo, hi = i * Q_BLOCK, (i + 1) * Q_BLOCK
        blocks.append(_stick_breaking(qc[:, lo:hi], kc[:, :hi], vc[:, :hi], jnp.arange(lo, hi), jnp.arange(hi),
                                      sb_bias))
    oc = jnp.concatenate(blocks, axis=1).reshape(b_, s_, MIX_C)
    q, k, v, ii, lf = _mlstm_inputs(qd, kd, vd, ig, fg, gate_bias)
    n_chunk = s_ // MLSTM_CHUNK

    def chunks(a):
        a = a.reshape(a.shape[:2] + (n_chunk, MLSTM_CHUNK) + a.shape[3:])
        return jnp.moveaxis(a, 2, 0)

    init = (jnp.zeros((b_, N_HEADS_D, HEAD_DIM_D, HEAD_DIM_D), jnp.float32),
            jnp.zeros((b_, N_HEADS_D, HEAD_DIM_D), jnp.float32),
            jnp.zeros((b_, N_HEADS_D), jnp.float32))
    (c_f, n_f, m_f), hs = lax.scan(_mlstm_chunk, init, (chunks(q), chunks(k), chunks(v), chunks(ii), chunks(lf)))
    h = jnp.moveaxis(hs, 0, 2).reshape(b_, N_HEADS_D, s_, HEAD_DIM_D)
    odm = _mlstm_out(h, od, x.dtype)
    mixed = jnp.concatenate([oc * jax.nn.silu(gc), odm * jax.nn.silu(gd)], axis=-1)
    y = jnp.einsum('bse,ed->bsd', mixed, w_out)
    return y, kc, vc, c_f, n_f, m_f


def _odd_sample(x, pool_k, pool_v, page_table, c0, n0, m0, w_in, sb_bias, gate_bias, w_out):
    b_, t_, _ = x.shape
    past_len = page_table.shape[1] * PAGE_SIZE
    qc, kc, vc, gc, qd, kd, vd, od, gd, ig, fg = _split(jnp.einsum('bsd,de->bse', x, w_in), ODD_SPLITS)
    qc = qc.reshape(b_, t_, N_HEADS_C, HEAD_DIM)
    kc = kc.reshape(b_, t_, N_HEADS_C, HEAD_DIM)
    vc = vc.reshape(b_, t_, N_HEADS_C, HEAD_DIM)
    k_past = pool_k[page_table].reshape(b_, past_len, N_HEADS_C, HEAD_DIM)
    v_past = pool_v[page_table].reshape(b_, past_len, N_HEADS_C, HEAD_DIM)
    k_all = jnp.concatenate([k_past.astype(kc.dtype), kc], axis=1)
    v_all = jnp.concatenate([v_past.astype(vc.dtype), vc], axis=1)
    blocks = []
    for lo in range(0, t_, Q_BLOCK):
        hi = min(lo + Q_BLOCK, t_)
        n_keys = past_len + hi
        blocks.append(_stick_breaking(qc[:, lo:hi], k_all[:, :n_keys], v_all[:, :n_keys],
                                      past_len + jnp.arange(lo, hi), jnp.arange(n_keys), sb_bias))
    oc = jnp.concatenate(blocks, axis=1).reshape(b_, t_, MIX_C)
    q, k, v, ii, lf = _mlstm_inputs(qd, kd, vd, ig, fg, gate_bias)
    carry = (c0.astype(jnp.float32), n0.astype(jnp.float32), m0.astype(jnp.float32))
    (c_n, n_n, m_n), h = _mlstm_chunk(carry, (q, k, v, ii, lf))
    odm = _mlstm_out(h, od, x.dtype)
    mixed = jnp.concatenate([oc * jax.nn.silu(gc), odm * jax.nn.silu(gd)], axis=-1)
    y = jnp.einsum('bse,ed->bsd', mixed, w_out)
    return y, kc, vc, c_n, n_n, m_n


def setup_inputs(seed: int = 0) -> dict:
    key = jax.random.key(seed)
    ks = jax.random.split(key, 24)
    n_pages = PAST_LEN // PAGE_SIZE
    n_used = DEC_BATCH * n_pages
    n_pool = n_used + (n_used + 3) // 4
    wa = min(A_WINDOW, PAST_LEN)
    wp = min(POOL_HIST, PAST_LEN)

    def nrm(k, shape, s=1.0):
        return s * jax.random.normal(k, shape, jnp.float32)

    page_table = jax.random.permutation(ks[0], n_pool)[:n_used].reshape(DEC_BATCH, n_pages).astype(jnp.int32)
    gate_bias = jnp.stack([nrm(ks[1], (N_ODD, N_HEADS_D), 0.1),
                           jnp.linspace(3.0, 6.0, N_HEADS_D)[None, :] + nrm(ks[2], (N_ODD, N_HEADS_D), 0.1)], axis=1)
    sb_bias = jnp.linspace(-8.0, -5.0, N_HEADS_C)[None, :] + nrm(ks[22], (N_ODD, N_HEADS_C), 0.1)
    return dict(
        x_prompt=nrm(ks[3], (BATCH, SEQ, D_MODEL)),
        x_sample=nrm(ks[4], (DEC_BATCH, DEC_SEQ, D_MODEL)),
        cache_a_k=nrm(ks[5], (N_EVEN, DEC_BATCH, wa, N_HEADS_A, HEAD_DIM)),
        cache_a_v=nrm(ks[6], (N_EVEN, DEC_BATCH, wa, N_HEADS_A, HEAD_DIM)),
        state_pool=nrm(ks[7], (N_EVEN, DEC_BATCH, wp, MIX_B)),
        cache_c_k=nrm(ks[8], (N_ODD, n_pool, PAGE_SIZE, N_HEADS_C, HEAD_DIM)),
        cache_c_v=nrm(ks[9], (N_ODD, n_pool, PAGE_SIZE, N_HEADS_C, HEAD_DIM)),
        state_mlstm_c=nrm(ks[10], (N_ODD, DEC_BATCH, N_HEADS_D, HEAD_DIM_D, HEAD_DIM_D), HEAD_DIM_D ** -0.5),
        state_mlstm_n=nrm(ks[11], (N_ODD, DEC_BATCH, N_HEADS_D, HEAD_DIM_D), 0.1),
        state_mlstm_m=nrm(ks[12], (N_ODD, DEC_BATCH, N_HEADS_D), 0.5),
        page_table=page_table,
        rel_bias=nrm(ks[13], (N_REL_BUCKETS, N_HEADS_A), 0.5),
        w_in_even=nrm(ks[14], (N_EVEN, D_MODEL, sum(EVEN_SPLITS)), D_MODEL ** -0.5),
        pool_w=nrm(ks[15], (N_EVEN, N_POOL_GROUPS, POOL_GROUP, POOL_GROUP), POOL_GROUP ** -0.5),
        pool_scale=1.0 + nrm(ks[16], (N_EVEN, MIX_B), 0.02),
        w_out_even=nrm(ks[17], (N_EVEN, D_MODEL, D_MODEL), DEEPNORM_BETA * D_MODEL ** -0.5),
        w_in_odd=nrm(ks[18], (N_ODD, D_MODEL, sum(ODD_SPLITS)), D_MODEL ** -0.5),
        sb_bias=sb_bias,
        mlstm_gate_bias=gate_bias,
        w_out_odd=nrm(ks[19], (N_ODD, D_MODEL, D_MODEL), DEEPNORM_BETA * D_MODEL ** -0.5),
        ln_g=1.0 + nrm(ks[20], (DEPTH, D_MODEL), 0.02),
        ln_b=nrm(ks[21], (DEPTH, D_MODEL), 0.02),
    )


def reference(x_prompt, x_sample, cache_a_k, cache_a_v, state_pool, cache_c_k, cache_c_v,
              state_mlstm_c, state_mlstm_n, state_mlstm_m, page_table, rel_bias, w_in_even, pool_w,
              pool_scale, w_out_even, w_in_odd, sb_bias, mlstm_gate_bias, w_out_odd, ln_g, ln_b):
    past_len = page_table.shape[1] * PAGE_SIZE
    xp, xs = x_prompt, x_sample
    akp, avp, aks, avs, pup, pus = [], [], [], [], [], []
    ckp, cvp, cks, cvs = [], [], [], []
    mcp, mnp, mmp, mcs, mns, mms = [], [], [], [], [], []
    for layer in range(DEPTH):
        j = layer // 2
        if layer % 2 == 0:
            yp, k_p, v_p, u_p = _even_prompt(xp, w_in_even[j], pool_w[j], pool_scale[j], w_out_even[j], rel_bias)
            ys, k_s, v_s, u_s = _even_sample(xs, cache_a_k[j], cache_a_v[j], state_pool[j], past_len,
                                             w_in_even[j], pool_w[j], pool_scale[j], w_out_even[j], rel_bias)
            akp.append(k_p)
            avp.append(v_p)
            aks.append(k_s)
            avs.append(v_s)
            pup.append(u_p)
            pus.append(u_s)
        else:
            yp, k_p, v_p, c_p, n_p, m_p = _odd_prompt(xp, w_in_odd[j], sb_bias[j], mlstm_gate_bias[j], w_out_odd[j])
            ys, k_s, v_s, c_s, n_s, m_s = _odd_sample(xs, cache_c_k[j], cache_c_v[j], page_table,
                                                      state_mlstm_c[j], state_mlstm_n[j], state_mlstm_m[j],
                                                      w_in_odd[j], sb_bias[j], mlstm_gate_bias[j], w_out_odd[j])
            ckp.append(k_p)
            cvp.append(v_p)
            cks.append(k_s)
            cvs.append(v_s)
            mcp.append(c_p)
            mnp.append(n_p)
            mmp.append(m_p)
            mcs.append(c_s)
            mns.append(n_s)
            mms.append(m_s)
        xp = _layer_norm(DEEPNORM_ALPHA * xp + yp, ln_g[layer], ln_b[layer])
        xs = _layer_norm(DEEPNORM_ALPHA * xs + ys, ln_g[layer], ln_b[layer])
    a_k_prompt = jnp.stack(akp)
    a_v_prompt = jnp.stack(avp)
    a_k_sample = jnp.stack(aks)
    a_v_sample = jnp.stack(avs)
    pool_prompt = jnp.stack(pup)
    pool_sample = jnp.stack(pus)
    c_k_prompt = jnp.stack(ckp)
    c_v_prompt = jnp.stack(cvp)
    c_k_sample = jnp.stack(cks)
    c_v_sample = jnp.stack(cvs)
    mlstm_c_prompt = jnp.stack(mcp)
    mlstm_n_prompt = jnp.stack(mnp)
    mlstm_m_prompt = jnp.stack(mmp)
    mlstm_c_sample = jnp.stack(mcs)
    mlstm_n_sample = jnp.stack(mns)
    mlstm_m_sample = jnp.stack(mms)
    return (xp, xs, a_k_prompt, a_v_prompt, a_k_sample, a_v_sample, pool_prompt, pool_sample,
            c_k_prompt, c_v_prompt, c_k_sample, c_v_sample,
            mlstm_c_prompt, mlstm_n_prompt, mlstm_m_prompt, mlstm_c_sample, mlstm_n_sample, mlstm_m_sample)
```

```python
import functools
import math

import numpy as np
import jax
import jax.numpy as jnp
from jax import lax
from jax.experimental import pallas as pl
from jax.experimental.pallas import tpu as pltpu

HEAD_DIM = 128
DILATIONS = ((128, 1), (512, 4), (2048, 16))
A_WINDOW = max(w for w, _ in DILATIONS)
BAND = 128
POOL_WINDOWS = (2, 4, 8, 16)
POOL_HIST = max(POOL_WINDOWS) - 1
N_HEADS_D = 4
MLSTM_CHUNK = 128
N_REL_BUCKETS = 32
REL_MAX_DISTANCE = A_WINDOW
LN_EPS = 1e-5
PAGE_SIZE = 128
NEG = -1e30

V7X_VMEM_BYTES = 64 * 1024 * 1024
VMEM_LIMIT = 56 * 1024 * 1024
LANES = 128

BF16 = jnp.bfloat16
F32 = jnp.float32


def _params(*sem):
    return pltpu.CompilerParams(dimension_semantics=sem, vmem_limit_bytes=VMEM_LIMIT)


def _silu(x):
    return x * jax.nn.sigmoid(x)


def _dot(a, b):
    return jnp.dot(a, b, preferred_element_type=F32)


def _dot_nt(a, b):
    return lax.dot_general(a, b, (((1,), (1,)), ((), ())), preferred_element_type=F32)


def _dot_tn(a, b):
    return lax.dot_general(a, b, (((0,), (0,)), ((), ())), preferred_element_type=F32)


def _split_hi_lo(x):
    hi = x.astype(BF16)
    lo = (x - hi.astype(F32)).astype(BF16)
    return hi, lo


def _mm_kernel(x_ref, w_ref, o_ref):
    o_ref[...] = _dot(x_ref[...], w_ref[...]).astype(o_ref.dtype)


def _matmul(x, w, *, tm, tn, out_dtype=F32):
    m, k = x.shape
    n = w.shape[1]
    tm = min(tm, m)
    assert m % tm == 0 and n % tn == 0, (m, n, tm, tn)
    return pl.pallas_call(
        _mm_kernel,
        grid=(m // tm, n // tn),
        in_specs=[pl.BlockSpec((tm, k), lambda i, j: (i, 0)),
                  pl.BlockSpec((k, tn), lambda i, j: (0, j))],
        out_specs=pl.BlockSpec((tm, tn), lambda i, j: (i, j)),
        out_shape=jax.ShapeDtypeStruct((m, n), out_dtype),
        compiler_params=_params("parallel", "parallel"),
        name="proj",
    )(x, w)


def _mm2_kernel(xa_ref, xb_ref, w_ref, o_ref):
    ka = xa_ref.shape[1]
    o_ref[...] = _dot(xa_ref[...], w_ref[:ka, :]) + _dot(xb_ref[...], w_ref[ka:, :])


def _matmul2(xa, xb, w, *, tm, tn):
    m, ka = xa.shape
    kb = xb.shape[1]
    n = w.shape[1]
    tm = min(tm, m)
    assert m % tm == 0 and n % tn == 0 and w.shape[0] == ka + kb
    return pl.pallas_call(
        _mm2_kernel,
        grid=(m // tm, n // tn),
        in_specs=[pl.BlockSpec((tm, ka), lambda i, j: (i, 0)),
                  pl.BlockSpec((tm, kb), lambda i, j: (i, 0)),
                  pl.BlockSpec((ka + kb, tn), lambda i, j: (0, j))],
        out_specs=pl.BlockSpec((tm, tn), lambda i, j: (i, j)),
        out_shape=jax.ShapeDtypeStruct((m, n), F32),
        compiler_params=_params("parallel", "parallel"),
        name="out_proj",
    )(xa, xb, w)


def _ln_kernel(x_ref, y_ref, g_ref, b_ref, of_ref, ob_ref, *, alpha):
    h = alpha * x_ref[...] + y_ref[...]
    mu = jnp.mean(h, axis=-1, keepdims=True)
    c = h - mu
    var = jnp.mean(c * c, axis=-1, keepdims=True)
    r = c * lax.rsqrt(var + LN_EPS) * g_ref[...] + b_ref[...]
    of_ref[...] = r
    ob_ref[...] = r.astype(BF16)


def _residual_ln(x, y, g, b, *, alpha, tm):
    m, d = x.shape
    tm = min(tm, m)
    assert m % tm == 0
    row = pl.BlockSpec((tm, d), lambda i: (i, 0))
    vec = pl.BlockSpec((1, d), lambda i: (0, 0))
    return pl.pallas_call(
        functools.partial(_ln_kernel, alpha=alpha),
        grid=(m // tm,),
        in_specs=[row, row, vec, vec],
        out_specs=[row, row],
        out_shape=[jax.ShapeDtypeStruct((m, d), F32), jax.ShapeDtypeStruct((m, d), BF16)],
        compiler_params=_params("parallel"),
        name="residual_ln",
    )(x, y, g.reshape(1, d), b.reshape(1, d))


def _rel_bucket_static(dist):
    exact = N_REL_BUCKETS // 2
    dist = np.asarray(dist, np.int64)
    ratio = np.log(np.maximum(dist, exact) / exact) / math.log(REL_MAX_DISTANCE / exact)
    large = np.minimum(exact + (ratio * (N_REL_BUCKETS - exact)).astype(np.int64), N_REL_BUCKETS - 1)
    return np.where(dist < exact, dist, large).astype(np.int32)


def _band_bias(rel_bias):
    i_loc = np.arange(BAND)[:, None]
    j_loc = np.arange(2 * BAND)[None, :]
    m_loc = i_loc + BAND - j_loc
    out = []
    for window, dil in DILATIONS:
        steps = window // dil
        valid = (m_loc >= 0) & (m_loc <= steps)
        bucket = _rel_bucket_static(dil * np.clip(m_loc, 0, steps))
        bias = jnp.where(jnp.asarray(valid)[..., None], rel_bias[jnp.asarray(bucket)].astype(F32), NEG)
        out.append(jnp.transpose(bias, (2, 0, 1)))
    return jnp.stack(out)


def _sample_bias(rel_bias, n_hist, past_len):
    hist, new = [], []
    for window, dil in DILATIONS:
        steps = window // dil
        m = steps - np.arange(steps)
        idx = n_hist - dil * m
        valid = (past_len - dil * m >= 0) & (idx >= 0)
        bias = rel_bias[jnp.asarray(_rel_bucket_static(dil * m))].astype(F32)
        hist.append(jnp.where(jnp.asarray(valid)[:, None], bias, NEG))
        new.append(rel_bias[int(_rel_bucket_static(0))].astype(F32))
    return jnp.stack(hist), jnp.stack(new)


def _segment_matrices(n_heads):
    seg = np.kron(np.eye(n_heads, dtype=np.float32), np.ones((HEAD_DIM, 1), np.float32))
    return jnp.asarray(seg, BF16), jnp.asarray(seg.T, BF16)


def _dil_prompt_kernel(q_ref, k_ref, v_ref, g_ref, bm_ref, o_ref, og_ref, lse_ref, *, seq):
    scale = 1.0 / math.sqrt(HEAD_DIM)

    def rows(ref, start, n, dil):
        if dil == 1:
            return ref[pl.ds(start, n), :]
        return ref[pl.ds(start, n, stride=dil), :]

    def block(g, dil, base, first):
        qs = rows(q_ref, base, BAND, dil).astype(BF16)
        if first:
            ks = rows(k_ref, base, BAND, dil).astype(BF16)
            vs = rows(v_ref, base, BAND, dil).astype(BF16)
            bias = bm_ref[g, :, BAND:]
        else:
            ks = rows(k_ref, base - dil * BAND, 2 * BAND, dil).astype(BF16)
            vs = rows(v_ref, base - dil * BAND, 2 * BAND, dil).astype(BF16)
            bias = bm_ref[g]
        s = _dot_nt(qs, ks) * scale + bias
        mx = jnp.max(s, axis=-1, keepdims=True)
        p = jnp.exp(s - mx)
        l = jnp.sum(p, axis=-1, keepdims=True)
        o = _dot(p.astype(BF16), vs) / l
        lse = jnp.broadcast_to(mx + jnp.log(l), (BAND, HEAD_DIM))
        if dil == 1:
            og_ref[g, pl.ds(base, BAND), :] = o
            lse_ref[g, pl.ds(base, BAND), :] = lse
        else:
            og_ref[g, pl.ds(base, BAND, stride=dil), :] = o
            lse_ref[g, pl.ds(base, BAND, stride=dil), :] = lse

    for g, (window, dil) in enumerate(DILATIONS):
        n_blk = seq // dil // BAND

        def first_body(r, carry, g=g, dil=dil):
            block(g, dil, r, True)
            return carry

        lax.fori_loop(0, dil, first_body, 0)
        if n_blk > 1:
            def rest_body(t, carry, g=g, dil=dil, n_blk=n_blk):
                r = t // (n_blk - 1)
                i = t % (n_blk - 1) + 1
                block(g, dil, r + dil * BAND * i, False)
                return carry

            lax.fori_loop(0, dil * (n_blk - 1), rest_body, 0)

    tile = 256

    def merge(t, carry):
        sl = pl.ds(pl.multiple_of(t * tile, tile), tile)
        lses = [lse_ref[g, sl, :] for g in range(len(DILATIONS))]
        mx = functools.reduce(jnp.maximum, lses)
        ws = [jnp.exp(l - mx) for l in lses]
        den = functools.reduce(jnp.add, ws)
        num = functools.reduce(jnp.add, [w * og_ref[g, sl, :] for g, w in enumerate(ws)])
        o_ref[sl, :] = (num / den * _silu(g_ref[sl, :])).astype(o_ref.dtype)
        return carry

    lax.fori_loop(0, seq // tile, merge, 0)


def _dil_prompt(z3, band_bias, n_heads):
    bsz, seq, _ = z3.shape
    n_pat = len(DILATIONS)
    for window, dil in DILATIONS:
        assert seq % (dil * BAND) == 0 and window // dil == BAND

    def col(off):
        return pl.BlockSpec((None, seq, HEAD_DIM), lambda b, h, off=off: (b, 0, off * n_heads + h))

    return pl.pallas_call(
        functools.partial(_dil_prompt_kernel, seq=seq),
        grid=(bsz, n_heads),
        in_specs=[col(0), col(1), col(2), col(3),
                  pl.BlockSpec((n_pat, None, BAND, 2 * BAND), lambda b, h: (0, h, 0, 0))],
        out_specs=pl.BlockSpec((None, seq, HEAD_DIM), lambda b, h: (b, 0, h)),
        out_shape=jax.ShapeDtypeStruct((bsz, seq, n_heads * HEAD_DIM), BF16),
        scratch_shapes=[pltpu.VMEM((n_pat, seq, HEAD_DIM), F32), pltpu.VMEM((n_pat, seq, HEAD_DIM), F32)],
        compiler_params=_params("parallel", "parallel"),
        name="dilated_attn_prompt",
    )(z3, z3, z3, z3, band_bias)


POOL_HALO = 16


def _pool_prompt_kernel(u_ref, halo_ref, gate_ref, pw_ref, ps_ref, o_ref, *, tile):
    t = pl.program_id(1)
    grp = pw_ref.shape[1]
    halo = jnp.where(t > 0, halo_ref[...], 0.0)
    pos = (t * tile + lax.broadcasted_iota(jnp.int32, (tile, 1), 0) + 1).astype(F32)
    for g, win in enumerate(POOL_WINDOWS):
        cols = slice(g * grp, (g + 1) * grp)
        x = u_ref[:, cols]
        ext = jnp.concatenate([halo[:, cols], x], axis=0)
        shift = 1
        while shift < win:
            ext = ext + pltpu.roll(ext, shift, 0)
            shift *= 2
        wsum = ext[POOL_HALO:, :]
        pooled = wsum / jnp.minimum(pos, float(win)) - x
        y = _dot(pooled.astype(BF16), pw_ref[g]) * ps_ref[:, cols]
        o_ref[:, cols] = (y * _silu(gate_ref[:, cols])).astype(o_ref.dtype)


def _pool_prompt(z3, pool_w, pool_scale, *, u_off, gate_off, tile=256):
    bsz, seq, _ = z3.shape
    n_grp, grp, _ = pool_w.shape
    width = n_grp * grp
    assert seq % tile == 0 and tile % POOL_HALO == 0
    for win in POOL_WINDOWS:
        assert win & (win - 1) == 0 and win - 1 <= POOL_HALO
    per = tile // POOL_HALO
    return pl.pallas_call(
        functools.partial(_pool_prompt_kernel, tile=tile),
        grid=(bsz, seq // tile),
        in_specs=[pl.BlockSpec((None, tile, width), lambda b, t: (b, t, u_off)),
                  pl.BlockSpec((None, POOL_HALO, width),
                               lambda b, t: (b, jnp.maximum(t * per - 1, 0), u_off)),
                  pl.BlockSpec((None, tile, width), lambda b, t: (b, t, gate_off)),
                  pl.BlockSpec((n_grp, grp, grp), lambda b, t: (0, 0, 0)),
                  pl.BlockSpec((1, width), lambda b, t: (0, 0))],
        out_specs=pl.BlockSpec((None, tile, width), lambda b, t: (b, t, 0)),
        out_shape=jax.ShapeDtypeStruct((bsz, seq, width), BF16),
        compiler_params=_params("parallel", "parallel"),
        name="pool_mix_prompt",
    )(z3, z3, z3, pool_w, pool_scale.reshape(1, width))


SB_TILE = 256


def _softplus(z):
    return jnp.maximum(z, 0.0) + jnp.log1p(jnp.exp(-jnp.abs(z)))


def _log_sigmoid(z):
    return -_softplus(-z)


def _sb_prompt_kernel(q_ref, k_ref, v_ref, g_ref, bias_ref, uu_ref, o_ref):
    i = pl.program_id(2)
    scale = 1.0 / math.sqrt(HEAD_DIM)
    q = q_ref[...].astype(BF16)
    bias = bias_ref[:, 0:1]
    row = lax.broadcasted_iota(jnp.int32, (SB_TILE, SB_TILE), 0)
    colid = lax.broadcasted_iota(jnp.int32, (SB_TILE, SB_TILE), 1)

    def key_block(kb, carry, acc, diag):
        sl = pl.ds(pl.multiple_of(kb * SB_TILE, SB_TILE), SB_TILE)
        ks = k_ref[sl, :].astype(BF16)
        vs = v_ref[sl, :].astype(BF16)
        z = _dot_nt(q, ks) * scale + bias
        sp = _softplus(z)
        log_rest = -sp
        log_take = z - sp
        if diag:
            mask = colid < row
            log_rest = jnp.where(mask, log_rest, 0.0)
        hi, lo = _split_hi_lo(log_rest)
        suffix = _dot(jnp.concatenate([hi, lo], axis=1), uu_ref[...])
        a = jnp.exp(log_take + suffix + carry)
        if diag:
            a = jnp.where(mask, a, 0.0)
        acc = acc + _dot(a.astype(BF16), vs)
        carry = carry + jnp.sum(log_rest, axis=-1, keepdims=True)
        return carry, acc

    carry = jnp.zeros((SB_TILE, 1), F32)
    acc = jnp.zeros((SB_TILE, HEAD_DIM), F32)
    carry, acc = key_block(i, carry, acc, True)

    def body(n, ca):
        return key_block(i - 1 - n, ca[0], ca[1], False)

    carry, acc = lax.fori_loop(0, i, body, (carry, acc))
    o_ref[...] = (acc * _silu(g_ref[...])).astype(o_ref.dtype)


def _suffix_matrix(n):
    j = np.arange(n)[:, None]
    s = np.arange(n)[None, :]
    return (j > s).astype(np.float32)


def _sb_prompt(z3, sb_bias, n_heads):
    bsz, seq, _ = z3.shape
    assert seq % SB_TILE == 0
    u = _suffix_matrix(SB_TILE)
    uu = jnp.asarray(np.concatenate([u, u], axis=0), BF16)
    bias = jnp.broadcast_to(sb_bias.astype(F32)[:, None, None], (n_heads, 1, LANES))

    def qcol(off):
        return pl.BlockSpec((None, SB_TILE, HEAD_DIM), lambda b, h, i, off=off: (b, i, off * n_heads + h))

    def kcol(off):
        return pl.BlockSpec((None, seq, HEAD_DIM), lambda b, h, i, off=off: (b, 0, off * n_heads + h))

    return pl.pallas_call(
        _sb_prompt_kernel,
        grid=(bsz, n_heads, seq // SB_TILE),
        in_specs=[qcol(0), kcol(1), kcol(2), qcol(3),
                  pl.BlockSpec((None, 1, LANES), lambda b, h, i: (h, 0, 0)),
                  pl.BlockSpec((2 * SB_TILE, SB_TILE), lambda b, h, i: (0, 0))],
        out_specs=pl.BlockSpec((None, SB_TILE, HEAD_DIM), lambda b, h, i: (b, i, h)),
        out_shape=jax.ShapeDtypeStruct((bsz, seq, n_heads * HEAD_DIM), BF16),
        compiler_params=_params("parallel", "parallel", "arbitrary"),
        name="stick_breaking_prompt",
    )(z3, z3, z3, z3, bias, uu)


def _mlstm_prompt_kernel(q_ref, k_ref, v_ref, o_ref, g_ref, gc_ref, gr_ref, bc_ref, br_ref,
                         mix_ref, c_ref, n_ref, m_ref):
    chunk = pl.program_id(1)
    n_heads, dh, _ = c_ref.shape
    L = q_ref.shape[0]
    kscale = 1.0 / math.sqrt(dh)

    @pl.when(chunk == 0)
    def _():
        c_ref[...] = jnp.zeros_like(c_ref)
        n_ref[...] = jnp.zeros_like(n_ref)
        m_ref[...] = jnp.zeros_like(m_ref)

    t_idx = lax.broadcasted_iota(jnp.int32, (L, L), 0)
    s_idx = lax.broadcasted_iota(jnp.int32, (L, L), 1)
    causal = s_idx <= t_idx
    gates_c = gc_ref[...] + br_ref[...]
    gates_r = gr_ref[...] + bc_ref[...]
    for h in range(n_heads):
        cols = slice(h * dh, (h + 1) * dh)
        ii_c = gates_c[:, h:h + 1]
        lf_c = _log_sigmoid(gates_c[:, n_heads + h:n_heads + h + 1])
        ii_r = gates_r[h:h + 1, :]
        lf_r = _log_sigmoid(gates_r[n_heads + h:n_heads + h + 1, :])
        b_c = jnp.sum(jnp.where(causal, lf_r, 0.0), axis=1, keepdims=True)
        b_r = jnp.sum(jnp.where(t_idx <= s_idx, lf_c, 0.0), axis=0, keepdims=True)
        g_r = ii_r - b_r
        cm_c = jnp.max(jnp.where(causal, g_r, -jnp.inf), axis=1, keepdims=True)
        m0 = m_ref[h:h + 1, 0:1]
        m_t = b_c + jnp.maximum(m0, cm_c)
        inter = jnp.exp(b_c + m0 - m_t)
        dmat = jnp.where(causal, jnp.exp((b_c - m_t) + g_r), 0.0)
        q = q_ref[:, cols].astype(BF16)
        ks = k_ref[:, cols] * kscale
        v = v_ref[:, cols].astype(BF16)
        c0 = c_ref[h]
        n0 = n_ref[h:h + 1, :]
        sc = _dot_nt(q, ks.astype(BF16)) * dmat
        num = inter * _dot(q, c0.astype(BF16)) + _dot(sc.astype(BF16), v)
        qn = jnp.sum(q_ref[:, cols] * n0, axis=1, keepdims=True)
        den = inter * qn + jnp.sum(sc, axis=1, keepdims=True)
        hid = num / jnp.maximum(jnp.abs(den), jnp.exp(-m_t))
        m_last = m_t[L - 1:L, :]
        b_last = b_c[L - 1:L, :]
        decay = jnp.exp(b_last + m0 - m_last)
        wk = jnp.exp(b_last - b_c + ii_c - m_last)
        kw = ks * wk
        c_ref[h] = decay * c0 + _dot_tn(kw.astype(BF16), v)
        n_ref[h:h + 1, :] = decay * n0 + jnp.sum(kw, axis=0, keepdims=True)
        m_ref[h:h + 1, :] = jnp.broadcast_to(m_last, (1, m_ref.shape[1]))
        out = hid * jax.nn.sigmoid(o_ref[:, cols]) * _silu(g_ref[:, cols])
        mix_ref[:, cols] = out.astype(mix_ref.dtype)


def _mlstm_prompt(z3, zg3, gate_bias, *, col0, n_heads=N_HEADS_D, chunk=MLSTM_CHUNK):
    bsz, seq, _ = z3.shape
    width = z3.shape[2] - col0
    width //= 5
    dh = width // n_heads
    assert seq % chunk == 0 and col0 % width == 0 and 2 * n_heads <= 8
    base = col0 // width
    zg_rows = jnp.transpose(zg3[:, :, :8], (0, 2, 1))
    bias_lane = jnp.zeros((1, LANES), F32).at[0, :2 * n_heads].set(gate_bias.astype(F32).reshape(-1))
    bias_sub = bias_lane[0, :8].reshape(8, 1)

    def col(off):
        return pl.BlockSpec((None, chunk, width), lambda b, c, off=off: (b, c, base + off))

    mix, c_f, n_f, m_f = pl.pallas_call(
        _mlstm_prompt_kernel,
        grid=(bsz, seq // chunk),
        in_specs=[col(0), col(1), col(2), col(3), col(4),
                  pl.BlockSpec((None, chunk, LANES), lambda b, c: (b, c, 0)),
                  pl.BlockSpec((None, 8, chunk), lambda b, c: (b, 0, c)),
                  pl.BlockSpec((8, 1), lambda b, c: (0, 0)),
                  pl.BlockSpec((1, LANES), lambda b, c: (0, 0))],
        out_specs=[pl.BlockSpec((None, chunk, width), lambda b, c: (b, c, 0)),
                   pl.BlockSpec((None, n_heads, dh, dh), lambda b, c: (b, 0, 0, 0)),
                   pl.BlockSpec((None, n_heads, dh), lambda b, c: (b, 0, 0)),
                   pl.BlockSpec((None, n_heads, LANES), lambda b, c: (b, 0, 0))],
        out_shape=[jax.ShapeDtypeStruct((bsz, seq, width), BF16),
                   jax.ShapeDtypeStruct((bsz, n_heads, dh, dh), F32),
                   jax.ShapeDtypeStruct((bsz, n_heads, dh), F32),
                   jax.ShapeDtypeStruct((bsz, n_heads, LANES), F32)],
        compiler_params=_params("parallel", "arbitrary"),
        name="mlstm_prompt",
    )(z3, z3, z3, z3, z3, zg3, zg_rows, bias_sub, bias_lane)
    return mix, c_f, n_f, m_f[:, :, 0]


def _dil_sample_kernel(*refs, n_pat):
    q_ref, kn_ref, vn_ref, g_ref, bh_ref, bn_ref, seg_ref, segt_ref = refs[:8]
    k_refs = refs[8:8 + n_pat]
    v_refs = refs[8 + n_pat:8 + 2 * n_pat]
    o_ref = refs[8 + 2 * n_pat]
    scale = 1.0 / math.sqrt(HEAD_DIM)
    q = q_ref[...]
    seg = seg_ref[...]
    zn = _dot((kn_ref[...] * q).astype(BF16), seg) * scale
    z_new = zn + bn_ref[...]
    z_hist = [_dot((k_refs[g][...] * q).astype(BF16), seg) * scale + bh_ref[g] for g in range(n_pat)]
    mx = jnp.max(z_new, axis=0, keepdims=True)
    for z in z_hist:
        mx = jnp.maximum(mx, jnp.max(z, axis=0, keepdims=True))
    w_new = jnp.sum(jnp.exp(z_new - mx), axis=0, keepdims=True)
    w_hist = [jnp.exp(z - mx) for z in z_hist]
    den = w_new
    for w in w_hist:
        den = den + jnp.sum(w, axis=0, keepdims=True)
    segt = segt_ref[...]
    acc = _dot((w_new / den).astype(BF16), segt) * vn_ref[...]
    for g in range(n_pat):
        pe = _dot((w_hist[g] / den).astype(BF16), segt)
        acc = acc + jnp.sum(pe * v_refs[g][...], axis=0, keepdims=True)
    o_ref[...] = (acc * _silu(g_ref[...])).astype(o_ref.dtype)


def _dil_sample(zs, buf_k, buf_v, rel_bias, past_len, n_heads):
    bsz, n_hist = buf_k.shape[:2]
    width = n_heads * HEAD_DIM
    n_pat = len(DILATIONS)
    bias_hist, bias_new = _sample_bias(rel_bias, n_hist, past_len)
    seg, segt = _segment_matrices(n_heads)
    steps = BAND
    z3 = zs.reshape(bsz, 1, -1)
    k_views, v_views, kv_specs = [], [], []
    for window, dil in DILATIONS:
        assert window // dil == steps and n_hist % window == 0 and window % dil == 0
        k_views.append(buf_k.reshape(bsz, n_hist // dil, dil * width))
        v_views.append(buf_v.reshape(bsz, n_hist // dil, dil * width))
        blk = n_hist // window - 1
        kv_specs.append(pl.BlockSpec((None, steps, width), lambda b, blk=blk: (b, blk, 0)))

    def zcol(off):
        return pl.BlockSpec((None, 1, width), lambda b, off=off: (b, 0, off))

    out = pl.pallas_call(
        functools.partial(_dil_sample_kernel, n_pat=n_pat),
        grid=(bsz,),
        in_specs=[zcol(0), zcol(1), zcol(2), zcol(3),
                  pl.BlockSpec((n_pat, steps, n_heads), lambda b: (0, 0, 0)),
                  pl.BlockSpec((n_pat, n_heads), lambda b: (0, 0)),
                  pl.BlockSpec((width, n_heads), lambda b: (0, 0)),
                  pl.BlockSpec((n_heads, width), lambda b: (0, 0))] + kv_specs + kv_specs,
        out_specs=pl.BlockSpec((None, 1, width), lambda b: (b, 0, 0)),
        out_shape=jax.ShapeDtypeStruct((bsz, 1, width), BF16),
        compiler_params=_params("parallel"),
        name="dilated_attn_sample",
    )(z3, z3, z3, z3, bias_hist, bias_new, seg, segt, *k_views, *v_views)
    return out.reshape(bsz, width)


def _pool_sample_kernel(u_ref, hist_ref, gate_ref, pw_ref, ps_ref, o_ref):
    grp = pw_ref.shape[1]
    n_hist = hist_ref.shape[0]
    for g, win in enumerate(POOL_WINDOWS):
        cols = slice(g * grp, (g + 1) * grp)
        x = u_ref[:, cols]
        n_old = min(win - 1, n_hist)
        wsum = x
        for r in range(n_hist - n_old, n_hist):
            wsum = wsum + hist_ref[r, :, cols]
        pooled = wsum / float(n_old + 1) - x
        y = _dot(pooled.astype(BF16), pw_ref[g]) * ps_ref[:, cols]
        o_ref[:, cols] = (y * _silu(gate_ref[:, cols])).astype(o_ref.dtype)


def _pool_sample(zs, hist, pool_w, pool_scale, *, u_off, gate_off):
    bsz = zs.shape[0]
    n_grp, grp, _ = pool_w.shape
    width = n_grp * grp
    n_hist = hist.shape[1]
    hist = jnp.transpose(hist, (1, 0, 2))
    return pl.pallas_call(
        _pool_sample_kernel,
        grid=(1,),
        in_specs=[pl.BlockSpec((bsz, width), lambda i: (0, u_off)),
                  pl.BlockSpec((n_hist, bsz, width), lambda i: (0, 0, 0)),
                  pl.BlockSpec((bsz, width), lambda i: (0, gate_off)),
                  pl.BlockSpec((n_grp, grp, grp), lambda i: (0, 0, 0)),
                  pl.BlockSpec((1, width), lambda i: (0, 0))],
        out_specs=pl.BlockSpec((bsz, width), lambda i: (0, 0)),
        out_shape=jax.ShapeDtypeStruct((bsz, width), BF16),
        compiler_params=_params("arbitrary"),
        name="pool_mix_sample",
    )(zs, hist, zs, pool_w, pool_scale.reshape(1, width))


def _sb_sample_kernel(pt_ref, q_ref, g_ref, bias_ref, seg_ref, segt_ref, ut_ref, k_ref, v_ref, o_ref,
                      carry_sc, acc_sc):
    p = pl.program_id(1)
    scale = 1.0 / math.sqrt(HEAD_DIM)

    @pl.when(p == 0)
    def _():
        carry_sc[...] = jnp.zeros_like(carry_sc)
        acc_sc[...] = jnp.zeros_like(acc_sc)

    q = q_ref[...]
    z = _dot((k_ref[...] * q).astype(BF16), seg_ref[...]) * scale + bias_ref[...]
    sp = _softplus(z)
    log_rest = -sp
    log_take = z - sp
    hi, lo = _split_hi_lo(log_rest)
    suffix = _dot(ut_ref[...], jnp.concatenate([hi, lo], axis=0))
    a = jnp.exp(log_take + suffix + carry_sc[...])
    contrib = _dot(a.astype(BF16), segt_ref[...]) * v_ref[...]
    page = contrib.shape[0]
    part = contrib[0:8, :]
    for r in range(1, page // 8):
        part = part + contrib[r * 8:(r + 1) * 8, :]
    acc_sc[...] += part
    carry_sc[...] += jnp.sum(log_rest, axis=0, keepdims=True)

    @pl.when(p == pl.num_programs(1) - 1)
    def _():
        o = jnp.sum(acc_sc[...], axis=0, keepdims=True)
        o_ref[...] = (o * _silu(g_ref[...])).astype(o_ref.dtype)


def _sb_sample(zs, pool_k, pool_v, page_table, sb_bias, n_heads):
    bsz, n_pages = page_table.shape
    n_pool, page = pool_k.shape[:2]
    width = n_heads * HEAD_DIM
    seg, segt = _segment_matrices(n_heads)
    u = _suffix_matrix(page).T
    ut = jnp.asarray(np.concatenate([u, u], axis=1), BF16)
    z3 = zs.reshape(bsz, 1, -1)
    kp = pool_k.reshape(n_pool, page, width)
    vp = pool_v.reshape(n_pool, page, width)
    bias = sb_bias.astype(F32).reshape(1, n_heads)

    def page_spec():
        return pl.BlockSpec((None, page, width), lambda b, p, pt: (pt[b, n_pages - 1 - p], 0, 0))

    out = pl.pallas_call(
        _sb_sample_kernel,
        grid_spec=pltpu.PrefetchScalarGridSpec(
            num_scalar_prefetch=1,
            grid=(bsz, n_pages),
            in_specs=[pl.BlockSpec((None, 1, width), lambda b, p, pt: (b, 0, 0)),
                      pl.BlockSpec((None, 1, width), lambda b, p, pt: (b, 0, 3)),
                      pl.BlockSpec((1, n_heads), lambda b, p, pt: (0, 0)),
                      pl.BlockSpec((width, n_heads), lambda b, p, pt: (0, 0)),
                      pl.BlockSpec((n_heads, width), lambda b, p, pt: (0, 0)),
                      pl.BlockSpec((page, 2 * page), lambda b, p, pt: (0, 0)),
                      page_spec(), page_spec()],
            out_specs=pl.BlockSpec((None, 1, width), lambda b, p, pt: (b, 0, 0)),
            scratch_shapes=[pltpu.VMEM((1, n_heads), F32), pltpu.VMEM((8, width), F32)]),
        out_shape=jax.ShapeDtypeStruct((bsz, 1, width), BF16),
        compiler_params=_params("parallel", "arbitrary"),
        name="stick_breaking_sample",
    )(page_table, z3, z3, bias, seg, segt, ut, kp, vp)
    return out.reshape(bsz, width)


def _mlstm_sample_kernel(q_ref, k_ref, v_ref, o_ref, g_ref, sc_ref, c0_ref, n0_ref,
                         mix_ref, c_ref, n_ref, m_ref):
    dh = c0_ref.shape[0]
    kscale = 1.0 / math.sqrt(dh)
    s = sc_ref[...]
    ii = s[:, 0:1] + s[:, 2:3]
    lf = _log_sigmoid(s[:, 1:2] + s[:, 3:4])
    m0 = s[:, 4:5]
    m_t = jnp.maximum(lf + m0, ii)
    inter = jnp.exp(lf + m0 - m_t)
    wk = jnp.exp(ii - m_t)
    q = q_ref[...]
    ks = k_ref[...] * kscale
    v = v_ref[...]
    c0 = c0_ref[...]
    n0 = n0_ref[...]
    sc = jnp.sum(q * ks, axis=0, keepdims=True) * wk
    num = inter * jnp.sum(c0 * q, axis=0, keepdims=True) + sc * v
    den = inter * jnp.sum(q * n0, axis=0, keepdims=True) + sc
    hid = num / jnp.maximum(jnp.abs(den), jnp.exp(-m_t))
    c_ref[...] = inter * c0 + (ks * wk) * v
    n_ref[...] = inter * n0 + ks * wk
    m_ref[...] = jnp.broadcast_to(m_t, m_ref.shape)
    mix_ref[...] = (hid * jax.nn.sigmoid(o_ref[...]) * _silu(g_ref[...])).astype(mix_ref.dtype)


def _mlstm_sample(zs, zg, gate_bias, c0, n0, m0, *, col0, n_heads=N_HEADS_D):
    bsz = zs.shape[0]
    dh = c0.shape[-1]
    width = n_heads * dh
    grp = zs[:, col0:col0 + 5 * width].reshape(bsz, 5, n_heads, dh)
    q_col = grp[:, 0].reshape(bsz, n_heads, dh, 1)
    k_col = grp[:, 1].reshape(bsz, n_heads, dh, 1)
    v_row = grp[:, 2].reshape(bsz, n_heads, 1, dh)
    o_row = grp[:, 3].reshape(bsz, n_heads, 1, dh)
    g_row = grp[:, 4].reshape(bsz, n_heads, 1, dh)
    gb = gate_bias.astype(F32)
    scal = jnp.stack([zg[:, :n_heads], zg[:, n_heads:2 * n_heads],
                      jnp.broadcast_to(gb[0][None], (bsz, n_heads)),
                      jnp.broadcast_to(gb[1][None], (bsz, n_heads)),
                      m0.astype(F32)], axis=-1)
    scal = jnp.pad(scal, ((0, 0), (0, 0), (0, LANES - scal.shape[-1]))).reshape(bsz, n_heads, 1, LANES)

    def spec(r, c):
        return pl.BlockSpec((None, None, r, c), lambda b, h: (b, h, 0, 0))

    mix, c_n, n_n, m_n = pl.pallas_call(
        _mlstm_sample_kernel,
        grid=(bsz, n_heads),
        in_specs=[spec(dh, 1), spec(dh, 1), spec(1, dh), spec(1, dh), spec(1, dh), spec(1, LANES),
                  spec(dh, dh), spec(dh, 1)],
        out_specs=[spec(1, dh), spec(dh, dh), spec(dh, 1), spec(1, LANES)],
        out_shape=[jax.ShapeDtypeStruct((bsz, n_heads, 1, dh), BF16),
                   jax.ShapeDtypeStruct((bsz, n_heads, dh, dh), F32),
                   jax.ShapeDtypeStruct((bsz, n_heads, dh, 1), F32),
                   jax.ShapeDtypeStruct((bsz, n_heads, 1, LANES), F32)],
        compiler_params=_params("parallel", "parallel"),
        name="mlstm_sample",
    )(q_col, k_col, v_row, o_row, g_row, scal, c0.astype(F32), n0.astype(F32).reshape(bsz, n_heads, dh, 1))
    return mix.reshape(bsz, width), c_n, n_n.reshape(bsz, n_heads, dh), m_n[:, :, 0, 0]


def kernel(x_prompt, x_sample, cache_a_k, cache_a_v, state_pool, cache_c_k, cache_c_v, state_mlstm_c,
           state_mlstm_n, state_mlstm_m, page_table, rel_bias, w_in_even, pool_w, pool_scale, w_out_even,
           w_in_odd, sb_bias, mlstm_gate_bias, w_out_odd, ln_g, ln_b):
    bsz, seq, d_model = x_prompt.shape
    dbs, dec_seq, _ = x_sample.shape
    assert dec_seq == 1
    depth = ln_g.shape[0]
    alpha = (2 * depth) ** 0.25
    past_len = page_table.shape[1] * PAGE_SIZE
    mix_a = d_model // 2
    n_heads = mix_a // HEAD_DIM
    mix_b = d_model - mix_a
    n_rows = bsz * seq
    odd_main = 9 * mix_a

    xp_f = x_prompt.reshape(n_rows, d_model)
    xp_b = xp_f.astype(BF16)
    xs_f = x_sample.reshape(dbs, d_model)
    xs_b = xs_f.astype(BF16)
    band_bias = _band_bias(rel_bias)

    outs = {k: [] for k in ("akp", "avp", "aks", "avs", "pup", "pus", "ckp", "cvp", "cks", "cvs",
                            "mcp", "mnp", "mmp", "mcs", "mns", "mms")}
    for layer in range(depth):
        j = layer // 2
        if layer % 2 == 0:
            w_in = w_in_even[j].astype(BF16)
            w_out = w_out_even[j].astype(BF16)
            pw = pool_w[j].astype(BF16)
            z = _matmul(xp_b, w_in, tm=1024, tn=512)
            z3 = z.reshape(bsz, seq, -1)
            oa = _dil_prompt(z3, band_bias, n_heads)
            ob = _pool_prompt(z3, pw, pool_scale[j], u_off=4 * mix_a // mix_b, gate_off=4 * mix_a // mix_b + 1)
            yp = _matmul2(oa.reshape(n_rows, mix_a), ob.reshape(n_rows, mix_b), w_out, tm=1024, tn=512)
            keep_a = min(A_WINDOW, seq)
            keep_u = min(POOL_HIST, seq)
            outs["akp"].append(z3[:, seq - keep_a:, mix_a:2 * mix_a].reshape(bsz, keep_a, n_heads, HEAD_DIM))
            outs["avp"].append(z3[:, seq - keep_a:, 2 * mix_a:3 * mix_a].reshape(bsz, keep_a, n_heads, HEAD_DIM))
            outs["pup"].append(z3[:, seq - keep_u:, 4 * mix_a:4 * mix_a + mix_b])
            zs = _matmul(xs_b, w_in, tm=dbs, tn=1024)
            oas = _dil_sample(zs, cache_a_k[j], cache_a_v[j], rel_bias, past_len, n_heads)
            obs = _pool_sample(zs, state_pool[j], pw, pool_scale[j],
                               u_off=4 * mix_a // mix_b, gate_off=4 * mix_a // mix_b + 1)
            ys = _matmul2(oas, obs, w_out, tm=dbs, tn=1024)
            k_new = zs[:, mix_a:2 * mix_a].reshape(dbs, 1, n_heads, HEAD_DIM)
            v_new = zs[:, 2 * mix_a:3 * mix_a].reshape(dbs, 1, n_heads, HEAD_DIM)
            u_new = zs[:, 4 * mix_a:4 * mix_a + mix_b].reshape(dbs, 1, mix_b)
            outs["aks"].append(jnp.concatenate([cache_a_k[j][:, 1:], k_new], axis=1))
            outs["avs"].append(jnp.concatenate([cache_a_v[j][:, 1:], v_new], axis=1))
            outs["pus"].append(jnp.concatenate([state_pool[j][:, 1:], u_new], axis=1))
        else:
            w_full = w_in_odd[j]
            w_in = w_full[:, :odd_main].astype(BF16)
            w_gate = jnp.pad(w_full[:, odd_main:], ((0, 0), (0, LANES - (w_full.shape[1] - odd_main)))).astype(BF16)
            w_out = w_out_odd[j].astype(BF16)
            z = _matmul(xp_b, w_in, tm=1024, tn=512)
            zg = _matmul(xp_b, w_gate, tm=1024, tn=LANES)
            z3 = z.reshape(bsz, seq, -1)
            oc = _sb_prompt(z3, sb_bias[j], n_heads)
            od, c_f, n_f, m_f = _mlstm_prompt(z3, zg.reshape(bsz, seq, LANES), mlstm_gate_bias[j], col0=4 * mix_a)
            yp = _matmul2(oc.reshape(n_rows, mix_a), od.reshape(n_rows, mix_b), w_out, tm=1024, tn=512)
            outs["ckp"].append(z3[:, :, mix_a:2 * mix_a].reshape(bsz, seq, n_heads, HEAD_DIM))
            outs["cvp"].append(z3[:, :, 2 * mix_a:3 * mix_a].reshape(bsz, seq, n_heads, HEAD_DIM))
            outs["mcp"].append(c_f)
            outs["mnp"].append(n_f)
            outs["mmp"].append(m_f)
            zs = _matmul(xs_b, w_in, tm=dbs, tn=1024)
            zgs = _matmul(xs_b, w_gate, tm=dbs, tn=LANES)
            ocs = _sb_sample(zs, cache_c_k[j], cache_c_v[j], page_table, sb_bias[j], n_heads)
            ods, c_n, n_n, m_n = _mlstm_sample(zs, zgs, mlstm_gate_bias[j], state_mlstm_c[j], state_mlstm_n[j],
                                               state_mlstm_m[j], col0=4 * mix_a)
            ys = _matmul2(ocs, ods, w_out, tm=dbs, tn=1024)
            outs["cks"].append(zs[:, mix_a:2 * mix_a].reshape(dbs, 1, n_heads, HEAD_DIM))
            outs["cvs"].append(zs[:, 2 * mix_a:3 * mix_a].reshape(dbs, 1, n_heads, HEAD_DIM))
            outs["mcs"].append(c_n)
            outs["mns"].append(n_n)
            outs["mms"].append(m_n)
        xp_f, xp_b = _residual_ln(xp_f, yp, ln_g[layer], ln_b[layer], alpha=alpha, tm=256)
        xs_f, xs_b = _residual_ln(xs_f, ys, ln_g[layer], ln_b[layer], alpha=alpha, tm=dbs)

    st = {k: jnp.stack(v) for k, v in outs.items()}
    return (xp_f.reshape(bsz, seq, d_model), xs_f.reshape(dbs, 1, d_model),
            st["akp"], st["avp"], st["aks"], st["avs"], st["pup"], st["pus"],
            st["ckp"], st["cvp"], st["cks"], st["cvs"],
            st["mcp"], st["mnp"], st["mmp"], st["mcs"], st["mns"], st["mms"])
```

```python
import functools
import math

import numpy as np
import jax
import jax.numpy as jnp
from jax import lax
from jax.experimental import pallas as pl
from jax.experimental.pallas import tpu as pltpu

HEAD_DIM = 128
DILATIONS = ((128, 1), (512, 4), (2048, 16))
A_WINDOW = max(w for w, _ in DILATIONS)
BAND = 128
POOL_WINDOWS = (2, 4, 8, 16)
POOL_HIST = max(POOL_WINDOWS) - 1
N_HEADS_D = 4
MLSTM_CHUNK = 128
N_REL_BUCKETS = 32
REL_MAX_DISTANCE = A_WINDOW
LN_EPS = 1e-5
LOG2E = 1.0 / math.log(2.0)
PAGE_SIZE = 128
NEG = -1e30

VMEM_LIMIT = 56 * 1024 * 1024
LANES = 128
SUBLANES = 8

BF16 = jnp.bfloat16
F32 = jnp.float32


def _params(*sem):
    return pltpu.CompilerParams(dimension_semantics=sem, vmem_limit_bytes=VMEM_LIMIT)


def _silu(x):
    return x * jax.nn.sigmoid(x)


def _dot(a, b):
    return jnp.dot(a, b, preferred_element_type=F32)


def _dot_nt(a, b):
    return lax.dot_general(a, b, (((1,), (1,)), ((), ())), preferred_element_type=F32)


def _dot_tn(a, b):
    return lax.dot_general(a, b, (((0,), (0,)), ((), ())), preferred_element_type=F32)


def _split_hi_lo(x):
    hi = x.astype(BF16)
    lo = (x - hi.astype(F32)).astype(BF16)
    return hi, lo


def _softplus(z):
    return jnp.maximum(z, 0.0) + jnp.log(1.0 + jnp.exp(-jnp.abs(z)))


def _log_sigmoid(z):
    return -_softplus(-z)


def _mm_kernel(x_ref, w_ref, o_ref):
    o_ref[...] = _dot(x_ref[...], w_ref[...].astype(BF16))


def _matmul(x, w, layer, n_cols, *, tm, tn):
    m, k = x.shape
    tm = min(tm, m)
    assert m % tm == 0 and n_cols % tn == 0 and w.shape[1] == k
    return pl.pallas_call(
        _mm_kernel,
        grid=(m // tm, n_cols // tn),
        in_specs=[pl.BlockSpec((tm, k), lambda i, j: (i, 0), pipeline_mode=pl.Buffered(1)),
                  pl.BlockSpec((None, k, tn), lambda i, j: (layer, 0, j))],
        out_specs=pl.BlockSpec((tm, tn), lambda i, j: (i, j)),
        out_shape=jax.ShapeDtypeStruct((m, n_cols), F32),
        compiler_params=_params("parallel", "parallel"),
        name="proj",
    )(x, w)


def _mm2_kernel(xa_ref, xb_ref, w_ref, o_ref):
    ka = xa_ref.shape[1]
    o_ref[...] = (_dot(xa_ref[...], w_ref[:ka, :].astype(BF16))
                  + _dot(xb_ref[...], w_ref[ka:, :].astype(BF16)))


def _matmul2(xa, xb, w, layer, *, tm, tn):
    m, ka = xa.shape
    kb = xb.shape[1]
    n = w.shape[2]
    tm = min(tm, m)
    assert m % tm == 0 and n % tn == 0 and w.shape[1] == ka + kb
    return pl.pallas_call(
        _mm2_kernel,
        grid=(m // tm, n // tn),
        in_specs=[pl.BlockSpec((tm, ka), lambda i, j: (i, 0), pipeline_mode=pl.Buffered(1)),
                  pl.BlockSpec((tm, kb), lambda i, j: (i, 0), pipeline_mode=pl.Buffered(1)),
                  pl.BlockSpec((None, ka + kb, tn), lambda i, j: (layer, 0, j))],
        out_specs=pl.BlockSpec((tm, tn), lambda i, j: (i, j)),
        out_shape=jax.ShapeDtypeStruct((m, n), F32),
        compiler_params=_params("parallel", "parallel"),
        name="out_proj",
    )(xa, xb, w)


def _ln_kernel(x_ref, y_ref, g_ref, b_ref, of_ref, ob_ref, *, alpha):
    h = alpha * x_ref[...] + y_ref[...]
    mu = jnp.mean(h, axis=-1, keepdims=True)
    c = h - mu
    var = jnp.mean(c * c, axis=-1, keepdims=True)
    r = c * lax.rsqrt(var + LN_EPS) * g_ref[...] + b_ref[...]
    of_ref[...] = r
    ob_ref[...] = r.astype(BF16)


def _residual_ln(x, y, g, b, *, alpha, tm):
    m, d = x.shape
    tm = min(tm, m)
    assert m % tm == 0
    row = pl.BlockSpec((tm, d), lambda i: (i, 0))
    vec = pl.BlockSpec((1, d), lambda i: (0, 0))
    return pl.pallas_call(
        functools.partial(_ln_kernel, alpha=alpha),
        grid=(m // tm,),
        in_specs=[row, row, vec, vec],
        out_specs=[row, row],
        out_shape=[jax.ShapeDtypeStruct((m, d), F32), jax.ShapeDtypeStruct((m, d), BF16)],
        compiler_params=_params("parallel"),
        name="residual_ln",
    )(x, y, g.reshape(1, d), b.reshape(1, d))


def _rel_bucket_static(dist):
    exact = N_REL_BUCKETS // 2
    dist = np.asarray(dist, np.int64)
    ratio = np.log(np.maximum(dist, exact) / exact) / math.log(REL_MAX_DISTANCE / exact)
    large = np.minimum(exact + (ratio * (N_REL_BUCKETS - exact)).astype(np.int64), N_REL_BUCKETS - 1)
    return np.where(dist < exact, dist, large).astype(np.int32)


def _band_bias(rel_bias):
    n_heads = rel_bias.shape[1]
    period = 2 * BAND + 1
    out = []
    for window, dil in DILATIONS:
        steps = window // dil
        m = BAND - np.arange(period)
        valid = (m >= 0) & (m <= steps)
        tab = rel_bias[jnp.asarray(_rel_bucket_static(dil * np.clip(m, 0, steps)))].astype(F32)
        tab = jnp.where(jnp.asarray(valid)[:, None], tab, NEG).T
        rows = jnp.tile(tab, (1, BAND))[:, :BAND * 2 * BAND]
        out.append(rows.reshape(n_heads, BAND, 2 * BAND))
    return jnp.stack(out)


def _sample_bias(rel_bias, n_hist, past_len):
    hist, new = [], []
    for window, dil in DILATIONS:
        steps = window // dil
        m = steps - np.arange(steps)
        idx = n_hist - dil * m
        valid = (past_len - dil * m >= 0) & (idx >= 0)
        bias = rel_bias[jnp.asarray(_rel_bucket_static(dil * m))].astype(F32)
        hist.append(jnp.where(jnp.asarray(valid)[:, None], bias, NEG))
        new.append(rel_bias[int(_rel_bucket_static(0))].astype(F32))
    return jnp.stack(hist), jnp.stack(new)


DIL_UNROLL_FIRST = 4
DIL_UNROLL_REST = 3


def _dil_prompt_kernel(q_ref, k_ref, v_ref, g_ref, bm_ref, o_ref, og_ref, lse_ref, *, seq):
    scale = 1.0 / math.sqrt(HEAD_DIM)

    def rows(ref, start, n, dil):
        if dil == 1:
            return ref[pl.ds(start, n), :]
        return ref[pl.ds(start, n, stride=dil), :]

    def blocks(g, dil, bases, first):
        n = range(len(bases))
        back = 0 if first else dil * BAND
        n_keys = BAND if first else 2 * BAND
        bias = bm_ref[g, :, BAND:] if first else bm_ref[g]
        qs = [(rows(q_ref, b, BAND, dil) * scale).astype(BF16) for b in bases]
        ks = [rows(k_ref, b - back, n_keys, dil).astype(BF16) for b in bases]
        s = [_dot_nt(qs[u], ks[u]) + bias for u in n]
        mx = [jnp.max(s[u], axis=-1, keepdims=True) for u in n]
        p = [jnp.exp(s[u] - mx[u]) for u in n]
        l = [jnp.sum(p[u], axis=-1, keepdims=True) for u in n]
        vs = [rows(v_ref, b - back, n_keys, dil).astype(BF16) for b in bases]
        o = [_dot(p[u].astype(BF16), vs[u]) / l[u] for u in n]
        for u, b in enumerate(bases):
            lse = jnp.broadcast_to(mx[u] + jnp.log(l[u]), (BAND, HEAD_DIM))
            if dil == 1:
                og_ref[g, pl.ds(b, BAND), :] = o[u]
                lse_ref[g, pl.ds(b, BAND), :] = lse
            else:
                og_ref[g, pl.ds(b, BAND, stride=dil), :] = o[u]
                lse_ref[g, pl.ds(b, BAND, stride=dil), :] = lse

    def run(n_items, unroll, base_of, g, dil, first):
        main = n_items // unroll

        def body(t, carry):
            blocks(g, dil, [base_of(t * unroll + u) for u in range(unroll)], first)
            return carry

        if main:
            lax.fori_loop(0, main, body, 0)
        if n_items > main * unroll:
            blocks(g, dil, [base_of(idx) for idx in range(main * unroll, n_items)], first)

    for g, (window, dil) in enumerate(DILATIONS):
        n_blk = seq // dil // BAND
        run(dil, DIL_UNROLL_FIRST, lambda r: r, g, dil, True)
        if n_blk > 1:
            def rest_base(t, dil=dil, n_blk=n_blk):
                return t // (n_blk - 1) + dil * BAND * (t % (n_blk - 1) + 1)

            run(dil * (n_blk - 1), DIL_UNROLL_REST, rest_base, g, dil, False)

    tile = 256

    def merge(t, carry):
        sl = pl.ds(pl.multiple_of(t * tile, tile), tile)
        lses = [lse_ref[g, sl, :] for g in range(len(DILATIONS))]
        mx = functools.reduce(jnp.maximum, lses)
        ws = [jnp.exp(l - mx) for l in lses]
        den = functools.reduce(jnp.add, ws)
        num = functools.reduce(jnp.add, [w * og_ref[g, sl, :] for g, w in enumerate(ws)])
        o_ref[sl, :] = (num / den * _silu(g_ref[sl, :])).astype(o_ref.dtype)
        return carry

    lax.fori_loop(0, seq // tile, merge, 0)


def _dil_prompt(z3, band_bias, n_heads):
    bsz, seq, _ = z3.shape
    n_pat = len(DILATIONS)
    for window, dil in DILATIONS:
        assert seq % (dil * BAND) == 0 and window // dil == BAND

    def col(off):
        return pl.BlockSpec((None, seq, HEAD_DIM), lambda b, h, off=off: (b, 0, off * n_heads + h))

    return pl.pallas_call(
        functools.partial(_dil_prompt_kernel, seq=seq),
        grid=(bsz, n_heads),
        in_specs=[col(0), col(1), col(2), col(3),
                  pl.BlockSpec((n_pat, None, BAND, 2 * BAND), lambda b, h: (0, h, 0, 0))],
        out_specs=pl.BlockSpec((None, seq, HEAD_DIM), lambda b, h: (b, 0, h)),
        out_shape=jax.ShapeDtypeStruct((bsz, seq, n_heads * HEAD_DIM), BF16),
        scratch_shapes=[pltpu.VMEM((n_pat, seq, HEAD_DIM), F32), pltpu.VMEM((n_pat, seq, HEAD_DIM), F32)],
        compiler_params=_params("parallel", "parallel"),
        name="dilated_attn_prompt",
    )(z3, z3, z3, z3, band_bias)


POOL_HALO = 16


def _pool_prompt_kernel(u_ref, halo_ref, gate_ref, pw_ref, ps_ref, o_ref, *, tile):
    t = pl.program_id(1)
    grp = pw_ref.shape[1]
    halo = jnp.where(t > 0, halo_ref[...], 0.0)
    pos = (t * tile + lax.broadcasted_iota(jnp.int32, (tile, 1), 0) + 1).astype(F32)
    for g, win in enumerate(POOL_WINDOWS):
        cols = slice(g * grp, (g + 1) * grp)
        x = u_ref[:, cols]
        ext = jnp.concatenate([halo[:, cols], x], axis=0)
        shift = 1
        while shift < win:
            ext = ext + pltpu.roll(ext, shift, 0)
            shift *= 2
        wsum = ext[POOL_HALO:, :]
        pooled = wsum / jnp.minimum(pos, float(win)) - x
        y = _dot(pooled.astype(BF16), pw_ref[g].astype(BF16)) * ps_ref[:, cols]
        o_ref[:, cols] = (y * _silu(gate_ref[:, cols])).astype(o_ref.dtype)


def _pool_prompt(z3, pool_w, pool_scale, layer, *, u_off, gate_off, tile=256):
    bsz, seq, _ = z3.shape
    _, n_grp, grp, _ = pool_w.shape
    width = n_grp * grp
    assert seq % tile == 0 and tile % POOL_HALO == 0
    for win in POOL_WINDOWS:
        assert win & (win - 1) == 0 and win - 1 <= POOL_HALO
    per = tile // POOL_HALO
    return pl.pallas_call(
        functools.partial(_pool_prompt_kernel, tile=tile),
        grid=(bsz, seq // tile),
        in_specs=[pl.BlockSpec((None, tile, width), lambda b, t: (b, t, u_off)),
                  pl.BlockSpec((None, POOL_HALO, width),
                               lambda b, t: (b, jnp.maximum(t * per - 1, 0), u_off)),
                  pl.BlockSpec((None, tile, width), lambda b, t: (b, t, gate_off)),
                  pl.BlockSpec((None, n_grp, grp, grp), lambda b, t: (layer, 0, 0, 0)),
                  pl.BlockSpec((None, 1, width), lambda b, t: (layer, 0, 0))],
        out_specs=pl.BlockSpec((None, tile, width), lambda b, t: (b, t, 0)),
        out_shape=jax.ShapeDtypeStruct((bsz, seq, width), BF16),
        compiler_params=_params("parallel", "parallel"),
        name="pool_mix_prompt",
    )(z3, z3, z3, pool_w, pool_scale.reshape(pool_scale.shape[0], 1, width))


SB_TILE = 256
SB_HEADS = 4


def _suffix_matrix(n):
    j = np.arange(n)[:, None]
    s = np.arange(n)[None, :]
    return (j >= s).astype(np.float32)


def _sb_prompt_kernel(q_ref, k_ref, v_ref, g_ref, bias_ref, uu_ref, o_ref, kb_ref, vb_ref):
    i = pl.program_id(2)
    scale = 1.0 / math.sqrt(HEAD_DIM)

    @pl.when(i == 0)
    def _():
        kb_ref[...] = k_ref[...].astype(BF16)
        vb_ref[...] = v_ref[...].astype(BF16)

    row = lax.broadcasted_iota(jnp.int32, (SB_TILE, SB_TILE), 0)
    colid = lax.broadcasted_iota(jnp.int32, (SB_TILE, SB_TILE), 1)
    heads = [slice(h * HEAD_DIM, (h + 1) * HEAD_DIM) for h in range(SB_HEADS)]
    hs_all = range(SB_HEADS)
    qs = [(q_ref[:, heads[h]] * (scale * LOG2E)).astype(BF16) for h in hs_all]
    biases = [bias_ref[h][:, 0:1] * LOG2E for h in hs_all]

    def key_block(kb, spent, acc, diag):
        sl = pl.ds(pl.multiple_of(kb * SB_TILE, SB_TILE), SB_TILE)
        z2 = [_dot_nt(qs[h], kb_ref[sl, heads[h]]) + biases[h] for h in hs_all]
        lhs = []
        for h in hs_all:
            sp2 = jnp.maximum(z2[h], 0.0) + jnp.log(1.0 + jnp.exp2(-jnp.abs(z2[h]))) * LOG2E
            if diag:
                sp2 = jnp.where(colid < row, sp2, 0.0)
            hi, lo = _split_hi_lo(sp2)
            lhs.append(jnp.concatenate([hi, lo], axis=1))
        incl = [_dot(lhs[h], uu_ref[...]) for h in hs_all]
        a = []
        for h in hs_all:
            w = jnp.exp2(z2[h] - incl[h] - spent[h])
            if diag:
                w = jnp.where(colid < row, w, 0.0)
            a.append(w.astype(BF16))
        acc = [acc[h] + _dot(a[h], vb_ref[sl, heads[h]]) for h in hs_all]
        spent = [spent[h] + incl[h][:, 0:1] for h in hs_all]
        return spent, acc

    spent = [jnp.zeros((SB_TILE, 1), F32) for _ in hs_all]
    acc = [jnp.zeros((SB_TILE, HEAD_DIM), F32) for _ in hs_all]
    spent, acc = key_block(i, spent, acc, True)

    def body(n, st):
        sp_new, acc_new = key_block(i - 1 - n, list(st[:SB_HEADS]), list(st[SB_HEADS:]), False)
        return tuple(sp_new) + tuple(acc_new)

    state = lax.fori_loop(0, i, body, tuple(spent) + tuple(acc))
    for h in hs_all:
        o_ref[:, heads[h]] = (state[SB_HEADS + h] * _silu(g_ref[:, heads[h]])).astype(o_ref.dtype)


def _sb_prompt(z3, sb_bias, n_heads):
    bsz, seq, _ = z3.shape
    assert seq % SB_TILE == 0 and n_heads % SB_HEADS == 0
    n_grp = n_heads // SB_HEADS
    width = SB_HEADS * HEAD_DIM
    u = _suffix_matrix(SB_TILE)
    uu = jnp.asarray(np.concatenate([u, u], axis=0), BF16)
    bias = jnp.broadcast_to(sb_bias.astype(F32)[:, None, None], (n_heads, 1, LANES))

    def qcol(off):
        return pl.BlockSpec((None, SB_TILE, width), lambda b, h, i, off=off: (b, i, off * n_grp + h))

    def kcol(off):
        return pl.BlockSpec((None, seq, width), lambda b, h, i, off=off: (b, 0, off * n_grp + h))

    return pl.pallas_call(
        _sb_prompt_kernel,
        grid=(bsz, n_grp, seq // SB_TILE),
        in_specs=[qcol(0), kcol(1), kcol(2), qcol(3),
                  pl.BlockSpec((SB_HEADS, 1, LANES), lambda b, h, i: (h, 0, 0)),
                  pl.BlockSpec((2 * SB_TILE, SB_TILE), lambda b, h, i: (0, 0))],
        out_specs=pl.BlockSpec((None, SB_TILE, width), lambda b, h, i: (b, i, h)),
        out_shape=jax.ShapeDtypeStruct((bsz, seq, n_heads * HEAD_DIM), BF16),
        scratch_shapes=[pltpu.VMEM((seq, width), BF16), pltpu.VMEM((seq, width), BF16)],
        compiler_params=_params("parallel", "parallel", "arbitrary"),
        name="stick_breaking_prompt",
    )(z3, z3, z3, z3, bias, uu)


def _mlstm_prompt_kernel(q_ref, k_ref, v_ref, o_ref, g_ref, gc_ref, gr_ref, bc_ref, br_ref,
                         mix_ref, c_ref, n_ref, m_ref):
    chunk = pl.program_id(1)
    n_heads, dh, _ = c_ref.shape
    L = q_ref.shape[0]
    kscale = 1.0 / math.sqrt(dh)

    @pl.when(chunk == 0)
    def _():
        c_ref[...] = jnp.zeros_like(c_ref)
        n_ref[...] = jnp.zeros_like(n_ref)
        m_ref[...] = jnp.zeros_like(m_ref)

    t_idx = lax.broadcasted_iota(jnp.int32, (L, L), 0)
    s_idx = lax.broadcasted_iota(jnp.int32, (L, L), 1)
    causal = s_idx <= t_idx
    gates_c = gc_ref[...] + br_ref[...]
    gates_r = gr_ref[...] + bc_ref[...]
    for h in range(n_heads):
        cols = slice(h * dh, (h + 1) * dh)
        ii_c = gates_c[:, h:h + 1]
        lf_c = _log_sigmoid(gates_c[:, n_heads + h:n_heads + h + 1])
        ii_r = gates_r[h:h + 1, :]
        lf_r = _log_sigmoid(gates_r[n_heads + h:n_heads + h + 1, :])
        b_c = jnp.sum(jnp.where(causal, lf_r, 0.0), axis=1, keepdims=True)
        b_r = jnp.sum(jnp.where(t_idx <= s_idx, lf_c, 0.0), axis=0, keepdims=True)
        g_r = ii_r - b_r
        cm_c = jnp.max(jnp.where(causal, g_r, -jnp.inf), axis=1, keepdims=True)
        m0 = m_ref[h:h + 1, 0:1]
        m_t = b_c + jnp.maximum(m0, cm_c)
        inter = jnp.exp(b_c + m0 - m_t)
        dmat = jnp.where(causal, jnp.exp((b_c - m_t) + g_r), 0.0)
        q = q_ref[:, cols].astype(BF16)
        ks = k_ref[:, cols] * kscale
        v = v_ref[:, cols].astype(BF16)
        c0 = c_ref[h]
        n0 = n_ref[h:h + 1, :]
        sc = _dot_nt(q, ks.astype(BF16)) * dmat
        num = inter * _dot(q, c0.astype(BF16)) + _dot(sc.astype(BF16), v)
        qn = jnp.sum(q_ref[:, cols] * n0, axis=1, keepdims=True)
        den = inter * qn + jnp.sum(sc, axis=1, keepdims=True)
        hid = num / jnp.maximum(jnp.abs(den), jnp.exp(-m_t))
        m_last = m_t[L - 1:L, :]
        b_last = b_c[L - 1:L, :]
        decay = jnp.exp(b_last + m0 - m_last)
        wk = jnp.exp(b_last - b_c + ii_c - m_last)
        kw = ks * wk
        c_ref[h] = decay * c0 + _dot_tn(kw.astype(BF16), v)
        n_ref[h:h + 1, :] = decay * n0 + jnp.sum(kw, axis=0, keepdims=True)
        m_ref[h:h + 1, :] = jnp.broadcast_to(m_last, (1, m_ref.shape[1]))
        out = hid * jax.nn.sigmoid(o_ref[:, cols]) * _silu(g_ref[:, cols])
        mix_ref[:, cols] = out.astype(mix_ref.dtype)


def _mlstm_prompt(z3, zg3, gate_bias, *, col0, n_heads=N_HEADS_D, chunk=MLSTM_CHUNK):
    bsz, seq, _ = z3.shape
    width = (z3.shape[2] - col0) // 5
    dh = width // n_heads
    assert seq % chunk == 0 and col0 % width == 0 and 2 * n_heads <= SUBLANES
    base = col0 // width
    zg_rows = jnp.transpose(zg3[:, :, :SUBLANES], (0, 2, 1))
    bias_lane = jnp.zeros((1, LANES), F32).at[0, :2 * n_heads].set(gate_bias.astype(F32).reshape(-1))
    bias_sub = bias_lane[0, :SUBLANES].reshape(SUBLANES, 1)

    def col(off):
        return pl.BlockSpec((None, chunk, width), lambda b, c, off=off: (b, c, base + off))

    mix, c_f, n_f, m_f = pl.pallas_call(
        _mlstm_prompt_kernel,
        grid=(bsz, seq // chunk),
        in_specs=[col(0), col(1), col(2), col(3), col(4),
                  pl.BlockSpec((None, chunk, LANES), lambda b, c: (b, c, 0)),
                  pl.BlockSpec((None, SUBLANES, chunk), lambda b, c: (b, 0, c)),
                  pl.BlockSpec((SUBLANES, 1), lambda b, c: (0, 0)),
                  pl.BlockSpec((1, LANES), lambda b, c: (0, 0))],
        out_specs=[pl.BlockSpec((None, chunk, width), lambda b, c: (b, c, 0)),
                   pl.BlockSpec((None, n_heads, dh, dh), lambda b, c: (b, 0, 0, 0)),
                   pl.BlockSpec((None, n_heads, dh), lambda b, c: (b, 0, 0)),
                   pl.BlockSpec((None, n_heads, LANES), lambda b, c: (b, 0, 0))],
        out_shape=[jax.ShapeDtypeStruct((bsz, seq, width), BF16),
                   jax.ShapeDtypeStruct((bsz, n_heads, dh, dh), F32),
                   jax.ShapeDtypeStruct((bsz, n_heads, dh), F32),
                   jax.ShapeDtypeStruct((bsz, n_heads, LANES), F32)],
        compiler_params=_params("parallel", "arbitrary"),
        name="mlstm_prompt",
    )(z3, z3, z3, z3, z3, zg3, zg_rows, bias_sub, bias_lane)
    return mix, c_f, n_f, m_f[:, :, 0]


def _group_tokens(n_heads):
    assert LANES % n_heads == 0
    return LANES // n_heads


def _stack_groups(zt, n_grp):
    return jnp.concatenate([zt[:, g * LANES:(g + 1) * LANES] for g in range(n_grp)], axis=0)


def _unstack_groups(a, n_grp, n_heads):
    return jnp.concatenate([a[g * n_heads:(g + 1) * n_heads, :] for g in range(n_grp)], axis=1)


def _own_head_mask(n_grp, n_heads):
    r = lax.broadcasted_iota(jnp.int32, (n_grp * n_heads, LANES), 0)
    c = lax.broadcasted_iota(jnp.int32, (n_grp * n_heads, LANES), 1)
    return (r % n_heads) == (c % n_heads)


def _dil_sample_kernel(*refs, n_pat):
    q_ref, kn_ref, vn_ref, g_ref, bh_ref, bn_ref = refs[:6]
    k_refs = refs[6:6 + n_pat]
    v_refs = refs[6 + n_pat:6 + 2 * n_pat]
    o_ref = refs[6 + 2 * n_pat]
    steps, n_heads, dh = k_refs[0].shape
    n_grp = steps * n_heads // LANES
    scale = 1.0 / math.sqrt(dh)
    q = q_ref[...] * scale
    qb = q.astype(BF16)
    zn = jnp.sum(q * kn_ref[...], axis=-1, keepdims=True)
    z_new = [zn + bn_ref[g][:, 0:1] for g in range(n_pat)]
    z_hist = []
    for g in range(n_pat):
        k2 = k_refs[g][...].reshape(steps * n_heads, dh).astype(BF16)
        z_hist.append(_stack_groups(_dot_nt(qb, k2), n_grp) + bh_ref[g])
    mx = functools.reduce(jnp.maximum, z_new)
    for z in z_hist:
        zr = jnp.max(z, axis=-1, keepdims=True)
        for gi in range(n_grp):
            mx = jnp.maximum(mx, zr[gi * n_heads:(gi + 1) * n_heads, :])
    mx_rows = jnp.concatenate([mx] * n_grp, axis=0)
    den = functools.reduce(jnp.add, [jnp.exp(z - mx) for z in z_new])
    w_new = den
    w_hist = [jnp.exp(z - mx_rows) for z in z_hist]
    for w in w_hist:
        wr = jnp.sum(w, axis=-1, keepdims=True)
        for gi in range(n_grp):
            den = den + wr[gi * n_heads:(gi + 1) * n_heads, :]
    inv = 1.0 / den
    inv_rows = jnp.concatenate([inv] * n_grp, axis=0)
    acc = (w_new * inv) * vn_ref[...]
    for g in range(n_pat):
        p = _unstack_groups((w_hist[g] * inv_rows).astype(BF16), n_grp, n_heads)
        v2 = v_refs[g][...].reshape(steps * n_heads, dh).astype(BF16)
        acc = acc + _dot(p, v2)
    o_ref[...] = (acc * _silu(g_ref[...])).astype(o_ref.dtype)


def _dil_sample(zs, buf_k, buf_v, layer, rel_bias, past_len):
    _, bsz, n_hist, n_heads, dh = buf_k.shape
    width = n_heads * dh
    n_pat = len(DILATIONS)
    steps = BAND
    tok = _group_tokens(n_heads)
    n_grp = steps // tok
    bias_hist, bias_new = _sample_bias(rel_bias, n_hist, past_len)
    bh = bias_hist.reshape(n_pat, n_grp, 1, tok, n_heads)
    own = jnp.asarray(np.eye(n_heads, dtype=bool)).reshape(1, 1, n_heads, 1, n_heads)
    bh = jnp.where(own, bh, NEG).reshape(n_pat, n_grp * n_heads, LANES)
    bn = jnp.broadcast_to(bias_new[:, :, None], (n_pat, n_heads, LANES))
    z4 = zs[:, :4 * width].reshape(bsz, 4, n_heads, dh)
    k_views, v_views, kv_specs = [], [], []
    for window, dil in DILATIONS:
        assert window // dil == steps and n_hist % window == 0 and window % dil == 0
        shape = (buf_k.shape[0], bsz, n_hist // dil, dil, n_heads, dh)
        k_views.append(buf_k.reshape(shape))
        v_views.append(buf_v.reshape(shape))
        blk = n_hist // window - 1
        kv_specs.append(pl.BlockSpec((None, None, steps, None, n_heads, dh),
                                     lambda b, blk=blk: (layer, b, blk, 0, 0, 0)))

    def zrow(off):
        return pl.BlockSpec((None, None, n_heads, dh), lambda b, off=off: (b, off, 0, 0))

    return pl.pallas_call(
        functools.partial(_dil_sample_kernel, n_pat=n_pat),
        grid=(bsz,),
        in_specs=[zrow(0), zrow(1), zrow(2), zrow(3),
                  pl.BlockSpec((n_pat, n_grp * n_heads, LANES), lambda b: (0, 0, 0)),
                  pl.BlockSpec((n_pat, n_heads, LANES), lambda b: (0, 0, 0))] + kv_specs + kv_specs,
        out_specs=pl.BlockSpec((None, n_heads, dh), lambda b: (b, 0, 0)),
        out_shape=jax.ShapeDtypeStruct((bsz, n_heads, dh), BF16),
        compiler_params=_params("parallel"),
        name="dilated_attn_sample",
    )(z4, z4, z4, z4, bh, bn, *k_views, *v_views).reshape(bsz, width)


def _pool_sample_kernel(u_ref, hist_ref, gate_ref, pw_ref, ps_ref, o_ref):
    grp = pw_ref.shape[1]
    n_hist = hist_ref.shape[0]
    for g, win in enumerate(POOL_WINDOWS):
        cols = slice(g * grp, (g + 1) * grp)
        x = u_ref[:, cols]
        n_old = min(win - 1, n_hist)
        wsum = x
        for r in range(n_hist - n_old, n_hist):
            wsum = wsum + hist_ref[r, :, cols]
        pooled = wsum / float(n_old + 1) - x
        y = _dot(pooled.astype(BF16), pw_ref[g].astype(BF16)) * ps_ref[:, cols]
        o_ref[:, cols] = (y * _silu(gate_ref[:, cols])).astype(o_ref.dtype)


def _pool_sample(zs, hist, pool_w, pool_scale, layer, *, u_off, gate_off):
    bsz = zs.shape[0]
    _, n_grp, grp, _ = pool_w.shape
    width = n_grp * grp
    n_hist = hist.shape[1]
    hist = jnp.transpose(hist, (1, 0, 2))
    return pl.pallas_call(
        _pool_sample_kernel,
        grid=(1,),
        in_specs=[pl.BlockSpec((bsz, width), lambda i: (0, u_off)),
                  pl.BlockSpec((n_hist, bsz, width), lambda i: (0, 0, 0)),
                  pl.BlockSpec((bsz, width), lambda i: (0, gate_off)),
                  pl.BlockSpec((None, n_grp, grp, grp), lambda i: (layer, 0, 0, 0)),
                  pl.BlockSpec((None, 1, width), lambda i: (layer, 0, 0))],
        out_specs=pl.BlockSpec((bsz, width), lambda i: (0, 0)),
        out_shape=jax.ShapeDtypeStruct((bsz, width), BF16),
        compiler_params=_params("arbitrary"),
        name="pool_mix_sample",
    )(zs, hist, zs, pool_w, pool_scale.reshape(pool_scale.shape[0], 1, width))


def _sb_sample_kernel(pt_ref, q_ref, g_ref, bias_ref, ut_ref, k_ref, v_ref, o_ref, carry_sc, acc_sc):
    p = pl.program_id(1)
    page, n_heads, dh = k_ref.shape
    n_grp = page * n_heads // LANES
    scale = 1.0 / math.sqrt(dh)

    @pl.when(p == 0)
    def _():
        carry_sc[...] = jnp.zeros_like(carry_sc)
        acc_sc[...] = jnp.zeros_like(acc_sc)

    qb = (q_ref[...] * scale).astype(BF16)
    k2 = k_ref[...].reshape(page * n_heads, dh).astype(BF16)
    v2 = v_ref[...].reshape(page * n_heads, dh).astype(BF16)
    own = _own_head_mask(n_grp, n_heads)
    z = _stack_groups(_dot_nt(qb, k2), n_grp) + bias_ref[...]
    sp = _softplus(z)
    log_rest = jnp.where(own, -sp, 0.0)
    log_take = z - sp
    hi, lo = _split_hi_lo(log_rest)
    sums = _dot(jnp.concatenate([hi, lo], axis=1), ut_ref[...])
    within = sums[:, :LANES]
    total = sums[:, LANES:]
    after = carry_sc[...]
    offs = [None] * n_grp
    for gi in range(n_grp - 1, -1, -1):
        offs[gi] = after
        after = after + total[gi * n_heads:(gi + 1) * n_heads, :]
    carry_sc[...] = after
    a = jnp.where(own, jnp.exp(log_take + within + jnp.concatenate(offs, axis=0)), 0.0)
    acc_sc[...] += _dot(_unstack_groups(a.astype(BF16), n_grp, n_heads), v2)

    @pl.when(p == pl.num_programs(1) - 1)
    def _():
        o_ref[...] = (acc_sc[...] * _silu(g_ref[...])).astype(o_ref.dtype)


def _sb_sample(zs, pool_k, pool_v, layer, page_table, sb_bias):
    bsz, n_pages = page_table.shape
    _, _, page, n_heads, dh = pool_k.shape
    width = n_heads * dh
    tok = _group_tokens(n_heads)
    n_grp = page // tok
    lane_tok = np.arange(LANES) // n_heads
    lane_head = np.arange(LANES) % n_heads
    later = (lane_head[:, None] == lane_head[None, :]) & (lane_tok[:, None] > lane_tok[None, :])
    half = np.concatenate([later.astype(np.float32), np.ones((LANES, LANES), np.float32)], axis=1)
    ut = jnp.asarray(np.concatenate([half, half], axis=0), BF16)
    z4 = zs[:, :4 * width].reshape(bsz, 4, n_heads, dh)
    bias = jnp.tile(jnp.broadcast_to(sb_bias.astype(F32)[:, None], (n_heads, LANES)), (n_grp, 1))

    def page_spec():
        return pl.BlockSpec((None, None, page, n_heads, dh),
                            lambda b, p, pt: (layer, pt[b, n_pages - 1 - p], 0, 0, 0))

    return pl.pallas_call(
        _sb_sample_kernel,
        grid_spec=pltpu.PrefetchScalarGridSpec(
            num_scalar_prefetch=1,
            grid=(bsz, n_pages),
            in_specs=[pl.BlockSpec((None, None, n_heads, dh), lambda b, p, pt: (b, 0, 0, 0)),
                      pl.BlockSpec((None, None, n_heads, dh), lambda b, p, pt: (b, 3, 0, 0)),
                      pl.BlockSpec((n_grp * n_heads, LANES), lambda b, p, pt: (0, 0)),
                      pl.BlockSpec((2 * LANES, 2 * LANES), lambda b, p, pt: (0, 0)),
                      page_spec(), page_spec()],
            out_specs=pl.BlockSpec((None, n_heads, dh), lambda b, p, pt: (b, 0, 0)),
            scratch_shapes=[pltpu.VMEM((n_heads, LANES), F32), pltpu.VMEM((n_heads, dh), F32)]),
        out_shape=jax.ShapeDtypeStruct((bsz, n_heads, dh), BF16),
        compiler_params=_params("parallel", "arbitrary"),
        name="stick_breaking_sample",
    )(page_table, z4, z4, bias, ut, pool_k, pool_v).reshape(bsz, width)


def _mlstm_sample_kernel(q_ref, k_ref, v_ref, o_ref, g_ref, sc_ref, c0_ref, n0_ref,
                         mix_ref, c_ref, n_ref, m_ref):
    dh = c0_ref.shape[0]
    kscale = 1.0 / math.sqrt(dh)
    s = sc_ref[...]
    ii = s[:, 0:1] + s[:, 2:3]
    lf = _log_sigmoid(s[:, 1:2] + s[:, 3:4])
    m0 = s[:, 4:5]
    m_t = jnp.maximum(lf + m0, ii)
    inter = jnp.exp(lf + m0 - m_t)
    wk = jnp.exp(ii - m_t)
    q = q_ref[...]
    ks = k_ref[...] * kscale
    v = v_ref[...]
    c0 = c0_ref[...]
    n0 = n0_ref[...]
    sc = jnp.sum(q * ks, axis=0, keepdims=True) * wk
    num = inter * jnp.sum(c0 * q, axis=0, keepdims=True) + sc * v
    den = inter * jnp.sum(q * n0, axis=0, keepdims=True) + sc
    hid = num / jnp.maximum(jnp.abs(den), jnp.exp(-m_t))
    c_ref[...] = inter * c0 + (ks * wk) * v
    n_ref[...] = inter * n0 + ks * wk
    m_ref[...] = jnp.broadcast_to(m_t, m_ref.shape)
    mix_ref[...] = (hid * jax.nn.sigmoid(o_ref[...]) * _silu(g_ref[...])).astype(mix_ref.dtype)


def _mlstm_sample(zs, zg, gate_bias, c0, n0, m0, layer, *, col0, n_heads=N_HEADS_D):
    bsz = zs.shape[0]
    dh = c0.shape[-1]
    width = n_heads * dh
    grp = zs[:, col0:col0 + 5 * width].reshape(bsz, 5, n_heads, dh)
    q_col = grp[:, 0].reshape(bsz, n_heads, dh, 1)
    k_col = grp[:, 1].reshape(bsz, n_heads, dh, 1)
    v_row = grp[:, 2].reshape(bsz, n_heads, 1, dh)
    o_row = grp[:, 3].reshape(bsz, n_heads, 1, dh)
    g_row = grp[:, 4].reshape(bsz, n_heads, 1, dh)
    gb = gate_bias.astype(F32)
    scal = jnp.stack([zg[:, :n_heads], zg[:, n_heads:2 * n_heads],
                      jnp.broadcast_to(gb[0][None], (bsz, n_heads)),
                      jnp.broadcast_to(gb[1][None], (bsz, n_heads)),
                      m0.astype(F32)], axis=-1)
    scal = jnp.pad(scal, ((0, 0), (0, 0), (0, LANES - scal.shape[-1]))).reshape(bsz, n_heads, 1, LANES)

    def spec(r, c):
        return pl.BlockSpec((None, None, r, c), lambda b, h: (b, h, 0, 0))

    mix, c_n, n_n, m_n = pl.pallas_call(
        _mlstm_sample_kernel,
        grid=(bsz, n_heads),
        in_specs=[spec(dh, 1), spec(dh, 1), spec(1, dh), spec(1, dh), spec(1, dh), spec(1, LANES),
                  pl.BlockSpec((None, None, None, dh, dh), lambda b, h: (layer, b, h, 0, 0)),
                  spec(dh, 1)],
        out_specs=[spec(1, dh), spec(dh, dh), spec(dh, 1), spec(1, LANES)],
        out_shape=[jax.ShapeDtypeStruct((bsz, n_heads, 1, dh), BF16),
                   jax.ShapeDtypeStruct((bsz, n_heads, dh, dh), F32),
                   jax.ShapeDtypeStruct((bsz, n_heads, dh, 1), F32),
                   jax.ShapeDtypeStruct((bsz, n_heads, 1, LANES), F32)],
        compiler_params=_params("parallel", "parallel"),
        name="mlstm_sample",
    )(q_col, k_col, v_row, o_row, g_row, scal, c0, n0.astype(F32).reshape(bsz, n_heads, dh, 1))
    return mix.reshape(bsz, width), c_n, n_n.reshape(bsz, n_heads, dh), m_n[:, :, 0, 0]


PROJ_TM = 2048
PROJ_TN = 512
SAMPLE_TN = 1024


def kernel(x_prompt, x_sample, cache_a_k, cache_a_v, state_pool, cache_c_k, cache_c_v, state_mlstm_c,
           state_mlstm_n, state_mlstm_m, page_table, rel_bias, w_in_even, pool_w, pool_scale, w_out_even,
           w_in_odd, sb_bias, mlstm_gate_bias, w_out_odd, ln_g, ln_b):
    bsz, seq, d_model = x_prompt.shape
    dbs, dec_seq, _ = x_sample.shape
    assert dec_seq == 1
    depth = ln_g.shape[0]
    alpha = (2 * depth) ** 0.25
    past_len = page_table.shape[1] * PAGE_SIZE
    mix_a = d_model // 2
    n_heads = mix_a // HEAD_DIM
    mix_b = d_model - mix_a
    n_rows = bsz * seq
    even_cols = w_in_even.shape[2]
    odd_main = 9 * mix_a
    n_gate = w_in_odd.shape[2] - odd_main
    w_gate = jnp.pad(w_in_odd[:, :, odd_main:], ((0, 0), (0, 0), (0, LANES - n_gate)))

    xp_f = x_prompt.reshape(n_rows, d_model)
    xp_b = xp_f.astype(BF16)
    xs_f = x_sample.reshape(dbs, d_model)
    xs_b = xs_f.astype(BF16)
    band_bias = _band_bias(rel_bias)
    u_blk = 4 * mix_a // mix_b

    zp_even, zp_odd, zs_even, zs_odd = [], [], [], []
    mcp, mnp, mmp, mcs, mns, mms = [], [], [], [], [], []
    for layer in range(depth):
        j = layer // 2
        if layer % 2 == 0:
            z = _matmul(xp_b, w_in_even, j, even_cols, tm=PROJ_TM, tn=PROJ_TN)
            z3 = z.reshape(bsz, seq, -1)
            oa = _dil_prompt(z3, band_bias, n_heads)
            ob = _pool_prompt(z3, pool_w, pool_scale, j, u_off=u_blk, gate_off=u_blk + 1)
            yp = _matmul2(oa.reshape(n_rows, mix_a), ob.reshape(n_rows, mix_b), w_out_even, j,
                          tm=PROJ_TM, tn=PROJ_TN)
            zp_even.append(z3)
            zs = _matmul(xs_b, w_in_even, j, even_cols, tm=dbs, tn=SAMPLE_TN)
            oas = _dil_sample(zs, cache_a_k, cache_a_v, j, rel_bias, past_len)
            obs = _pool_sample(zs, state_pool[j], pool_w, pool_scale, j, u_off=u_blk, gate_off=u_blk + 1)
            ys = _matmul2(oas, obs, w_out_even, j, tm=dbs, tn=SAMPLE_TN)
            zs_even.append(zs)
        else:
            z = _matmul(xp_b, w_in_odd, j, odd_main, tm=PROJ_TM, tn=PROJ_TN)
            zg = _matmul(xp_b, w_gate, j, LANES, tm=PROJ_TM, tn=LANES)
            z3 = z.reshape(bsz, seq, -1)
            oc = _sb_prompt(z3, sb_bias[j], n_heads)
            od, c_f, n_f, m_f = _mlstm_prompt(z3, zg.reshape(bsz, seq, LANES), mlstm_gate_bias[j], col0=4 * mix_a)
            yp = _matmul2(oc.reshape(n_rows, mix_a), od.reshape(n_rows, mix_b), w_out_odd, j,
                          tm=PROJ_TM, tn=PROJ_TN)
            zp_odd.append(z3)
            mcp.append(c_f)
            mnp.append(n_f)
            mmp.append(m_f)
            zs = _matmul(xs_b, w_in_odd, j, odd_main, tm=dbs, tn=SAMPLE_TN)
            zgs = _matmul(xs_b, w_gate, j, LANES, tm=dbs, tn=LANES)
            ocs = _sb_sample(zs, cache_c_k, cache_c_v, j, page_table, sb_bias[j])
            ods, c_n, n_n, m_n = _mlstm_sample(zs, zgs, mlstm_gate_bias[j], state_mlstm_c, state_mlstm_n[j],
                                               state_mlstm_m[j], j, col0=4 * mix_a)
            ys = _matmul2(ocs, ods, w_out_odd, j, tm=dbs, tn=SAMPLE_TN)
            zs_odd.append(zs)
            mcs.append(c_n)
            mns.append(n_n)
            mms.append(m_n)
        xp_f, xp_b = _residual_ln(xp_f, yp, ln_g[layer], ln_b[layer], alpha=alpha, tm=256)
        xs_f, xs_b = _residual_ln(xs_f, ys, ln_g[layer], ln_b[layer], alpha=alpha, tm=dbs)

    keep_a = min(A_WINDOW, seq)
    keep_u = min(POOL_HIST, seq)

    def heads(a):
        return a.reshape(a.shape[:-1] + (n_heads, HEAD_DIM))

    def pick(zs, rows, lo, hi):
        return jnp.stack([z[..., rows, lo:hi] for z in zs])

    every = slice(None)
    new_row = np.newaxis
    a_k_prompt = heads(pick(zp_even, slice(seq - keep_a, seq), mix_a, 2 * mix_a))
    a_v_prompt = heads(pick(zp_even, slice(seq - keep_a, seq), 2 * mix_a, 3 * mix_a))
    pool_prompt = pick(zp_even, slice(seq - keep_u, seq), 4 * mix_a, 4 * mix_a + mix_b)
    a_k_sample = jnp.concatenate([cache_a_k[:, :, 1:], heads(pick(zs_even, every, mix_a, 2 * mix_a))[:, :, new_row]],
                                 axis=2)
    a_v_sample = jnp.concatenate([cache_a_v[:, :, 1:],
                                  heads(pick(zs_even, every, 2 * mix_a, 3 * mix_a))[:, :, new_row]], axis=2)
    pool_sample = jnp.concatenate([state_pool[:, :, 1:],
                                   pick(zs_even, every, 4 * mix_a, 4 * mix_a + mix_b)[:, :, new_row]], axis=2)
    c_k_prompt = heads(pick(zp_odd, every, mix_a, 2 * mix_a))
    c_v_prompt = heads(pick(zp_odd, every, 2 * mix_a, 3 * mix_a))
    c_k_sample = heads(pick(zs_odd, every, mix_a, 2 * mix_a))[:, :, new_row]
    c_v_sample = heads(pick(zs_odd, every, 2 * mix_a, 3 * mix_a))[:, :, new_row]
    return (xp_f.reshape(bsz, seq, d_model), xs_f.reshape(dbs, 1, d_model),
            a_k_prompt, a_v_prompt, a_k_sample, a_v_sample, pool_prompt, pool_sample,
            c_k_prompt, c_v_prompt, c_k_sample, c_v_sample,
            jnp.stack(mcp), jnp.stack(mnp), jnp.stack(mmp), jnp.stack(mcs), jnp.stack(mns), jnp.stack(mms))
```

```python
import functools
import math

import numpy as np
import jax
import jax.numpy as jnp
from jax import lax
from jax.experimental import pallas as pl
from jax.experimental.pallas import tpu as pltpu

HEAD_DIM = 128
DILATIONS = ((128, 1), (512, 4), (2048, 16))
A_WINDOW = max(w for w, _ in DILATIONS)
BAND = 128
POOL_WINDOWS = (2, 4, 8, 16)
POOL_HIST = max(POOL_WINDOWS) - 1
N_HEADS_D = 4
MLSTM_CHUNK = 128
N_REL_BUCKETS = 32
REL_MAX_DISTANCE = A_WINDOW
LN_EPS = 1e-5
LOG2E = 1.0 / math.log(2.0)
PAGE_SIZE = 128
NEG = -1e30

VMEM_LIMIT = 56 * 1024 * 1024
LANES = 128
SUBLANES = 8

BF16 = jnp.bfloat16
F32 = jnp.float32


def _params(*sem):
    return pltpu.CompilerParams(dimension_semantics=sem, vmem_limit_bytes=VMEM_LIMIT)


def _silu(x):
    return x * jax.nn.sigmoid(x)


def _dot(a, b):
    return jnp.dot(a, b, preferred_element_type=F32)


def _dot_nt(a, b):
    return lax.dot_general(a, b, (((1,), (1,)), ((), ())), preferred_element_type=F32)


def _dot_tn(a, b):
    return lax.dot_general(a, b, (((0,), (0,)), ((), ())), preferred_element_type=F32)


def _split_hi_lo(x):
    hi = x.astype(BF16)
    lo = (x - hi.astype(F32)).astype(BF16)
    return hi, lo


def _softplus(z):
    return jnp.maximum(z, 0.0) + jnp.log(1.0 + jnp.exp(-jnp.abs(z)))


def _log_sigmoid(z):
    return -_softplus(-z)


def _proj_kernel(x_ref, xs_ref, w_ref, o_ref, os_ref, *, w_is_nk):
    wb = w_ref[...].astype(BF16)
    mm = _dot_nt if w_is_nk else _dot
    o_ref[...] = mm(x_ref[...], wb)
    os_ref[...] = mm(xs_ref[...], wb)


def _proj(x, xs, w, layer, n_cols, *, w_is_nk, tm, tn):
    m, k = x.shape
    ms = xs.shape[0]
    assert m % tm == 0 and n_cols % tn == 0 and w.shape[2 if w_is_nk else 1] == k
    if w_is_nk:
        w_spec = pl.BlockSpec((None, tn, k), lambda i, j: (layer, j, 0))
    else:
        w_spec = pl.BlockSpec((None, k, tn), lambda i, j: (layer, 0, j))
    o, os = pl.pallas_call(
        functools.partial(_proj_kernel, w_is_nk=w_is_nk),
        grid=(m // tm, n_cols // tn),
        in_specs=[pl.BlockSpec((tm, k), lambda i, j: (i, 0), pipeline_mode=pl.Buffered(1)),
                  pl.BlockSpec((ms, k), lambda i, j: (0, 0)),
                  w_spec],
        out_specs=[pl.BlockSpec((tm, tn), lambda i, j: (i, j)),
                   pl.BlockSpec((None, ms, tn), lambda i, j: (i, 0, j))],
        out_shape=[jax.ShapeDtypeStruct((m, n_cols), F32), jax.ShapeDtypeStruct((m // tm, ms, n_cols), F32)],
        compiler_params=_params("arbitrary", "arbitrary"),
        name="proj",
    )(x, xs, w)
    return o, os[0]


def _gate_proj_kernel(x_ref, xs_ref, w_ref, o_ref, os_ref):
    wb = w_ref[...].astype(BF16)
    o_ref[...] = _dot_nt(x_ref[...], wb)
    os_ref[...] = _dot_nt(xs_ref[...], wb)


def _gate_proj(x, xs, w_nk, layer, row0, n_gate, *, tm):
    m, k = x.shape
    ms = xs.shape[0]
    assert m % tm == 0 and row0 % n_gate == 0 and n_gate % SUBLANES == 0
    o, os = pl.pallas_call(
        _gate_proj_kernel,
        grid=(m // tm,),
        in_specs=[pl.BlockSpec((tm, k), lambda i: (i, 0)),
                  pl.BlockSpec((ms, k), lambda i: (0, 0)),
                  pl.BlockSpec((None, n_gate, k), lambda i: (layer, row0 // n_gate, 0))],
        out_specs=[pl.BlockSpec((tm, n_gate), lambda i: (i, 0)),
                   pl.BlockSpec((None, ms, n_gate), lambda i: (i, 0, 0))],
        out_shape=[jax.ShapeDtypeStruct((m, n_gate), F32), jax.ShapeDtypeStruct((m // tm, ms, n_gate), F32)],
        compiler_params=_params("arbitrary"),
        name="gate_proj",
    )(x, xs, w_nk)
    return o, os[0]


def _out_proj_kernel(xa_ref, xb_ref, sa_ref, sb_ref, w_ref, o_ref, os_ref):
    ka = xa_ref.shape[1]
    wa = w_ref[:ka, :].astype(BF16)
    wb = w_ref[ka:, :].astype(BF16)
    o_ref[...] = _dot(xa_ref[...], wa) + _dot(xb_ref[...], wb)
    os_ref[...] = _dot(sa_ref[...], wa) + _dot(sb_ref[...], wb)


def _out_proj(xa, xb, sa, sb, w, layer, *, tm, tn):
    m, ka = xa.shape
    kb = xb.shape[1]
    ms = sa.shape[0]
    n = w.shape[2]
    assert m % tm == 0 and n % tn == 0 and w.shape[1] == ka + kb
    o, os = pl.pallas_call(
        _out_proj_kernel,
        grid=(m // tm, n // tn),
        in_specs=[pl.BlockSpec((tm, ka), lambda i, j: (i, 0), pipeline_mode=pl.Buffered(1)),
                  pl.BlockSpec((tm, kb), lambda i, j: (i, 0), pipeline_mode=pl.Buffered(1)),
                  pl.BlockSpec((ms, ka), lambda i, j: (0, 0)),
                  pl.BlockSpec((ms, kb), lambda i, j: (0, 0)),
                  pl.BlockSpec((None, ka + kb, tn), lambda i, j: (layer, 0, j))],
        out_specs=[pl.BlockSpec((tm, tn), lambda i, j: (i, j)),
                   pl.BlockSpec((None, ms, tn), lambda i, j: (i, 0, j))],
        out_shape=[jax.ShapeDtypeStruct((m, n), F32), jax.ShapeDtypeStruct((m // tm, ms, n), F32)],
        compiler_params=_params("arbitrary", "arbitrary"),
        name="out_proj",
    )(xa, xb, sa, sb, w)
    return o, os[0]


def _shift_kernel(ck_ref, cv_ref, nk_ref, nv_ref, ok_ref, ov_ref, sems):
    n_layers, bsz, n_hist = ck_ref.shape[:3]
    copies = []
    for src, new, dst in ((ck_ref, nk_ref, ok_ref), (cv_ref, nv_ref, ov_ref)):
        for l in range(n_layers):
            for b in range(bsz):
                copies.append((src.at[l, b, pl.ds(1, n_hist - 1)], dst.at[l, b, pl.ds(0, n_hist - 1)]))
                copies.append((new.at[l, b], dst.at[l, b, pl.ds(n_hist - 1, 1)]))
    copies = [pltpu.make_async_copy(s, d, sems.at[n]) for n, (s, d) in enumerate(copies)]
    for c in copies:
        c.start()
    for c in copies:
        c.wait()


def _shift_windows(cache_k, cache_v, new_k, new_v):
    n_layers, bsz = cache_k.shape[:2]
    any_spec = pl.BlockSpec(memory_space=pl.ANY)
    return pl.pallas_call(
        _shift_kernel,
        in_specs=[any_spec] * 4,
        out_specs=[any_spec] * 2,
        out_shape=[jax.ShapeDtypeStruct(cache_k.shape, cache_k.dtype),
                   jax.ShapeDtypeStruct(cache_v.shape, cache_v.dtype)],
        scratch_shapes=[pltpu.SemaphoreType.DMA((4 * n_layers * bsz,))],
        name="shift_windows",
    )(cache_k, cache_v, new_k.astype(cache_k.dtype), new_v.astype(cache_v.dtype))


def _ln_kernel(x_ref, y_ref, g_ref, b_ref, of_ref, ob_ref, *, alpha):
    h = alpha * x_ref[...] + y_ref[...]
    mu = jnp.mean(h, axis=-1, keepdims=True)
    c = h - mu
    var = jnp.mean(c * c, axis=-1, keepdims=True)
    r = c * lax.rsqrt(var + LN_EPS) * g_ref[...] + b_ref[...]
    of_ref[...] = r
    ob_ref[...] = r.astype(BF16)


def _residual_ln(x, y, g, b, *, alpha, tm):
    m, d = x.shape
    tm = min(tm, m)
    assert m % tm == 0
    row = pl.BlockSpec((tm, d), lambda i: (i, 0))
    vec = pl.BlockSpec((1, d), lambda i: (0, 0))
    return pl.pallas_call(
        functools.partial(_ln_kernel, alpha=alpha),
        grid=(m // tm,),
        in_specs=[row, row, vec, vec],
        out_specs=[row, row],
        out_shape=[jax.ShapeDtypeStruct((m, d), F32), jax.ShapeDtypeStruct((m, d), BF16)],
        compiler_params=_params("parallel"),
        name="residual_ln",
    )(x, y, g.reshape(1, d), b.reshape(1, d))


def _rel_bucket_static(dist):
    exact = N_REL_BUCKETS // 2
    dist = np.asarray(dist, np.int64)
    ratio = np.log(np.maximum(dist, exact) / exact) / math.log(REL_MAX_DISTANCE / exact)
    large = np.minimum(exact + (ratio * (N_REL_BUCKETS - exact)).astype(np.int64), N_REL_BUCKETS - 1)
    return np.where(dist < exact, dist, large).astype(np.int32)


def _band_bias(rel_bias):
    n_heads = rel_bias.shape[1]
    period = 2 * BAND + 1
    out = []
    for window, dil in DILATIONS:
        steps = window // dil
        m = BAND - np.arange(period)
        valid = (m >= 0) & (m <= steps)
        tab = rel_bias[jnp.asarray(_rel_bucket_static(dil * np.clip(m, 0, steps)))].astype(F32)
        tab = jnp.where(jnp.asarray(valid)[:, None], tab, NEG).T
        rows = jnp.tile(tab, (1, BAND))[:, :BAND * 2 * BAND]
        out.append(rows.reshape(n_heads, BAND, 2 * BAND))
    return jnp.stack(out)


def _sample_bias(rel_bias, n_hist, past_len):
    hist, new = [], []
    for window, dil in DILATIONS:
        steps = window // dil
        m = steps - np.arange(steps)
        idx = n_hist - dil * m
        valid = (past_len - dil * m >= 0) & (idx >= 0)
        bias = rel_bias[jnp.asarray(_rel_bucket_static(dil * m))].astype(F32)
        hist.append(jnp.where(jnp.asarray(valid)[:, None], bias, NEG))
        new.append(rel_bias[int(_rel_bucket_static(0))].astype(F32))
    return jnp.stack(hist), jnp.stack(new)


DIL_UNROLL_FIRST = 4
DIL_UNROLL_REST = 3


def _dil_prompt_kernel(q_ref, k_ref, v_ref, g_ref, bm_ref, o_ref, og_ref, lse_ref, *, seq):
    scale = 1.0 / math.sqrt(HEAD_DIM)

    def rows(ref, start, n, dil):
        if dil == 1:
            return ref[pl.ds(start, n), :]
        return ref[pl.ds(start, n, stride=dil), :]

    def blocks(g, dil, bases, first):
        n = range(len(bases))
        back = 0 if first else dil * BAND
        n_keys = BAND if first else 2 * BAND
        bias = bm_ref[g, :, BAND:] if first else bm_ref[g]
        qs = [(rows(q_ref, b, BAND, dil) * scale).astype(BF16) for b in bases]
        ks = [rows(k_ref, b - back, n_keys, dil).astype(BF16) for b in bases]
        s = [_dot_nt(qs[u], ks[u]) + bias for u in n]
        mx = [jnp.max(s[u], axis=-1, keepdims=True) for u in n]
        p = [jnp.exp(s[u] - mx[u]) for u in n]
        l = [jnp.sum(p[u], axis=-1, keepdims=True) for u in n]
        vs = [rows(v_ref, b - back, n_keys, dil).astype(BF16) for b in bases]
        o = [_dot(p[u].astype(BF16), vs[u]) / l[u] for u in n]
        for u, b in enumerate(bases):
            lse = jnp.broadcast_to(mx[u] + jnp.log(l[u]), (BAND, HEAD_DIM))
            if dil == 1:
                og_ref[g, pl.ds(b, BAND), :] = o[u]
                lse_ref[g, pl.ds(b, BAND), :] = lse
            else:
                og_ref[g, pl.ds(b, BAND, stride=dil), :] = o[u]
                lse_ref[g, pl.ds(b, BAND, stride=dil), :] = lse

    def run(n_items, unroll, base_of, g, dil, first):
        main = n_items // unroll

        def body(t, carry):
            blocks(g, dil, [base_of(t * unroll + u) for u in range(unroll)], first)
            return carry

        if main:
            lax.fori_loop(0, main, body, 0)
        if n_items > main * unroll:
            blocks(g, dil, [base_of(idx) for idx in range(main * unroll, n_items)], first)

    for g, (window, dil) in enumerate(DILATIONS):
        n_blk = seq // dil // BAND
        run(dil, DIL_UNROLL_FIRST, lambda r: r, g, dil, True)
        if n_blk > 1:
            def rest_base(t, dil=dil, n_blk=n_blk):
                return t // (n_blk - 1) + dil * BAND * (t % (n_blk - 1) + 1)

            run(dil * (n_blk - 1), DIL_UNROLL_REST, rest_base, g, dil, False)

    tile = 256

    def merge(t, carry):
        sl = pl.ds(pl.multiple_of(t * tile, tile), tile)
        lses = [lse_ref[g, sl, :] for g in range(len(DILATIONS))]
        mx = functools.reduce(jnp.maximum, lses)
        ws = [jnp.exp(l - mx) for l in lses]
        den = functools.reduce(jnp.add, ws)
        num = functools.reduce(jnp.add, [w * og_ref[g, sl, :] for g, w in enumerate(ws)])
        o_ref[sl, :] = (num / den * _silu(g_ref[sl, :])).astype(o_ref.dtype)
        return carry

    lax.fori_loop(0, seq // tile, merge, 0)


def _dil_prompt(z3, band_bias, n_heads):
    bsz, seq, _ = z3.shape
    n_pat = len(DILATIONS)
    for window, dil in DILATIONS:
        assert seq % (dil * BAND) == 0 and window // dil == BAND

    def col(off):
        return pl.BlockSpec((None, seq, HEAD_DIM), lambda b, h, off=off: (b, 0, off * n_heads + h))

    return pl.pallas_call(
        functools.partial(_dil_prompt_kernel, seq=seq),
        grid=(bsz, n_heads),
        in_specs=[col(0), col(1), col(2), col(3),
                  pl.BlockSpec((n_pat, None, BAND, 2 * BAND), lambda b, h: (0, h, 0, 0))],
        out_specs=pl.BlockSpec((None, seq, HEAD_DIM), lambda b, h: (b, 0, h)),
        out_shape=jax.ShapeDtypeStruct((bsz, seq, n_heads * HEAD_DIM), BF16),
        scratch_shapes=[pltpu.VMEM((n_pat, seq, HEAD_DIM), F32), pltpu.VMEM((n_pat, seq, HEAD_DIM), F32)],
        compiler_params=_params("parallel", "parallel"),
        name="dilated_attn_prompt",
    )(z3, z3, z3, z3, band_bias)


POOL_HALO = 16


def _pool_prompt_kernel(u_ref, halo_ref, gate_ref, pw_ref, ps_ref, o_ref, *, tile):
    t = pl.program_id(1)
    grp = pw_ref.shape[1]
    halo = jnp.where(t > 0, halo_ref[...], 0.0)
    pos = (t * tile + lax.broadcasted_iota(jnp.int32, (tile, 1), 0) + 1).astype(F32)
    for g, win in enumerate(POOL_WINDOWS):
        cols = slice(g * grp, (g + 1) * grp)
        x = u_ref[:, cols]
        ext = jnp.concatenate([halo[:, cols], x], axis=0)
        shift = 1
        while shift < win:
            ext = ext + pltpu.roll(ext, shift, 0)
            shift *= 2
        wsum = ext[POOL_HALO:, :]
        pooled = wsum / jnp.minimum(pos, float(win)) - x
        y = _dot(pooled.astype(BF16), pw_ref[g].astype(BF16)) * ps_ref[:, cols]
        o_ref[:, cols] = (y * _silu(gate_ref[:, cols])).astype(o_ref.dtype)


def _pool_prompt(z3, pool_w, pool_scale, layer, *, u_off, gate_off, tile=256):
    bsz, seq, _ = z3.shape
    _, n_grp, grp, _ = pool_w.shape
    width = n_grp * grp
    assert seq % tile == 0 and tile % POOL_HALO == 0
    for win in POOL_WINDOWS:
        assert win & (win - 1) == 0 and win - 1 <= POOL_HALO
    per = tile // POOL_HALO
    return pl.pallas_call(
        functools.partial(_pool_prompt_kernel, tile=tile),
        grid=(bsz, seq // tile),
        in_specs=[pl.BlockSpec((None, tile, width), lambda b, t: (b, t, u_off)),
                  pl.BlockSpec((None, POOL_HALO, width),
                               lambda b, t: (b, jnp.maximum(t * per - 1, 0), u_off)),
                  pl.BlockSpec((None, tile, width), lambda b, t: (b, t, gate_off)),
                  pl.BlockSpec((None, n_grp, grp, grp), lambda b, t: (layer, 0, 0, 0)),
                  pl.BlockSpec((None, 1, width), lambda b, t: (layer, 0, 0))],
        out_specs=pl.BlockSpec((None, tile, width), lambda b, t: (b, t, 0)),
        out_shape=jax.ShapeDtypeStruct((bsz, seq, width), BF16),
        compiler_params=_params("parallel", "parallel"),
        name="pool_mix_prompt",
    )(z3, z3, z3, pool_w, pool_scale.reshape(pool_scale.shape[0], 1, width))


SB_TILE = 256
SB_HEADS = 4


def _suffix_matrix(n):
    j = np.arange(n)[:, None]
    s = np.arange(n)[None, :]
    return (j >= s).astype(np.float32)


def _sb_prompt_kernel(q_ref, k_ref, v_ref, g_ref, bias_ref, uu_ref, o_ref, kb_ref, vb_ref):
    i = pl.program_id(2)
    scale = 1.0 / math.sqrt(HEAD_DIM)

    @pl.when(i == 0)
    def _():
        kb_ref[...] = k_ref[...].astype(BF16)
        vb_ref[...] = v_ref[...].astype(BF16)

    row = lax.broadcasted_iota(jnp.int32, (SB_TILE, SB_TILE), 0)
    colid = lax.broadcasted_iota(jnp.int32, (SB_TILE, SB_TILE), 1)
    heads = [slice(h * HEAD_DIM, (h + 1) * HEAD_DIM) for h in range(SB_HEADS)]
    hs_all = range(SB_HEADS)
    qs = [(q_ref[:, heads[h]] * (scale * LOG2E)).astype(BF16) for h in hs_all]
    biases = [bias_ref[h][:, 0:1] * LOG2E for h in hs_all]

    def key_block(kb, spent, acc, diag):
        sl = pl.ds(pl.multiple_of(kb * SB_TILE, SB_TILE), SB_TILE)
        z2 = [_dot_nt(qs[h], kb_ref[sl, heads[h]]) + biases[h] for h in hs_all]
        lhs = []
        for h in hs_all:
            sp2 = jnp.maximum(z2[h], 0.0) + jnp.log(1.0 + jnp.exp2(-jnp.abs(z2[h]))) * LOG2E
            if diag:
                sp2 = jnp.where(colid < row, sp2, 0.0)
            hi, lo = _split_hi_lo(sp2)
            lhs.append(jnp.concatenate([hi, lo], axis=1))
        incl = [_dot(lhs[h], uu_ref[...]) for h in hs_all]
        a = []
        for h in hs_all:
            w = jnp.exp2(z2[h] - incl[h] - spent[h])
            if diag:
                w = jnp.where(colid < row, w, 0.0)
            a.append(w.astype(BF16))
        acc = [acc[h] + _dot(a[h], vb_ref[sl, heads[h]]) for h in hs_all]
        spent = [spent[h] + incl[h][:, 0:1] for h in hs_all]
        return spent, acc

    spent = [jnp.zeros((SB_TILE, 1), F32) for _ in hs_all]
    acc = [jnp.zeros((SB_TILE, HEAD_DIM), F32) for _ in hs_all]
    spent, acc = key_block(i, spent, acc, True)

    def body(n, st):
        sp_new, acc_new = key_block(i - 1 - n, list(st[:SB_HEADS]), list(st[SB_HEADS:]), False)
        return tuple(sp_new) + tuple(acc_new)

    state = lax.fori_loop(0, i, body, tuple(spent) + tuple(acc))
    for h in hs_all:
        o_ref[:, heads[h]] = (state[SB_HEADS + h] * _silu(g_ref[:, heads[h]])).astype(o_ref.dtype)


def _sb_prompt(z3, sb_bias, n_heads):
    bsz, seq, _ = z3.shape
    assert seq % SB_TILE == 0 and n_heads % SB_HEADS == 0
    n_grp = n_heads // SB_HEADS
    width = SB_HEADS * HEAD_DIM
    u = _suffix_matrix(SB_TILE)
    uu = jnp.asarray(np.concatenate([u, u], axis=0), BF16)
    bias = jnp.broadcast_to(sb_bias.astype(F32)[:, None, None], (n_heads, 1, LANES))

    def qcol(off):
        return pl.BlockSpec((None, SB_TILE, width), lambda b, h, i, off=off: (b, i, off * n_grp + h))

    def kcol(off):
        return pl.BlockSpec((None, seq, width), lambda b, h, i, off=off: (b, 0, off * n_grp + h))

    return pl.pallas_call(
        _sb_prompt_kernel,
        grid=(bsz, n_grp, seq // SB_TILE),
        in_specs=[qcol(0), kcol(1), kcol(2), qcol(3),
                  pl.BlockSpec((SB_HEADS, 1, LANES), lambda b, h, i: (h, 0, 0)),
                  pl.BlockSpec((2 * SB_TILE, SB_TILE), lambda b, h, i: (0, 0))],
        out_specs=pl.BlockSpec((None, SB_TILE, width), lambda b, h, i: (b, i, h)),
        out_shape=jax.ShapeDtypeStruct((bsz, seq, n_heads * HEAD_DIM), BF16),
        scratch_shapes=[pltpu.VMEM((seq, width), BF16), pltpu.VMEM((seq, width), BF16)],
        compiler_params=_params("parallel", "parallel", "arbitrary"),
        name="stick_breaking_prompt",
    )(z3, z3, z3, z3, bias, uu)


def _mlstm_prompt_kernel(q_ref, k_ref, v_ref, o_ref, g_ref, gc_ref, gr_ref, bc_ref, br_ref,
                         mix_ref, c_ref, n_ref, m_ref):
    chunk = pl.program_id(1)
    n_heads, dh, _ = c_ref.shape
    L = q_ref.shape[0]
    kscale = 1.0 / math.sqrt(dh)

    @pl.when(chunk == 0)
    def _():
        c_ref[...] = jnp.zeros_like(c_ref)
        n_ref[...] = jnp.zeros_like(n_ref)
        m_ref[...] = jnp.zeros_like(m_ref)

    t_idx = lax.broadcasted_iota(jnp.int32, (L, L), 0)
    s_idx = lax.broadcasted_iota(jnp.int32, (L, L), 1)
    causal = s_idx <= t_idx
    gates_c = gc_ref[...] + br_ref[...]
    gates_r = gr_ref[...] + bc_ref[...]
    for h in range(n_heads):
        cols = slice(h * dh, (h + 1) * dh)
        ii_c = gates_c[:, h:h + 1]
        lf_c = _log_sigmoid(gates_c[:, n_heads + h:n_heads + h + 1])
        ii_r = gates_r[h:h + 1, :]
        lf_r = _log_sigmoid(gates_r[n_heads + h:n_heads + h + 1, :])
        b_c = jnp.sum(jnp.where(causal, lf_r, 0.0), axis=1, keepdims=True)
        b_r = jnp.sum(jnp.where(t_idx <= s_idx, lf_c, 0.0), axis=0, keepdims=True)
        g_r = ii_r - b_r
        cm_c = jnp.max(jnp.where(causal, g_r, -jnp.inf), axis=1, keepdims=True)
        m0 = m_ref[h:h + 1, 0:1]
        m_t = b_c + jnp.maximum(m0, cm_c)
        inter = jnp.exp(b_c + m0 - m_t)
        dmat = jnp.where(causal, jnp.exp((b_c - m_t) + g_r), 0.0)
        q = q_ref[:, cols].astype(BF16)
        ks = k_ref[:, cols] * kscale
        v = v_ref[:, cols].astype(BF16)
        c0 = c_ref[h]
        n0 = n_ref[h:h + 1, :]
        sc = _dot_nt(q, ks.astype(BF16)) * dmat
        num = inter * _dot(q, c0.astype(BF16)) + _dot(sc.astype(BF16), v)
        qn = jnp.sum(q_ref[:, cols] * n0, axis=1, keepdims=True)
        den = inter * qn + jnp.sum(sc, axis=1, keepdims=True)
        hid = num / jnp.maximum(jnp.abs(den), jnp.exp(-m_t))
        m_last = m_t[L - 1:L, :]
        b_last = b_c[L - 1:L, :]
        decay = jnp.exp(b_last + m0 - m_last)
        wk = jnp.exp(b_last - b_c + ii_c - m_last)
        kw = ks * wk
        c_ref[h] = decay * c0 + _dot_tn(kw.astype(BF16), v)
        n_ref[h:h + 1, :] = decay * n0 + jnp.sum(kw, axis=0, keepdims=True)
        m_ref[h:h + 1, :] = jnp.broadcast_to(m_last, (1, m_ref.shape[1]))
        out = hid * jax.nn.sigmoid(o_ref[:, cols]) * _silu(g_ref[:, cols])
        mix_ref[:, cols] = out.astype(mix_ref.dtype)


def _mlstm_prompt(z3, zg3, gate_bias, *, col0, n_heads=N_HEADS_D, chunk=MLSTM_CHUNK):
    bsz, seq, _ = z3.shape
    width = (z3.shape[2] - col0) // 5
    dh = width // n_heads
    assert seq % chunk == 0 and col0 % width == 0 and zg3.shape[2] == 2 * n_heads == SUBLANES
    base = col0 // width
    zg_rows = jnp.transpose(zg3, (0, 2, 1))
    bias_lane = gate_bias.astype(F32).reshape(1, SUBLANES)
    bias_sub = bias_lane.reshape(SUBLANES, 1)

    def col(off):
        return pl.BlockSpec((None, chunk, width), lambda b, c, off=off: (b, c, base + off))

    mix, c_f, n_f, m_f = pl.pallas_call(
        _mlstm_prompt_kernel,
        grid=(bsz, seq // chunk),
        in_specs=[col(0), col(1), col(2), col(3), col(4),
                  pl.BlockSpec((None, chunk, SUBLANES), lambda b, c: (b, c, 0)),
                  pl.BlockSpec((None, SUBLANES, chunk), lambda b, c: (b, 0, c)),
                  pl.BlockSpec((SUBLANES, 1), lambda b, c: (0, 0)),
                  pl.BlockSpec((1, SUBLANES), lambda b, c: (0, 0))],
        out_specs=[pl.BlockSpec((None, chunk, width), lambda b, c: (b, c, 0)),
                   pl.BlockSpec((None, n_heads, dh, dh), lambda b, c: (b, 0, 0, 0)),
                   pl.BlockSpec((None, n_heads, dh), lambda b, c: (b, 0, 0)),
                   pl.BlockSpec((None, n_heads, LANES), lambda b, c: (b, 0, 0))],
        out_shape=[jax.ShapeDtypeStruct((bsz, seq, width), BF16),
                   jax.ShapeDtypeStruct((bsz, n_heads, dh, dh), F32),
                   jax.ShapeDtypeStruct((bsz, n_heads, dh), F32),
                   jax.ShapeDtypeStruct((bsz, n_heads, LANES), F32)],
        compiler_params=_params("parallel", "arbitrary"),
        name="mlstm_prompt",
    )(z3, z3, z3, z3, z3, zg3, zg_rows, bias_sub, bias_lane)
    return mix, c_f, n_f, m_f[:, :, 0]


def _group_tokens(n_heads):
    assert LANES % n_heads == 0
    return LANES // n_heads


def _stack_groups(zt, n_grp):
    return jnp.concatenate([zt[:, g * LANES:(g + 1) * LANES] for g in range(n_grp)], axis=0)


def _unstack_groups(a, n_grp, n_heads):
    return jnp.concatenate([a[g * n_heads:(g + 1) * n_heads, :] for g in range(n_grp)], axis=1)


def _own_head_mask(n_grp, n_heads):
    r = lax.broadcasted_iota(jnp.int32, (n_grp * n_heads, LANES), 0)
    c = lax.broadcasted_iota(jnp.int32, (n_grp * n_heads, LANES), 1)
    return (r % n_heads) == (c % n_heads)


def _dil_sample_kernel(*refs, n_pat):
    q_ref, kn_ref, vn_ref, g_ref, bh_ref, bn_ref = refs[:6]
    k_refs = refs[6:6 + n_pat]
    v_refs = refs[6 + n_pat:6 + 2 * n_pat]
    o_ref = refs[6 + 2 * n_pat]
    steps, n_heads, dh = k_refs[0].shape
    n_grp = steps * n_heads // LANES
    scale = 1.0 / math.sqrt(dh)
    q = q_ref[...] * scale
    qb = q.astype(BF16)
    zn = jnp.sum(q * kn_ref[...], axis=-1, keepdims=True)
    z_new = [zn + bn_ref[g][:, 0:1] for g in range(n_pat)]
    z_hist = []
    for g in range(n_pat):
        k2 = k_refs[g][...].reshape(steps * n_heads, dh).astype(BF16)
        z_hist.append(_stack_groups(_dot_nt(qb, k2), n_grp) + bh_ref[g])
    mx = functools.reduce(jnp.maximum, z_new)
    for z in z_hist:
        zr = jnp.max(z, axis=-1, keepdims=True)
        for gi in range(n_grp):
            mx = jnp.maximum(mx, zr[gi * n_heads:(gi + 1) * n_heads, :])
    mx_rows = jnp.concatenate([mx] * n_grp, axis=0)
    den = functools.reduce(jnp.add, [jnp.exp(z - mx) for z in z_new])
    w_new = den
    w_hist = [jnp.exp(z - mx_rows) for z in z_hist]
    for w in w_hist:
        wr = jnp.sum(w, axis=-1, keepdims=True)
        for gi in range(n_grp):
            den = den + wr[gi * n_heads:(gi + 1) * n_heads, :]
    inv = 1.0 / den
    inv_rows = jnp.concatenate([inv] * n_grp, axis=0)
    acc = (w_new * inv) * vn_ref[...]
    for g in range(n_pat):
        p = _unstack_groups((w_hist[g] * inv_rows).astype(BF16), n_grp, n_heads)
        v2 = v_refs[g][...].reshape(steps * n_heads, dh).astype(BF16)
        acc = acc + _dot(p, v2)
    o_ref[...] = (acc * _silu(g_ref[...])).astype(o_ref.dtype)


def _dil_sample(zs, buf_k, buf_v, layer, rel_bias, past_len):
    _, bsz, n_hist, n_heads, dh = buf_k.shape
    width = n_heads * dh
    n_pat = len(DILATIONS)
    steps = BAND
    tok = _group_tokens(n_heads)
    n_grp = steps // tok
    bias_hist, bias_new = _sample_bias(rel_bias, n_hist, past_len)
    bh = bias_hist.reshape(n_pat, n_grp, 1, tok, n_heads)
    own = jnp.asarray(np.eye(n_heads, dtype=bool)).reshape(1, 1, n_heads, 1, n_heads)
    bh = jnp.where(own, bh, NEG).reshape(n_pat, n_grp * n_heads, LANES)
    bn = jnp.broadcast_to(bias_new[:, :, None], (n_pat, n_heads, LANES))
    z4 = zs[:, :4 * width].reshape(bsz, 4, n_heads, dh)
    k_views, v_views, kv_specs = [], [], []
    for window, dil in DILATIONS:
        assert window // dil == steps and n_hist % window == 0 and window % dil == 0
        shape = (buf_k.shape[0], bsz, n_hist // dil, dil, n_heads, dh)
        k_views.append(buf_k.reshape(shape))
        v_views.append(buf_v.reshape(shape))
        blk = n_hist // window - 1
        kv_specs.append(pl.BlockSpec((None, None, steps, None, n_heads, dh),
                                     lambda b, blk=blk: (layer, b, blk, 0, 0, 0)))

    def zrow(off):
        return pl.BlockSpec((None, None, n_heads, dh), lambda b, off=off: (b, off, 0, 0))

    return pl.pallas_call(
        functools.partial(_dil_sample_kernel, n_pat=n_pat),
        grid=(bsz,),
        in_specs=[zrow(0), zrow(1), zrow(2), zrow(3),
                  pl.BlockSpec((n_pat, n_grp * n_heads, LANES), lambda b: (0, 0, 0)),
                  pl.BlockSpec((n_pat, n_heads, LANES), lambda b: (0, 0, 0))] + kv_specs + kv_specs,
        out_specs=pl.BlockSpec((None, n_heads, dh), lambda b: (b, 0, 0)),
        out_shape=jax.ShapeDtypeStruct((bsz, n_heads, dh), BF16),
        compiler_params=_params("parallel"),
        name="dilated_attn_sample",
    )(z4, z4, z4, z4, bh, bn, *k_views, *v_views).reshape(bsz, width)


def _pool_sample_kernel(u_ref, hist_ref, gate_ref, pw_ref, ps_ref, o_ref):
    grp = pw_ref.shape[1]
    n_hist = hist_ref.shape[0]
    for g, win in enumerate(POOL_WINDOWS):
        cols = slice(g * grp, (g + 1) * grp)
        x = u_ref[:, cols]
        n_old = min(win - 1, n_hist)
        wsum = x
        for r in range(n_hist - n_old, n_hist):
            wsum = wsum + hist_ref[r, :, cols]
        pooled = wsum / float(n_old + 1) - x
        y = _dot(pooled.astype(BF16), pw_ref[g].astype(BF16)) * ps_ref[:, cols]
        o_ref[:, cols] = (y * _silu(gate_ref[:, cols])).astype(o_ref.dtype)


def _pool_sample(zs, hist, pool_w, pool_scale, layer, *, u_off, gate_off):
    bsz = zs.shape[0]
    _, n_grp, grp, _ = pool_w.shape
    width = n_grp * grp
    n_hist = hist.shape[1]
    hist = jnp.transpose(hist, (1, 0, 2))
    return pl.pallas_call(
        _pool_sample_kernel,
        grid=(1,),
        in_specs=[pl.BlockSpec((bsz, width), lambda i: (0, u_off)),
                  pl.BlockSpec((n_hist, bsz, width), lambda i: (0, 0, 0)),
                  pl.BlockSpec((bsz, width), lambda i: (0, gate_off)),
                  pl.BlockSpec((None, n_grp, grp, grp), lambda i: (layer, 0, 0, 0)),
                  pl.BlockSpec((None, 1, width), lambda i: (layer, 0, 0))],
        out_specs=pl.BlockSpec((bsz, width), lambda i: (0, 0)),
        out_shape=jax.ShapeDtypeStruct((bsz, width), BF16),
        compiler_params=_params("arbitrary"),
        name="pool_mix_sample",
    )(zs, hist, zs, pool_w, pool_scale.reshape(pool_scale.shape[0], 1, width))


SB_SAMPLE_PAGES = 4


def _sb_sample_kernel(pt_ref, q_ref, g_ref, bias_ref, ut_ref, *refs):
    k_refs = refs[:SB_SAMPLE_PAGES]
    v_refs = refs[SB_SAMPLE_PAGES:2 * SB_SAMPLE_PAGES]
    o_ref, carry_sc, acc_sc = refs[2 * SB_SAMPLE_PAGES:]
    p = pl.program_id(1)
    page, n_heads, dh = k_refs[0].shape
    n_grp = page * n_heads // LANES
    scale = 1.0 / math.sqrt(dh)
    pages = range(SB_SAMPLE_PAGES)

    @pl.when(p == 0)
    def _():
        carry_sc[...] = jnp.zeros_like(carry_sc)
        acc_sc[...] = jnp.zeros_like(acc_sc)

    qb = (q_ref[...] * scale).astype(BF16)
    own = _own_head_mask(n_grp, n_heads)
    z = [_stack_groups(_dot_nt(qb, k_refs[u][...].reshape(page * n_heads, dh).astype(BF16)), n_grp)
         + bias_ref[...] for u in pages]
    lhs, log_take = [], []
    for u in pages:
        sp = _softplus(z[u])
        hi, lo = _split_hi_lo(jnp.where(own, -sp, 0.0))
        lhs.append(jnp.concatenate([hi, lo], axis=1))
        log_take.append(z[u] - sp)
    sums = [_dot(lhs[u], ut_ref[...]) for u in pages]
    after = carry_sc[...]
    offs = []
    for u in pages:
        total = sums[u][:, LANES:]
        offs_u = [None] * n_grp
        for gi in range(n_grp - 1, -1, -1):
            offs_u[gi] = after
            after = after + total[gi * n_heads:(gi + 1) * n_heads, :]
        offs.append(jnp.concatenate(offs_u, axis=0))
    carry_sc[...] = after
    acc = acc_sc[...]
    for u in pages:
        a = jnp.where(own, jnp.exp(log_take[u] + sums[u][:, :LANES] + offs[u]), 0.0)
        v2 = v_refs[u][...].reshape(page * n_heads, dh).astype(BF16)
        acc = acc + _dot(_unstack_groups(a.astype(BF16), n_grp, n_heads), v2)
    acc_sc[...] = acc

    @pl.when(p == pl.num_programs(1) - 1)
    def _():
        o_ref[...] = (acc_sc[...] * _silu(g_ref[...])).astype(o_ref.dtype)


def _sb_sample(zs, pool_k, pool_v, layer, page_table, sb_bias):
    bsz, n_pages = page_table.shape
    _, _, page, n_heads, dh = pool_k.shape
    width = n_heads * dh
    tok = _group_tokens(n_heads)
    n_grp = page // tok
    lane_tok = np.arange(LANES) // n_heads
    lane_head = np.arange(LANES) % n_heads
    later = (lane_head[:, None] == lane_head[None, :]) & (lane_tok[:, None] > lane_tok[None, :])
    half = np.concatenate([later.astype(np.float32), np.ones((LANES, LANES), np.float32)], axis=1)
    ut = jnp.asarray(np.concatenate([half, half], axis=0), BF16)
    z4 = zs[:, :4 * width].reshape(bsz, 4, n_heads, dh)
    bias = jnp.tile(jnp.broadcast_to(sb_bias.astype(F32)[:, None], (n_heads, LANES)), (n_grp, 1))

    def page_spec(u):
        return pl.BlockSpec((None, None, page, n_heads, dh),
                            lambda b, p, pt: (layer, pt[b, n_pages - 1 - (p * SB_SAMPLE_PAGES + u)], 0, 0, 0))

    assert n_pages % SB_SAMPLE_PAGES == 0
    page_specs = [page_spec(u) for u in range(SB_SAMPLE_PAGES)]
    return pl.pallas_call(
        _sb_sample_kernel,
        grid_spec=pltpu.PrefetchScalarGridSpec(
            num_scalar_prefetch=1,
            grid=(bsz, n_pages // SB_SAMPLE_PAGES),
            in_specs=[pl.BlockSpec((None, None, n_heads, dh), lambda b, p, pt: (b, 0, 0, 0)),
                      pl.BlockSpec((None, None, n_heads, dh), lambda b, p, pt: (b, 3, 0, 0)),
                      pl.BlockSpec((n_grp * n_heads, LANES), lambda b, p, pt: (0, 0)),
                      pl.BlockSpec((2 * LANES, 2 * LANES), lambda b, p, pt: (0, 0))] + page_specs + page_specs,
            out_specs=pl.BlockSpec((None, n_heads, dh), lambda b, p, pt: (b, 0, 0)),
            scratch_shapes=[pltpu.VMEM((n_heads, LANES), F32), pltpu.VMEM((n_heads, dh), F32)]),
        out_shape=jax.ShapeDtypeStruct((bsz, n_heads, dh), BF16),
        compiler_params=_params("parallel", "arbitrary"),
        name="stick_breaking_sample",
    )(page_table, z4, z4, bias, ut, *([pool_k] * SB_SAMPLE_PAGES), *([pool_v] * SB_SAMPLE_PAGES)).reshape(bsz, width)


def _mlstm_sample_kernel(q_ref, k_ref, v_ref, o_ref, g_ref, sc_ref, c0_ref, n0_ref,
                         mix_ref, c_ref, n_ref, m_ref):
    dh = c0_ref.shape[0]
    kscale = 1.0 / math.sqrt(dh)
    s = sc_ref[...]
    ii = s[:, 0:1] + s[:, 2:3]
    lf = _log_sigmoid(s[:, 1:2] + s[:, 3:4])
    m0 = s[:, 4:5]
    m_t = jnp.maximum(lf + m0, ii)
    inter = jnp.exp(lf + m0 - m_t)
    wk = jnp.exp(ii - m_t)
    q = q_ref[...]
    ks = k_ref[...] * kscale
    v = v_ref[...]
    c0 = c0_ref[...]
    n0 = n0_ref[...]
    sc = jnp.sum(q * ks, axis=0, keepdims=True) * wk
    num = inter * jnp.sum(c0 * q, axis=0, keepdims=True) + sc * v
    den = inter * jnp.sum(q * n0, axis=0, keepdims=True) + sc
    hid = num / jnp.maximum(jnp.abs(den), jnp.exp(-m_t))
    c_ref[...] = inter * c0 + (ks * wk) * v
    n_ref[...] = inter * n0 + ks * wk
    m_ref[...] = jnp.broadcast_to(m_t, m_ref.shape)
    mix_ref[...] = (hid * jax.nn.sigmoid(o_ref[...]) * _silu(g_ref[...])).astype(mix_ref.dtype)


def _mlstm_sample(zs, zg, gate_bias, c0, n0, m0, layer, *, col0, n_heads=N_HEADS_D):
    bsz = zs.shape[0]
    dh = c0.shape[-1]
    width = n_heads * dh
    grp = zs[:, col0:col0 + 5 * width].reshape(bsz, 5, n_heads, dh)
    q_col = grp[:, 0].reshape(bsz, n_heads, dh, 1)
    k_col = grp[:, 1].reshape(bsz, n_heads, dh, 1)
    v_row = grp[:, 2].reshape(bsz, n_heads, 1, dh)
    o_row = grp[:, 3].reshape(bsz, n_heads, 1, dh)
    g_row = grp[:, 4].reshape(bsz, n_heads, 1, dh)
    gb = gate_bias.astype(F32)
    scal = jnp.stack([zg[:, :n_heads], zg[:, n_heads:2 * n_heads],
                      jnp.broadcast_to(gb[0][None], (bsz, n_heads)),
                      jnp.broadcast_to(gb[1][None], (bsz, n_heads)),
                      m0.astype(F32)], axis=-1)
    scal = jnp.pad(scal, ((0, 0), (0, 0), (0, LANES - scal.shape[-1]))).reshape(bsz, n_heads, 1, LANES)

    def spec(r, c):
        return pl.BlockSpec((None, None, r, c), lambda b, h: (b, h, 0, 0))

    mix, c_n, n_n, m_n = pl.pallas_call(
        _mlstm_sample_kernel,
        grid=(bsz, n_heads),
        in_specs=[spec(dh, 1), spec(dh, 1), spec(1, dh), spec(1, dh), spec(1, dh), spec(1, LANES),
                  pl.BlockSpec((None, None, None, dh, dh), lambda b, h: (layer, b, h, 0, 0)),
                  spec(dh, 1)],
        out_specs=[spec(1, dh), spec(dh, dh), spec(dh, 1), spec(1, LANES)],
        out_shape=[jax.ShapeDtypeStruct((bsz, n_heads, 1, dh), BF16),
                   jax.ShapeDtypeStruct((bsz, n_heads, dh, dh), F32),
                   jax.ShapeDtypeStruct((bsz, n_heads, dh, 1), F32),
                   jax.ShapeDtypeStruct((bsz, n_heads, 1, LANES), F32)],
        compiler_params=_params("parallel", "parallel"),
        name="mlstm_sample",
    )(q_col, k_col, v_row, o_row, g_row, scal, c0, n0.astype(F32).reshape(bsz, n_heads, dh, 1))
    return mix.reshape(bsz, width), c_n, n_n.reshape(bsz, n_heads, dh), m_n[:, :, 0, 0]


PROJ_TM = 2048
PROJ_TN = 512


def kernel(x_prompt, x_sample, cache_a_k, cache_a_v, state_pool, cache_c_k, cache_c_v, state_mlstm_c,
           state_mlstm_n, state_mlstm_m, page_table, rel_bias, w_in_even, pool_w, pool_scale, w_out_even,
           w_in_odd, sb_bias, mlstm_gate_bias, w_out_odd, ln_g, ln_b):
    bsz, seq, d_model = x_prompt.shape
    dbs, dec_seq, _ = x_sample.shape
    assert dec_seq == 1
    depth = ln_g.shape[0]
    alpha = (2 * depth) ** 0.25
    past_len = page_table.shape[1] * PAGE_SIZE
    mix_a = d_model // 2
    n_heads = mix_a // HEAD_DIM
    mix_b = d_model - mix_a
    n_rows = bsz * seq
    even_cols = w_in_even.shape[2]
    odd_main = 9 * mix_a
    n_gate = w_in_odd.shape[2] - odd_main
    w_in_odd_nk = jnp.swapaxes(w_in_odd, 1, 2)

    xp_f = x_prompt.reshape(n_rows, d_model)
    xp_b = xp_f.astype(BF16)
    xs_f = x_sample.reshape(dbs, d_model)
    xs_b = xs_f.astype(BF16)
    band_bias = _band_bias(rel_bias)
    u_blk = 4 * mix_a // mix_b

    zp_even, zp_odd, zs_even, zs_odd = [], [], [], []
    mcp, mnp, mmp, mcs, mns, mms = [], [], [], [], [], []
    for layer in range(depth):
        j = layer // 2
        if layer % 2 == 0:
            z, zs = _proj(xp_b, xs_b, w_in_even, j, even_cols, w_is_nk=False, tm=PROJ_TM, tn=PROJ_TN)
            z3 = z.reshape(bsz, seq, -1)
            oa = _dil_prompt(z3, band_bias, n_heads)
            ob = _pool_prompt(z3, pool_w, pool_scale, j, u_off=u_blk, gate_off=u_blk + 1)
            oas = _dil_sample(zs, cache_a_k, cache_a_v, j, rel_bias, past_len)
            obs = _pool_sample(zs, state_pool[j], pool_w, pool_scale, j, u_off=u_blk, gate_off=u_blk + 1)
            yp, ys = _out_proj(oa.reshape(n_rows, mix_a), ob.reshape(n_rows, mix_b), oas, obs, w_out_even, j,
                               tm=PROJ_TM, tn=PROJ_TN)
            zp_even.append(z3)
            zs_even.append(zs)
        else:
            z, zs = _proj(xp_b, xs_b, w_in_odd_nk, j, odd_main, w_is_nk=True, tm=PROJ_TM, tn=PROJ_TN)
            zg, zgs = _gate_proj(xp_b, xs_b, w_in_odd_nk, j, odd_main, n_gate, tm=PROJ_TM)
            z3 = z.reshape(bsz, seq, -1)
            oc = _sb_prompt(z3, sb_bias[j], n_heads)
            od, c_f, n_f, m_f = _mlstm_prompt(z3, zg.reshape(bsz, seq, n_gate), mlstm_gate_bias[j], col0=4 * mix_a)
            ocs = _sb_sample(zs, cache_c_k, cache_c_v, j, page_table, sb_bias[j])
            ods, c_n, n_n, m_n = _mlstm_sample(zs, zgs, mlstm_gate_bias[j], state_mlstm_c, state_mlstm_n[j],
                                               state_mlstm_m[j], j, col0=4 * mix_a)
            yp, ys = _out_proj(oc.reshape(n_rows, mix_a), od.reshape(n_rows, mix_b), ocs, ods, w_out_odd, j,
                               tm=PROJ_TM, tn=PROJ_TN)
            zp_odd.append(z3)
            zs_odd.append(zs)
            mcp.append(c_f)
            mnp.append(n_f)
            mmp.append(m_f)
            mcs.append(c_n)
            mns.append(n_n)
            mms.append(m_n)
        xp_f, xp_b = _residual_ln(xp_f, yp, ln_g[layer], ln_b[layer], alpha=alpha, tm=256)
        xs_f, xs_b = _residual_ln(xs_f, ys, ln_g[layer], ln_b[layer], alpha=alpha, tm=dbs)

    keep_a = min(A_WINDOW, seq)
    keep_u = min(POOL_HIST, seq)

    def heads(a):
        return a.reshape(a.shape[:-1] + (n_heads, HEAD_DIM))

    def pick(zs, rows, lo, hi):
        return jnp.stack([z[..., rows, lo:hi] for z in zs])

    every = slice(None)
    new_row = np.newaxis
    a_k_prompt = heads(pick(zp_even, slice(seq - keep_a, seq), mix_a, 2 * mix_a))
    a_v_prompt = heads(pick(zp_even, slice(seq - keep_a, seq), 2 * mix_a, 3 * mix_a))
    pool_prompt = pick(zp_even, slice(seq - keep_u, seq), 4 * mix_a, 4 * mix_a + mix_b)
    a_k_sample, a_v_sample = _shift_windows(
        cache_a_k, cache_a_v, heads(pick(zs_even, every, mix_a, 2 * mix_a))[:, :, new_row],
        heads(pick(zs_even, every, 2 * mix_a, 3 * mix_a))[:, :, new_row])
    pool_sample = jnp.concatenate([state_pool[:, :, 1:],
                                   pick(zs_even, every, 4 * mix_a, 4 * mix_a + mix_b)[:, :, new_row]], axis=2)
    c_k_prompt = heads(pick(zp_odd, every, mix_a, 2 * mix_a))
    c_v_prompt = heads(pick(zp_odd, every, 2 * mix_a, 3 * mix_a))
    c_k_sample = heads(pick(zs_odd, every, mix_a, 2 * mix_a))[:, :, new_row]
    c_v_sample = heads(pick(zs_odd, every, 2 * mix_a, 3 * mix_a))[:, :, new_row]
    return (xp_f.reshape(bsz, seq, d_model), xs_f.reshape(dbs, 1, d_model),
            a_k_prompt, a_v_prompt, a_k_sample, a_v_sample, pool_prompt, pool_sample,
            c_k_prompt, c_v_prompt, c_k_sample, c_v_sample,
            jnp.stack(mcp), jnp.stack(mnp), jnp.stack(mmp), jnp.stack(mcs), jnp.stack(mns), jnp.stack(mms))
```

```python
import functools
import math

import numpy as np
import jax
import jax.numpy as jnp
from jax import lax
from jax.experimental import pallas as pl
from jax.experimental.pallas import tpu as pltpu

HEAD_DIM = 128
DILATIONS = ((128, 1), (512, 4), (2048, 16))
A_WINDOW = max(w for w, _ in DILATIONS)
BAND = 128
POOL_WINDOWS = (2, 4, 8, 16)
POOL_HIST = max(POOL_WINDOWS) - 1
N_HEADS_D = 4
MLSTM_CHUNK = 128
N_REL_BUCKETS = 32
REL_MAX_DISTANCE = A_WINDOW
LN_EPS = 1e-5
LOG2E = 1.0 / math.log(2.0)
PAGE_SIZE = 128
NEG = -1e30

VMEM_LIMIT = 56 * 1024 * 1024
LANES = 128
SUBLANES = 8

BF16 = jnp.bfloat16
F32 = jnp.float32


def _params(*sem):
    return pltpu.CompilerParams(dimension_semantics=sem, vmem_limit_bytes=VMEM_LIMIT)


def _silu(x):
    return x * jax.nn.sigmoid(x)


def _dot(a, b):
    return jnp.dot(a, b, preferred_element_type=F32)


def _dot_nt(a, b):
    return lax.dot_general(a, b, (((1,), (1,)), ((), ())), preferred_element_type=F32)


def _dot_tn(a, b):
    return lax.dot_general(a, b, (((0,), (0,)), ((), ())), preferred_element_type=F32)


def _split_hi_lo(x):
    hi = x.astype(BF16)
    lo = (x - hi.astype(F32)).astype(BF16)
    return hi, lo


def _softplus(z):
    return jnp.maximum(z, 0.0) + jnp.log(1.0 + jnp.exp(-jnp.abs(z)))


def _log_sigmoid(z):
    return -_softplus(-z)


def _proj_kernel(x_ref, xs_ref, w_ref, o_ref, os_ref, *, w_is_nk):
    wb = w_ref[...].astype(BF16)
    mm = _dot_nt if w_is_nk else _dot
    o_ref[...] = mm(x_ref[...], wb)
    os_ref[...] = mm(xs_ref[...], wb)


def _proj(x, xs, w, layer, n_cols, *, w_is_nk, tm, tn):
    m, k = x.shape
    ms = xs.shape[0]
    assert m % tm == 0 and n_cols % tn == 0 and w.shape[2 if w_is_nk else 1] == k
    if w_is_nk:
        w_spec = pl.BlockSpec((None, tn, k), lambda i, j: (layer, j, 0))
    else:
        w_spec = pl.BlockSpec((None, k, tn), lambda i, j: (layer, 0, j))
    o, os = pl.pallas_call(
        functools.partial(_proj_kernel, w_is_nk=w_is_nk),
        grid=(m // tm, n_cols // tn),
        in_specs=[pl.BlockSpec((tm, k), lambda i, j: (i, 0), pipeline_mode=pl.Buffered(1)),
                  pl.BlockSpec((ms, k), lambda i, j: (0, 0)),
                  w_spec],
        out_specs=[pl.BlockSpec((tm, tn), lambda i, j: (i, j)),
                   pl.BlockSpec((None, ms, tn), lambda i, j: (i, 0, j))],
        out_shape=[jax.ShapeDtypeStruct((m, n_cols), F32), jax.ShapeDtypeStruct((m // tm, ms, n_cols), F32)],
        compiler_params=_params("arbitrary", "arbitrary"),
        name="proj",
    )(x, xs, w)
    return o, os[0]


def _gate_proj_kernel(x_ref, xs_ref, w_ref, o_ref, os_ref):
    wb = w_ref[...].astype(BF16)
    o_ref[...] = _dot_nt(x_ref[...], wb)
    os_ref[...] = _dot_nt(xs_ref[...], wb)


def _gate_proj(x, xs, w_nk, layer, row0, n_gate, *, tm):
    m, k = x.shape
    ms = xs.shape[0]
    assert m % tm == 0 and row0 % n_gate == 0 and n_gate % SUBLANES == 0
    o, os = pl.pallas_call(
        _gate_proj_kernel,
        grid=(m // tm,),
        in_specs=[pl.BlockSpec((tm, k), lambda i: (i, 0)),
                  pl.BlockSpec((ms, k), lambda i: (0, 0)),
                  pl.BlockSpec((None, n_gate, k), lambda i: (layer, row0 // n_gate, 0))],
        out_specs=[pl.BlockSpec((tm, n_gate), lambda i: (i, 0)),
                   pl.BlockSpec((None, ms, n_gate), lambda i: (i, 0, 0))],
        out_shape=[jax.ShapeDtypeStruct((m, n_gate), F32), jax.ShapeDtypeStruct((m // tm, ms, n_gate), F32)],
        compiler_params=_params("arbitrary"),
        name="gate_proj",
    )(x, xs, w_nk)
    return o, os[0]


def _out_proj_kernel(xa_ref, xb_ref, sa_ref, sb_ref, w_ref, o_ref, os_ref):
    ka = xa_ref.shape[1]
    wa = w_ref[:ka, :].astype(BF16)
    wb = w_ref[ka:, :].astype(BF16)
    o_ref[...] = _dot(xa_ref[...], wa) + _dot(xb_ref[...], wb)
    os_ref[...] = _dot(sa_ref[...], wa) + _dot(sb_ref[...], wb)


def _out_proj(xa, xb, sa, sb, w, layer, *, tm, tn):
    m, ka = xa.shape
    kb = xb.shape[1]
    ms = sa.shape[0]
    n = w.shape[2]
    assert m % tm == 0 and n % tn == 0 and w.shape[1] == ka + kb
    o, os = pl.pallas_call(
        _out_proj_kernel,
        grid=(m // tm, n // tn),
        in_specs=[pl.BlockSpec((tm, ka), lambda i, j: (i, 0), pipeline_mode=pl.Buffered(1)),
                  pl.BlockSpec((tm, kb), lambda i, j: (i, 0), pipeline_mode=pl.Buffered(1)),
                  pl.BlockSpec((ms, ka), lambda i, j: (0, 0)),
                  pl.BlockSpec((ms, kb), lambda i, j: (0, 0)),
                  pl.BlockSpec((None, ka + kb, tn), lambda i, j: (layer, 0, j))],
        out_specs=[pl.BlockSpec((tm, tn), lambda i, j: (i, j)),
                   pl.BlockSpec((None, ms, tn), lambda i, j: (i, 0, j))],
        out_shape=[jax.ShapeDtypeStruct((m, n), F32), jax.ShapeDtypeStruct((m // tm, ms, n), F32)],
        compiler_params=_params("arbitrary", "arbitrary"),
        name="out_proj",
    )(xa, xb, sa, sb, w)
    return o, os[0]


SHIFT_ROWS = 256


def _shift_kernel(ck_ref, cv_ref, nk_ref, nv_ref, ok_ref, ov_ref, *, per_window):
    c = pl.program_id(0)
    rows = ok_ref.shape[0]
    last = c == pl.num_programs(0) - 1
    for src, new, dst in ((ck_ref, nk_ref, ok_ref), (cv_ref, nv_ref, ov_ref)):
        @pl.when(jnp.logical_not(last))
        def _(src=src, dst=dst):
            dst[...] = src[...]

        @pl.when(last)
        def _(src=src, dst=dst):
            dst[0:rows - 1] = src[1:rows]

        @pl.when(c % per_window == per_window - 1)
        def _(new=new, dst=dst):
            dst[rows - 1:rows] = new[...]


def _shift_windows(cache_k, cache_v, new_k, new_v):
    n_layers, bsz, n_hist, n_heads, dh = cache_k.shape
    total = n_layers * bsz * n_hist
    assert n_hist % SHIFT_ROWS == 0
    per_window = n_hist // SHIFT_ROWS
    flat = (total, n_heads, dh)
    src_spec = pl.BlockSpec((pl.Element(SHIFT_ROWS), pl.Element(n_heads), pl.Element(dh)),
                            lambda c: (jnp.minimum(c * SHIFT_ROWS + 1, total - SHIFT_ROWS), 0, 0))
    new_spec = pl.BlockSpec((None, 1, n_heads, dh), lambda c: (c // per_window, 0, 0, 0))
    dst_spec = pl.BlockSpec((SHIFT_ROWS, n_heads, dh), lambda c: (c, 0, 0))
    ok, ov = pl.pallas_call(
        functools.partial(_shift_kernel, per_window=per_window),
        grid=(total // SHIFT_ROWS,),
        in_specs=[src_spec, src_spec, new_spec, new_spec],
        out_specs=[dst_spec, dst_spec],
        out_shape=[jax.ShapeDtypeStruct(flat, cache_k.dtype), jax.ShapeDtypeStruct(flat, cache_v.dtype)],
        compiler_params=_params("arbitrary"),
        name="shift_windows",
    )(cache_k.reshape(flat), cache_v.reshape(flat),
      new_k.astype(cache_k.dtype).reshape(n_layers * bsz, 1, n_heads, dh),
      new_v.astype(cache_v.dtype).reshape(n_layers * bsz, 1, n_heads, dh))
    return ok.reshape(cache_k.shape), ov.reshape(cache_v.shape)


def _ln_kernel(x_ref, y_ref, g_ref, b_ref, of_ref, ob_ref, *, alpha):
    h = alpha * x_ref[...] + y_ref[...]
    mu = jnp.mean(h, axis=-1, keepdims=True)
    c = h - mu
    var = jnp.mean(c * c, axis=-1, keepdims=True)
    r = c * lax.rsqrt(var + LN_EPS) * g_ref[...] + b_ref[...]
    of_ref[...] = r
    ob_ref[...] = r.astype(BF16)


def _residual_ln(x, y, g, b, *, alpha, tm):
    m, d = x.shape
    tm = min(tm, m)
    assert m % tm == 0
    row = pl.BlockSpec((tm, d), lambda i: (i, 0))
    vec = pl.BlockSpec((1, d), lambda i: (0, 0))
    return pl.pallas_call(
        functools.partial(_ln_kernel, alpha=alpha),
        grid=(m // tm,),
        in_specs=[row, row, vec, vec],
        out_specs=[row, row],
        out_shape=[jax.ShapeDtypeStruct((m, d), F32), jax.ShapeDtypeStruct((m, d), BF16)],
        compiler_params=_params("parallel"),
        name="residual_ln",
    )(x, y, g.reshape(1, d), b.reshape(1, d))


def _rel_bucket_static(dist):
    exact = N_REL_BUCKETS // 2
    dist = np.asarray(dist, np.int64)
    ratio = np.log(np.maximum(dist, exact) / exact) / math.log(REL_MAX_DISTANCE / exact)
    large = np.minimum(exact + (ratio * (N_REL_BUCKETS - exact)).astype(np.int64), N_REL_BUCKETS - 1)
    return np.where(dist < exact, dist, large).astype(np.int32)


def _band_bias(rel_bias):
    n_heads = rel_bias.shape[1]
    period = 2 * BAND + 1
    out = []
    for window, dil in DILATIONS:
        steps = window // dil
        m = BAND - np.arange(period)
        valid = (m >= 0) & (m <= steps)
        tab = rel_bias[jnp.asarray(_rel_bucket_static(dil * np.clip(m, 0, steps)))].astype(F32)
        tab = jnp.where(jnp.asarray(valid)[:, None], tab, NEG).T
        rows = jnp.tile(tab, (1, BAND))[:, :BAND * 2 * BAND]
        out.append(rows.reshape(n_heads, BAND, 2 * BAND))
    return jnp.stack(out)


def _sample_bias(rel_bias, n_hist, past_len):
    hist, new = [], []
    for window, dil in DILATIONS:
        steps = window // dil
        m = steps - np.arange(steps)
        idx = n_hist - dil * m
        valid = (past_len - dil * m >= 0) & (idx >= 0)
        bias = rel_bias[jnp.asarray(_rel_bucket_static(dil * m))].astype(F32)
        hist.append(jnp.where(jnp.asarray(valid)[:, None], bias, NEG))
        new.append(rel_bias[int(_rel_bucket_static(0))].astype(F32))
    return jnp.stack(hist), jnp.stack(new)


DIL_UNROLL_FIRST = 8
DIL_UNROLL_REST = 5


def _largest_divisor(n, cap):
    return max(d for d in range(1, cap + 1) if n % d == 0)


def _dil_prompt_kernel(q_ref, k_ref, v_ref, g_ref, bm_ref, o_ref, og_ref, lse_ref, *, seq):
    scale = 1.0 / math.sqrt(HEAD_DIM)

    def rows(ref, start, n, dil):
        if dil == 1:
            return ref[pl.ds(start, n), :]
        return ref[pl.ds(start, n, stride=dil), :]

    def blocks(g, dil, bases, first):
        n = range(len(bases))
        back = 0 if first else dil * BAND
        n_keys = BAND if first else 2 * BAND
        bias = bm_ref[g, :, BAND:] if first else bm_ref[g]
        qs = [(rows(q_ref, b, BAND, dil) * scale).astype(BF16) for b in bases]
        ks = [rows(k_ref, b - back, n_keys, dil).astype(BF16) for b in bases]
        s = [_dot_nt(qs[u], ks[u]) + bias for u in n]
        mx = [jnp.max(s[u], axis=-1, keepdims=True) for u in n]
        p = [jnp.exp(s[u] - mx[u]) for u in n]
        l = [jnp.sum(p[u], axis=-1, keepdims=True) for u in n]
        vs = [rows(v_ref, b - back, n_keys, dil).astype(BF16) for b in bases]
        o = [_dot(p[u].astype(BF16), vs[u]) / l[u] for u in n]
        for u, b in enumerate(bases):
            lse = jnp.broadcast_to(mx[u] + jnp.log(l[u]), (BAND, HEAD_DIM))
            if dil == 1:
                og_ref[g, pl.ds(b, BAND), :] = o[u]
                lse_ref[g, pl.ds(b, BAND), :] = lse
            else:
                og_ref[g, pl.ds(b, BAND, stride=dil), :] = o[u]
                lse_ref[g, pl.ds(b, BAND, stride=dil), :] = lse

    def run(n_items, unroll, base_of, g, dil, first):
        main = n_items // unroll

        def body(t, carry):
            blocks(g, dil, [base_of(t * unroll + u) for u in range(unroll)], first)
            return carry

        if main:
            lax.fori_loop(0, main, body, 0)
        if n_items > main * unroll:
            blocks(g, dil, [base_of(idx) for idx in range(main * unroll, n_items)], first)

    for g, (window, dil) in enumerate(DILATIONS):
        n_blk = seq // dil // BAND
        run(dil, _largest_divisor(dil, DIL_UNROLL_FIRST), lambda r: r, g, dil, True)
        if n_blk > 1:
            def rest_base(t, dil=dil, n_blk=n_blk):
                return t // (n_blk - 1) + dil * BAND * (t % (n_blk - 1) + 1)

            n_rest = dil * (n_blk - 1)
            run(n_rest, _largest_divisor(n_rest, DIL_UNROLL_REST), rest_base, g, dil, False)

    tile = 256

    def merge(t, carry):
        sl = pl.ds(pl.multiple_of(t * tile, tile), tile)
        lses = [lse_ref[g, sl, :] for g in range(len(DILATIONS))]
        mx = functools.reduce(jnp.maximum, lses)
        ws = [jnp.exp(l - mx) for l in lses]
        den = functools.reduce(jnp.add, ws)
        num = functools.reduce(jnp.add, [w * og_ref[g, sl, :] for g, w in enumerate(ws)])
        o_ref[sl, :] = (num / den * _silu(g_ref[sl, :])).astype(o_ref.dtype)
        return carry

    lax.fori_loop(0, seq // tile, merge, 0)


def _dil_prompt(z3, band_bias, n_heads):
    bsz, seq, _ = z3.shape
    n_pat = len(DILATIONS)
    for window, dil in DILATIONS:
        assert seq % (dil * BAND) == 0 and window // dil == BAND

    def col(off):
        return pl.BlockSpec((None, seq, HEAD_DIM), lambda b, h, off=off: (b, 0, off * n_heads + h))

    return pl.pallas_call(
        functools.partial(_dil_prompt_kernel, seq=seq),
        grid=(bsz, n_heads),
        in_specs=[col(0), col(1), col(2), col(3),
                  pl.BlockSpec((n_pat, None, BAND, 2 * BAND), lambda b, h: (0, h, 0, 0))],
        out_specs=pl.BlockSpec((None, seq, HEAD_DIM), lambda b, h: (b, 0, h)),
        out_shape=jax.ShapeDtypeStruct((bsz, seq, n_heads * HEAD_DIM), BF16),
        scratch_shapes=[pltpu.VMEM((n_pat, seq, HEAD_DIM), F32), pltpu.VMEM((n_pat, seq, HEAD_DIM), F32)],
        compiler_params=_params("parallel", "parallel"),
        name="dilated_attn_prompt",
    )(z3, z3, z3, z3, band_bias)


POOL_HALO = 16


def _pool_prompt_kernel(u_ref, halo_ref, gate_ref, pw_ref, ps_ref, o_ref, *, tile):
    t = pl.program_id(1)
    grp = pw_ref.shape[1]
    halo = jnp.where(t > 0, halo_ref[...], 0.0)
    pos = (t * tile + lax.broadcasted_iota(jnp.int32, (tile, 1), 0) + 1).astype(F32)
    for g, win in enumerate(POOL_WINDOWS):
        cols = slice(g * grp, (g + 1) * grp)
        x = u_ref[:, cols]
        ext = jnp.concatenate([halo[:, cols], x], axis=0)
        shift = 1
        while shift < win:
            ext = ext + pltpu.roll(ext, shift, 0)
            shift *= 2
        wsum = ext[POOL_HALO:, :]
        pooled = wsum / jnp.minimum(pos, float(win)) - x
        y = _dot(pooled.astype(BF16), pw_ref[g].astype(BF16)) * ps_ref[:, cols]
        o_ref[:, cols] = (y * _silu(gate_ref[:, cols])).astype(o_ref.dtype)


def _pool_prompt(z3, pool_w, pool_scale, layer, *, u_off, gate_off, tile=256):
    bsz, seq, _ = z3.shape
    _, n_grp, grp, _ = pool_w.shape
    width = n_grp * grp
    assert seq % tile == 0 and tile % POOL_HALO == 0
    for win in POOL_WINDOWS:
        assert win & (win - 1) == 0 and win - 1 <= POOL_HALO
    per = tile // POOL_HALO
    return pl.pallas_call(
        functools.partial(_pool_prompt_kernel, tile=tile),
        grid=(bsz, seq // tile),
        in_specs=[pl.BlockSpec((None, tile, width), lambda b, t: (b, t, u_off)),
                  pl.BlockSpec((None, POOL_HALO, width),
                               lambda b, t: (b, jnp.maximum(t * per - 1, 0), u_off)),
                  pl.BlockSpec((None, tile, width), lambda b, t: (b, t, gate_off)),
                  pl.BlockSpec((None, n_grp, grp, grp), lambda b, t: (layer, 0, 0, 0)),
                  pl.BlockSpec((None, 1, width), lambda b, t: (layer, 0, 0))],
        out_specs=pl.BlockSpec((None, tile, width), lambda b, t: (b, t, 0)),
        out_shape=jax.ShapeDtypeStruct((bsz, seq, width), BF16),
        compiler_params=_params("parallel", "parallel"),
        name="pool_mix_prompt",
    )(z3, z3, z3, pool_w, pool_scale.reshape(pool_scale.shape[0], 1, width))


SB_TILE = 256
SB_HEADS = 4


def _suffix_matrix(n):
    j = np.arange(n)[:, None]
    s = np.arange(n)[None, :]
    return (j >= s).astype(np.float32)


def _sb_prompt_kernel(q_ref, k_ref, v_ref, g_ref, bias_ref, uu_ref, o_ref, kb_ref, vb_ref,
                      za_ref, la_ref, zb_ref, lb_ref):
    i = pl.program_id(2)
    scale = 1.0 / math.sqrt(HEAD_DIM)

    @pl.when(i == 0)
    def _():
        kb_ref[...] = k_ref[...].astype(BF16)
        vb_ref[...] = v_ref[...].astype(BF16)

    row = lax.broadcasted_iota(jnp.int32, (SB_TILE, SB_TILE), 0)
    colid = lax.broadcasted_iota(jnp.int32, (SB_TILE, SB_TILE), 1)
    heads = [slice(h * HEAD_DIM, (h + 1) * HEAD_DIM) for h in range(SB_HEADS)]
    hs_all = range(SB_HEADS)
    qs = [(q_ref[:, heads[h]] * (scale * LOG2E)).astype(BF16) for h in hs_all]
    biases = [bias_ref[h][:, 0:1] * LOG2E for h in hs_all]

    slots = ((za_ref, la_ref), (zb_ref, lb_ref))

    def front(m, slot, diag):
        z_ref, l_ref = slots[slot]
        sl = pl.ds(pl.multiple_of((i - m) * SB_TILE, SB_TILE), SB_TILE)
        z2 = [_dot_nt(qs[h], kb_ref[sl, heads[h]]) + biases[h] for h in hs_all]
        for h in hs_all:
            sp2 = jnp.maximum(z2[h], 0.0) + jnp.log(1.0 + jnp.exp2(-jnp.abs(z2[h]))) * LOG2E
            if diag:
                sp2 = jnp.where(colid < row, sp2, 0.0)
                z_ref[h] = jnp.where(colid < row, z2[h], NEG)
            else:
                z_ref[h] = z2[h]
            hi, lo = _split_hi_lo(sp2)
            l_ref[h] = jnp.concatenate([hi, lo], axis=1)

    def back(m, slot, spent, acc):
        z_ref, l_ref = slots[slot]
        sl = pl.ds(pl.multiple_of((i - m) * SB_TILE, SB_TILE), SB_TILE)
        incl = [_dot(l_ref[h], uu_ref[...]) for h in hs_all]
        a = [jnp.exp2(z_ref[h] - incl[h] - spent[h]).astype(BF16) for h in hs_all]
        acc = [acc[h] + _dot(a[h], vb_ref[sl, heads[h]]) for h in hs_all]
        spent = [spent[h] + incl[h][:, 0:1] for h in hs_all]
        return spent, acc

    def finish(acc):
        for h in hs_all:
            o_ref[:, heads[h]] = (acc[h] * _silu(g_ref[:, heads[h]])).astype(o_ref.dtype)

    front(0, 0, True)

    def pair(n, st):
        spent, acc = list(st[:SB_HEADS]), list(st[SB_HEADS:])
        front(2 * n + 1, 1, False)
        spent, acc = back(2 * n, 0, spent, acc)
        front(2 * n + 2, 0, False)
        spent, acc = back(2 * n + 1, 1, spent, acc)
        return tuple(spent) + tuple(acc)

    zero = [jnp.zeros((SB_TILE, 1), F32) for _ in hs_all] + [jnp.zeros((SB_TILE, HEAD_DIM), F32) for _ in hs_all]
    state = lax.fori_loop(0, i // 2, pair, tuple(zero))
    spent, acc = list(state[:SB_HEADS]), list(state[SB_HEADS:])
    done = 2 * (i // 2)

    @pl.when(i % 2 == 0)
    def _():
        finish(back(done, 0, spent, acc)[1])

    @pl.when(i % 2 == 1)
    def _():
        front(done + 1, 1, False)
        sp1, acc1 = back(done, 0, spent, acc)
        finish(back(done + 1, 1, sp1, acc1)[1])


def _sb_prompt(z3, sb_bias, n_heads):
    bsz, seq, _ = z3.shape
    assert seq % SB_TILE == 0 and n_heads % SB_HEADS == 0
    n_grp = n_heads // SB_HEADS
    width = SB_HEADS * HEAD_DIM
    u = _suffix_matrix(SB_TILE)
    uu = jnp.asarray(np.concatenate([u, u], axis=0), BF16)
    bias = jnp.broadcast_to(sb_bias.astype(F32)[:, None, None], (n_heads, 1, LANES))

    def qcol(off):
        return pl.BlockSpec((None, SB_TILE, width), lambda b, h, i, off=off: (b, i, off * n_grp + h))

    def kcol(off):
        return pl.BlockSpec((None, seq, width), lambda b, h, i, off=off: (b, 0, off * n_grp + h))

    return pl.pallas_call(
        _sb_prompt_kernel,
        grid=(bsz, n_grp, seq // SB_TILE),
        in_specs=[qcol(0), kcol(1), kcol(2), qcol(3),
                  pl.BlockSpec((SB_HEADS, 1, LANES), lambda b, h, i: (h, 0, 0)),
                  pl.BlockSpec((2 * SB_TILE, SB_TILE), lambda b, h, i: (0, 0))],
        out_specs=pl.BlockSpec((None, SB_TILE, width), lambda b, h, i: (b, i, h)),
        out_shape=jax.ShapeDtypeStruct((bsz, seq, n_heads * HEAD_DIM), BF16),
        scratch_shapes=[pltpu.VMEM((seq, width), BF16), pltpu.VMEM((seq, width), BF16)]
        + [pltpu.VMEM((SB_HEADS, SB_TILE, SB_TILE), F32), pltpu.VMEM((SB_HEADS, SB_TILE, 2 * SB_TILE), BF16)] * 2,
        compiler_params=_params("parallel", "parallel", "arbitrary"),
        name="stick_breaking_prompt",
    )(z3, z3, z3, z3, bias, uu)


def _mlstm_prompt_kernel(q_ref, k_ref, v_ref, o_ref, g_ref, gc_ref, gr_ref, bc_ref, br_ref,
                         mix_ref, c_ref, n_ref, m_ref):
    chunk = pl.program_id(1)
    n_heads, dh, _ = c_ref.shape
    L = q_ref.shape[0]
    kscale = 1.0 / math.sqrt(dh)

    @pl.when(chunk == 0)
    def _():
        c_ref[...] = jnp.zeros_like(c_ref)
        n_ref[...] = jnp.zeros_like(n_ref)
        m_ref[...] = jnp.zeros_like(m_ref)

    t_idx = lax.broadcasted_iota(jnp.int32, (L, L), 0)
    s_idx = lax.broadcasted_iota(jnp.int32, (L, L), 1)
    causal = s_idx <= t_idx
    gates_c = gc_ref[...] + br_ref[...]
    gates_r = gr_ref[...] + bc_ref[...]
    for h in range(n_heads):
        cols = slice(h * dh, (h + 1) * dh)
        ii_c = gates_c[:, h:h + 1]
        lf_c = _log_sigmoid(gates_c[:, n_heads + h:n_heads + h + 1])
        ii_r = gates_r[h:h + 1, :]
        lf_r = _log_sigmoid(gates_r[n_heads + h:n_heads + h + 1, :])
        b_c = jnp.sum(jnp.where(causal, lf_r, 0.0), axis=1, keepdims=True)
        b_r = jnp.sum(jnp.where(t_idx <= s_idx, lf_c, 0.0), axis=0, keepdims=True)
        g_r = ii_r - b_r
        cm_c = jnp.max(jnp.where(causal, g_r, -jnp.inf), axis=1, keepdims=True)
        m0 = m_ref[h:h + 1, 0:1]
        m_t = b_c + jnp.maximum(m0, cm_c)
        inter = jnp.exp(b_c + m0 - m_t)
        dmat = jnp.where(causal, jnp.exp((b_c - m_t) + g_r), 0.0)
        q = q_ref[:, cols].astype(BF16)
        ks = k_ref[:, cols] * kscale
        v = v_ref[:, cols].astype(BF16)
        c0 = c_ref[h]
        n0 = n_ref[h:h + 1, :]
        sc = _dot_nt(q, ks.astype(BF16)) * dmat
        num = inter * _dot(q, c0.astype(BF16)) + _dot(sc.astype(BF16), v)
        qn = jnp.sum(q_ref[:, cols] * n0, axis=1, keepdims=True)
        den = inter * qn + jnp.sum(sc, axis=1, keepdims=True)
        hid = num / jnp.maximum(jnp.abs(den), jnp.exp(-m_t))
        m_last = m_t[L - 1:L, :]
        b_last = b_c[L - 1:L, :]
        decay = jnp.exp(b_last + m0 - m_last)
        wk = jnp.exp(b_last - b_c + ii_c - m_last)
        kw = ks * wk
        c_ref[h] = decay * c0 + _dot_tn(kw.astype(BF16), v)
        n_ref[h:h + 1, :] = decay * n0 + jnp.sum(kw, axis=0, keepdims=True)
        m_ref[h:h + 1, :] = jnp.broadcast_to(m_last, (1, m_ref.shape[1]))
        out = hid * jax.nn.sigmoid(o_ref[:, cols]) * _silu(g_ref[:, cols])
        mix_ref[:, cols] = out.astype(mix_ref.dtype)


def _mlstm_prompt(z3, zg3, gate_bias, *, col0, n_heads=N_HEADS_D, chunk=MLSTM_CHUNK):
    bsz, seq, _ = z3.shape
    width = (z3.shape[2] - col0) // 5
    dh = width // n_heads
    assert seq % chunk == 0 and col0 % width == 0 and zg3.shape[2] == 2 * n_heads == SUBLANES
    base = col0 // width
    zg_rows = jnp.transpose(zg3, (0, 2, 1))
    bias_lane = gate_bias.astype(F32).reshape(1, SUBLANES)
    bias_sub = bias_lane.reshape(SUBLANES, 1)

    def col(off):
        return pl.BlockSpec((None, chunk, width), lambda b, c, off=off: (b, c, base + off))

    mix, c_f, n_f, m_f = pl.pallas_call(
        _mlstm_prompt_kernel,
        grid=(bsz, seq // chunk),
        in_specs=[col(0), col(1), col(2), col(3), col(4),
                  pl.BlockSpec((None, chunk, SUBLANES), lambda b, c: (b, c, 0)),
                  pl.BlockSpec((None, SUBLANES, chunk), lambda b, c: (b, 0, c)),
                  pl.BlockSpec((SUBLANES, 1), lambda b, c: (0, 0)),
                  pl.BlockSpec((1, SUBLANES), lambda b, c: (0, 0))],
        out_specs=[pl.BlockSpec((None, chunk, width), lambda b, c: (b, c, 0)),
                   pl.BlockSpec((None, n_heads, dh, dh), lambda b, c: (b, 0, 0, 0)),
                   pl.BlockSpec((None, n_heads, dh), lambda b, c: (b, 0, 0)),
                   pl.BlockSpec((None, n_heads, LANES), lambda b, c: (b, 0, 0))],
        out_shape=[jax.ShapeDtypeStruct((bsz, seq, width), BF16),
                   jax.ShapeDtypeStruct((bsz, n_heads, dh, dh), F32),
                   jax.ShapeDtypeStruct((bsz, n_heads, dh), F32),
                   jax.ShapeDtypeStruct((bsz, n_heads, LANES), F32)],
        compiler_params=_params("parallel", "arbitrary"),
        name="mlstm_prompt",
    )(z3, z3, z3, z3, z3, zg3, zg_rows, bias_sub, bias_lane)
    return mix, c_f, n_f, m_f[:, :, 0]


def _group_tokens(n_heads):
    assert LANES % n_heads == 0
    return LANES // n_heads


def _stack_groups(zt, n_grp):
    return jnp.concatenate([zt[:, g * LANES:(g + 1) * LANES] for g in range(n_grp)], axis=0)


def _unstack_groups(a, n_grp, n_heads):
    return jnp.concatenate([a[g * n_heads:(g + 1) * n_heads, :] for g in range(n_grp)], axis=1)


def _own_head_mask(n_grp, n_heads):
    r = lax.broadcasted_iota(jnp.int32, (n_grp * n_heads, LANES), 0)
    c = lax.broadcasted_iota(jnp.int32, (n_grp * n_heads, LANES), 1)
    return (r % n_heads) == (c % n_heads)


def _dil_sample_kernel(*refs, n_pat):
    q_ref, kn_ref, vn_ref, g_ref, bh_ref, bn_ref = refs[:6]
    k_refs = refs[6:6 + n_pat]
    v_refs = refs[6 + n_pat:6 + 2 * n_pat]
    o_ref = refs[6 + 2 * n_pat]
    steps, n_heads, dh = k_refs[0].shape
    n_grp = steps * n_heads // LANES
    scale = 1.0 / math.sqrt(dh)
    q = q_ref[...] * scale
    qb = q.astype(BF16)
    zn = jnp.sum(q * kn_ref[...], axis=-1, keepdims=True)
    z_new = [zn + bn_ref[g][:, 0:1] for g in range(n_pat)]
    z_hist = []
    for g in range(n_pat):
        k2 = k_refs[g][...].reshape(steps * n_heads, dh).astype(BF16)
        z_hist.append(_stack_groups(_dot_nt(qb, k2), n_grp) + bh_ref[g])
    mx = functools.reduce(jnp.maximum, z_new)
    for z in z_hist:
        zr = jnp.max(z, axis=-1, keepdims=True)
        for gi in range(n_grp):
            mx = jnp.maximum(mx, zr[gi * n_heads:(gi + 1) * n_heads, :])
    mx_rows = jnp.concatenate([mx] * n_grp, axis=0)
    den = functools.reduce(jnp.add, [jnp.exp(z - mx) for z in z_new])
    w_new = den
    w_hist = [jnp.exp(z - mx_rows) for z in z_hist]
    for w in w_hist:
        wr = jnp.sum(w, axis=-1, keepdims=True)
        for gi in range(n_grp):
            den = den + wr[gi * n_heads:(gi + 1) * n_heads, :]
    inv = 1.0 / den
    inv_rows = jnp.concatenate([inv] * n_grp, axis=0)
    acc = (w_new * inv) * vn_ref[...]
    for g in range(n_pat):
        p = _unstack_groups((w_hist[g] * inv_rows).astype(BF16), n_grp, n_heads)
        v2 = v_refs[g][...].reshape(steps * n_heads, dh).astype(BF16)
        acc = acc + _dot(p, v2)
    o_ref[...] = (acc * _silu(g_ref[...])).astype(o_ref.dtype)


def _dil_sample(zs, buf_k, buf_v, layer, rel_bias, past_len):
    _, bsz, n_hist, n_heads, dh = buf_k.shape
    width = n_heads * dh
    n_pat = len(DILATIONS)
    steps = BAND
    tok = _group_tokens(n_heads)
    n_grp = steps // tok
    bias_hist, bias_new = _sample_bias(rel_bias, n_hist, past_len)
    bh = bias_hist.reshape(n_pat, n_grp, 1, tok, n_heads)
    own = jnp.asarray(np.eye(n_heads, dtype=bool)).reshape(1, 1, n_heads, 1, n_heads)
    bh = jnp.where(own, bh, NEG).reshape(n_pat, n_grp * n_heads, LANES)
    bn = jnp.broadcast_to(bias_new[:, :, None], (n_pat, n_heads, LANES))
    z4 = zs[:, :4 * width].reshape(bsz, 4, n_heads, dh)
    k_views, v_views, kv_specs = [], [], []
    for window, dil in DILATIONS:
        assert window // dil == steps and n_hist % window == 0 and window % dil == 0
        shape = (buf_k.shape[0], bsz, n_hist // dil, dil, n_heads, dh)
        k_views.append(buf_k.reshape(shape))
        v_views.append(buf_v.reshape(shape))
        blk = n_hist // window - 1
        kv_specs.append(pl.BlockSpec((None, None, steps, None, n_heads, dh),
                                     lambda b, blk=blk: (layer, b, blk, 0, 0, 0)))

    def zrow(off):
        return pl.BlockSpec((None, None, n_heads, dh), lambda b, off=off: (b, off, 0, 0))

    return pl.pallas_call(
        functools.partial(_dil_sample_kernel, n_pat=n_pat),
        grid=(bsz,),
        in_specs=[zrow(0), zrow(1), zrow(2), zrow(3),
                  pl.BlockSpec((n_pat, n_grp * n_heads, LANES), lambda b: (0, 0, 0)),
                  pl.BlockSpec((n_pat, n_heads, LANES), lambda b: (0, 0, 0))] + kv_specs + kv_specs,
        out_specs=pl.BlockSpec((None, n_heads, dh), lambda b: (b, 0, 0)),
        out_shape=jax.ShapeDtypeStruct((bsz, n_heads, dh), BF16),
        compiler_params=_params("parallel"),
        name="dilated_attn_sample",
    )(z4, z4, z4, z4, bh, bn, *k_views, *v_views).reshape(bsz, width)


def _pool_sample_kernel(u_ref, hist_ref, gate_ref, pw_ref, ps_ref, o_ref):
    grp = pw_ref.shape[1]
    n_hist = hist_ref.shape[0]
    for g, win in enumerate(POOL_WINDOWS):
        cols = slice(g * grp, (g + 1) * grp)
        x = u_ref[:, cols]
        n_old = min(win - 1, n_hist)
        wsum = x
        for r in range(n_hist - n_old, n_hist):
            wsum = wsum + hist_ref[r, :, cols]
        pooled = wsum / float(n_old + 1) - x
        y = _dot(pooled.astype(BF16), pw_ref[g].astype(BF16)) * ps_ref[:, cols]
        o_ref[:, cols] = (y * _silu(gate_ref[:, cols])).astype(o_ref.dtype)


def _pool_sample(zs, hist, pool_w, pool_scale, layer, *, u_off, gate_off):
    bsz = zs.shape[0]
    _, n_grp, grp, _ = pool_w.shape
    width = n_grp * grp
    n_hist = hist.shape[1]
    hist = jnp.transpose(hist, (1, 0, 2))
    return pl.pallas_call(
        _pool_sample_kernel,
        grid=(1,),
        in_specs=[pl.BlockSpec((bsz, width), lambda i: (0, u_off)),
                  pl.BlockSpec((n_hist, bsz, width), lambda i: (0, 0, 0)),
                  pl.BlockSpec((bsz, width), lambda i: (0, gate_off)),
                  pl.BlockSpec((None, n_grp, grp, grp), lambda i: (layer, 0, 0, 0)),
                  pl.BlockSpec((None, 1, width), lambda i: (layer, 0, 0))],
        out_specs=pl.BlockSpec((bsz, width), lambda i: (0, 0)),
        out_shape=jax.ShapeDtypeStruct((bsz, width), BF16),
        compiler_params=_params("arbitrary"),
        name="pool_mix_sample",
    )(zs, hist, zs, pool_w, pool_scale.reshape(pool_scale.shape[0], 1, width))


SB_SAMPLE_PAGES = 4


def _sb_sample_kernel(pt_ref, q_ref, g_ref, bias_ref, ut_ref, *refs):
    k_refs = refs[:SB_SAMPLE_PAGES]
    v_refs = refs[SB_SAMPLE_PAGES:2 * SB_SAMPLE_PAGES]
    o_ref, carry_sc, acc_sc = refs[2 * SB_SAMPLE_PAGES:]
    p = pl.program_id(1)
    page, n_heads, dh = k_refs[0].shape
    n_grp = page * n_heads // LANES
    scale = 1.0 / math.sqrt(dh)
    pages = range(SB_SAMPLE_PAGES)

    @pl.when(p == 0)
    def _():
        carry_sc[...] = jnp.zeros_like(carry_sc)
        acc_sc[...] = jnp.zeros_like(acc_sc)

    qb = (q_ref[...] * scale).astype(BF16)
    own = _own_head_mask(n_grp, n_heads)
    z = [_stack_groups(_dot_nt(qb, k_refs[u][...].reshape(page * n_heads, dh).astype(BF16)), n_grp)
         + bias_ref[...] for u in pages]
    lhs, log_take = [], []
    for u in pages:
        sp = _softplus(z[u])
        hi, lo = _split_hi_lo(jnp.where(own, -sp, 0.0))
        lhs.append(jnp.concatenate([hi, lo], axis=1))
        log_take.append(z[u] - sp)
    sums = [_dot(lhs[u], ut_ref[...]) for u in pages]
    after = carry_sc[...]
    offs = []
    for u in pages:
        total = sums[u][:, LANES:]
        offs_u = [None] * n_grp
        for gi in range(n_grp - 1, -1, -1):
            offs_u[gi] = after
            after = after + total[gi * n_heads:(gi + 1) * n_heads, :]
        offs.append(jnp.concatenate(offs_u, axis=0))
    carry_sc[...] = after
    acc = acc_sc[...]
    for u in pages:
        a = jnp.where(own, jnp.exp(log_take[u] + sums[u][:, :LANES] + offs[u]), 0.0)
        v2 = v_refs[u][...].reshape(page * n_heads, dh).astype(BF16)
        acc = acc + _dot(_unstack_groups(a.astype(BF16), n_grp, n_heads), v2)
    acc_sc[...] = acc

    @pl.when(p == pl.num_programs(1) - 1)
    def _():
        o_ref[...] = (acc_sc[...] * _silu(g_ref[...])).astype(o_ref.dtype)


def _sb_sample(zs, pool_k, pool_v, layer, page_table, sb_bias):
    bsz, n_pages = page_table.shape
    _, _, page, n_heads, dh = pool_k.shape
    width = n_heads * dh
    tok = _group_tokens(n_heads)
    n_grp = page // tok
    lane_tok = np.arange(LANES) // n_heads
    lane_head = np.arange(LANES) % n_heads
    later = (lane_head[:, None] == lane_head[None, :]) & (lane_tok[:, None] > lane_tok[None, :])
    half = np.concatenate([later.astype(np.float32), np.ones((LANES, LANES), np.float32)], axis=1)
    ut = jnp.asarray(np.concatenate([half, half], axis=0), BF16)
    z4 = zs[:, :4 * width].reshape(bsz, 4, n_heads, dh)
    bias = jnp.tile(jnp.broadcast_to(sb_bias.astype(F32)[:, None], (n_heads, LANES)), (n_grp, 1))

    def page_spec(u):
        return pl.BlockSpec((None, None, page, n_heads, dh),
                            lambda b, p, pt: (layer, pt[b, n_pages - 1 - (p * SB_SAMPLE_PAGES + u)], 0, 0, 0))

    assert n_pages % SB_SAMPLE_PAGES == 0
    page_specs = [page_spec(u) for u in range(SB_SAMPLE_PAGES)]
    return pl.pallas_call(
        _sb_sample_kernel,
        grid_spec=pltpu.PrefetchScalarGridSpec(
            num_scalar_prefetch=1,
            grid=(bsz, n_pages // SB_SAMPLE_PAGES),
            in_specs=[pl.BlockSpec((None, None, n_heads, dh), lambda b, p, pt: (b, 0, 0, 0)),
                      pl.BlockSpec((None, None, n_heads, dh), lambda b, p, pt: (b, 3, 0, 0)),
                      pl.BlockSpec((n_grp * n_heads, LANES), lambda b, p, pt: (0, 0)),
                      pl.BlockSpec((2 * LANES, 2 * LANES), lambda b, p, pt: (0, 0))] + page_specs + page_specs,
            out_specs=pl.BlockSpec((None, n_heads, dh), lambda b, p, pt: (b, 0, 0)),
            scratch_shapes=[pltpu.VMEM((n_heads, LANES), F32), pltpu.VMEM((n_heads, dh), F32)]),
        out_shape=jax.ShapeDtypeStruct((bsz, n_heads, dh), BF16),
        compiler_params=_params("parallel", "arbitrary"),
        name="stick_breaking_sample",
    )(page_table, z4, z4, bias, ut, *([pool_k] * SB_SAMPLE_PAGES), *([pool_v] * SB_SAMPLE_PAGES)).reshape(bsz, width)


def _mlstm_sample_kernel(q_ref, k_ref, v_ref, o_ref, g_ref, sc_ref, c0_ref, n0_ref,
                         mix_ref, c_ref, n_ref, m_ref):
    dh = c0_ref.shape[0]
    kscale = 1.0 / math.sqrt(dh)
    s = sc_ref[...]
    ii = s[:, 0:1] + s[:, 2:3]
    lf = _log_sigmoid(s[:, 1:2] + s[:, 3:4])
    m0 = s[:, 4:5]
    m_t = jnp.maximum(lf + m0, ii)
    inter = jnp.exp(lf + m0 - m_t)
    wk = jnp.exp(ii - m_t)
    q = q_ref[...]
    ks = k_ref[...] * kscale
    v = v_ref[...]
    c0 = c0_ref[...]
    n0 = n0_ref[...]
    sc = jnp.sum(q * ks, axis=0, keepdims=True) * wk
    num = inter * jnp.sum(c0 * q, axis=0, keepdims=True) + sc * v
    den = inter * jnp.sum(q * n0, axis=0, keepdims=True) + sc
    hid = num / jnp.maximum(jnp.abs(den), jnp.exp(-m_t))
    c_ref[...] = inter * c0 + (ks * wk) * v
    n_ref[...] = inter * n0 + ks * wk
    m_ref[...] = jnp.broadcast_to(m_t, m_ref.shape)
    mix_ref[...] = (hid * jax.nn.sigmoid(o_ref[...]) * _silu(g_ref[...])).astype(mix_ref.dtype)


def _mlstm_sample(zs, zg, gate_bias, c0, n0, m0, layer, *, col0, n_heads=N_HEADS_D):
    bsz = zs.shape[0]
    dh = c0.shape[-1]
    width = n_heads * dh
    grp = zs[:, col0:col0 + 5 * width].reshape(bsz, 5, n_heads, dh)
    q_col = grp[:, 0].reshape(bsz, n_heads, dh, 1)
    k_col = grp[:, 1].reshape(bsz, n_heads, dh, 1)
    v_row = grp[:, 2].reshape(bsz, n_heads, 1, dh)
    o_row = grp[:, 3].reshape(bsz, n_heads, 1, dh)
    g_row = grp[:, 4].reshape(bsz, n_heads, 1, dh)
    gb = gate_bias.astype(F32)
    scal = jnp.stack([zg[:, :n_heads], zg[:, n_heads:2 * n_heads],
                      jnp.broadcast_to(gb[0][None], (bsz, n_heads)),
                      jnp.broadcast_to(gb[1][None], (bsz, n_heads)),
                      m0.astype(F32)], axis=-1)
    scal = jnp.pad(scal, ((0, 0), (0, 0), (0, LANES - scal.shape[-1]))).reshape(bsz, n_heads, 1, LANES)

    def spec(r, c):
        return pl.BlockSpec((None, None, r, c), lambda b, h: (b, h, 0, 0))

    mix, c_n, n_n, m_n = pl.pallas_call(
        _mlstm_sample_kernel,
        grid=(bsz, n_heads),
        in_specs=[spec(dh, 1), spec(dh, 1), spec(1, dh), spec(1, dh), spec(1, dh), spec(1, LANES),
                  pl.BlockSpec((None, None, None, dh, dh), lambda b, h: (layer, b, h, 0, 0)),
                  spec(dh, 1)],
        out_specs=[spec(1, dh), spec(dh, dh), spec(dh, 1), spec(1, LANES)],
        out_shape=[jax.ShapeDtypeStruct((bsz, n_heads, 1, dh), BF16),
                   jax.ShapeDtypeStruct((bsz, n_heads, dh, dh), F32),
                   jax.ShapeDtypeStruct((bsz, n_heads, dh, 1), F32),
                   jax.ShapeDtypeStruct((bsz, n_heads, 1, LANES), F32)],
        compiler_params=_params("parallel", "parallel"),
        name="mlstm_sample",
    )(q_col, k_col, v_row, o_row, g_row, scal, c0, n0.astype(F32).reshape(bsz, n_heads, dh, 1))
    return mix.reshape(bsz, width), c_n, n_n.reshape(bsz, n_heads, dh), m_n[:, :, 0, 0]


PROJ_TM = 2048
PROJ_TN = 512


def kernel(x_prompt, x_sample, cache_a_k, cache_a_v, state_pool, cache_c_k, cache_c_v, state_mlstm_c,
           state_mlstm_n, state_mlstm_m, page_table, rel_bias, w_in_even, pool_w, pool_scale, w_out_even,
           w_in_odd, sb_bias, mlstm_gate_bias, w_out_odd, ln_g, ln_b):
    bsz, seq, d_model = x_prompt.shape
    dbs, dec_seq, _ = x_sample.shape
    assert dec_seq == 1
    depth = ln_g.shape[0]
    alpha = (2 * depth) ** 0.25
    past_len = page_table.shape[1] * PAGE_SIZE
    mix_a = d_model // 2
    n_heads = mix_a // HEAD_DIM
    mix_b = d_model - mix_a
    n_rows = bsz * seq
    even_cols = w_in_even.shape[2]
    odd_main = 9 * mix_a
    n_gate = w_in_odd.shape[2] - odd_main
    w_in_odd_nk = jnp.swapaxes(w_in_odd, 1, 2)

    xp_f = x_prompt.reshape(n_rows, d_model)
    xp_b = xp_f.astype(BF16)
    xs_f = x_sample.reshape(dbs, d_model)
    xs_b = xs_f.astype(BF16)
    band_bias = _band_bias(rel_bias)
    u_blk = 4 * mix_a // mix_b

    zp_even, zp_odd, zs_even, zs_odd = [], [], [], []
    mcp, mnp, mmp, mcs, mns, mms = [], [], [], [], [], []
    for layer in range(depth):
        j = layer // 2
        if layer % 2 == 0:
            z, zs = _proj(xp_b, xs_b, w_in_even, j, even_cols, w_is_nk=False, tm=PROJ_TM, tn=PROJ_TN)
            z3 = z.reshape(bsz, seq, -1)
            oa = _dil_prompt(z3, band_bias, n_heads)
            ob = _pool_prompt(z3, pool_w, pool_scale, j, u_off=u_blk, gate_off=u_blk + 1)
            oas = _dil_sample(zs, cache_a_k, cache_a_v, j, rel_bias, past_len)
            obs = _pool_sample(zs, state_pool[j], pool_w, pool_scale, j, u_off=u_blk, gate_off=u_blk + 1)
            yp, ys = _out_proj(oa.reshape(n_rows, mix_a), ob.reshape(n_rows, mix_b), oas, obs, w_out_even, j,
                               tm=PROJ_TM, tn=PROJ_TN)
            zp_even.append(z3)
            zs_even.append(zs)
        else:
            z, zs = _proj(xp_b, xs_b, w_in_odd_nk, j, odd_main, w_is_nk=True, tm=PROJ_TM, tn=PROJ_TN)
            zg, zgs = _gate_proj(xp_b, xs_b, w_in_odd_nk, j, odd_main, n_gate, tm=PROJ_TM)
            z3 = z.reshape(bsz, seq, -1)
            oc = _sb_prompt(z3, sb_bias[j], n_heads)
            od, c_f, n_f, m_f = _mlstm_prompt(z3, zg.reshape(bsz, seq, n_gate), mlstm_gate_bias[j], col0=4 * mix_a)
            ocs = _sb_sample(zs, cache_c_k, cache_c_v, j, page_table, sb_bias[j])
            ods, c_n, n_n, m_n = _mlstm_sample(zs, zgs, mlstm_gate_bias[j], state_mlstm_c, state_mlstm_n[j],
                                               state_mlstm_m[j], j, col0=4 * mix_a)
            yp, ys = _out_proj(oc.reshape(n_rows, mix_a), od.reshape(n_rows, mix_b), ocs, ods, w_out_odd, j,
                               tm=PROJ_TM, tn=PROJ_TN)
            zp_odd.append(z3)
            zs_odd.append(zs)
            mcp.append(c_f)
            mnp.append(n_f)
            mmp.append(m_f)
            mcs.append(c_n)
            mns.append(n_n)
            mms.append(m_n)
        xp_f, xp_b = _residual_ln(xp_f, yp, ln_g[layer], ln_b[layer], alpha=alpha, tm=256)
        xs_f, xs_b = _residual_ln(xs_f, ys, ln_g[layer], ln_b[layer], alpha=alpha, tm=dbs)

    keep_a = min(A_WINDOW, seq)
    keep_u = min(POOL_HIST, seq)

    def heads(a):
        return a.reshape(a.shape[:-1] + (n_heads, HEAD_DIM))

    def pick(zs, rows, lo, hi):
        return jnp.stack([z[..., rows, lo:hi] for z in zs])

    every = slice(None)
    new_row = np.newaxis
    a_k_prompt = heads(pick(zp_even, slice(seq - keep_a, seq), mix_a, 2 * mix_a))
    a_v_prompt = heads(pick(zp_even, slice(seq - keep_a, seq), 2 * mix_a, 3 * mix_a))
    pool_prompt = pick(zp_even, slice(seq - keep_u, seq), 4 * mix_a, 4 * mix_a + mix_b)
    a_k_sample, a_v_sample = _shift_windows(
        cache_a_k, cache_a_v, heads(pick(zs_even, every, mix_a, 2 * mix_a))[:, :, new_row],
        heads(pick(zs_even, every, 2 * mix_a, 3 * mix_a))[:, :, new_row])
    pool_sample = jnp.concatenate([state_pool[:, :, 1:],
                                   pick(zs_even, every, 4 * mix_a, 4 * mix_a + mix_b)[:, :, new_row]], axis=2)
    c_k_prompt = heads(pick(zp_odd, every, mix_a, 2 * mix_a))
    c_v_prompt = heads(pick(zp_odd, every, 2 * mix_a, 3 * mix_a))
    c_k_sample = heads(pick(zs_odd, every, mix_a, 2 * mix_a))[:, :, new_row]
    c_v_sample = heads(pick(zs_odd, every, 2 * mix_a, 3 * mix_a))[:, :, new_row]
    return (xp_f.reshape(bsz, seq, d_model), xs_f.reshape(dbs, 1, d_model),
            a_k_prompt, a_v_prompt, a_k_sample, a_v_sample, pool_prompt, pool_sample,
            c_k_prompt, c_v_prompt, c_k_sample, c_v_sample,
            jnp.stack(mcp), jnp.stack(mnp), jnp.stack(mmp), jnp.stack(mcs), jnp.stack(mns), jnp.stack(mms))
```

```python
import functools
import math

import numpy as np
import jax
import jax.numpy as jnp
from jax import lax
from jax.experimental import pallas as pl
from jax.experimental.pallas import tpu as pltpu

HEAD_DIM = 128
DILATIONS = ((128, 1), (512, 4), (2048, 16))
A_WINDOW = max(w for w, _ in DILATIONS)
BAND = 128
POOL_WINDOWS = (2, 4, 8, 16)
POOL_HIST = max(POOL_WINDOWS) - 1
N_HEADS_D = 4
MLSTM_CHUNK = 128
N_REL_BUCKETS = 32
REL_MAX_DISTANCE = A_WINDOW
LN_EPS = 1e-5
LOG2E = 1.0 / math.log(2.0)
PAGE_SIZE = 128
NEG = -1e30

VMEM_LIMIT = 56 * 1024 * 1024
LANES = 128
SUBLANES = 8

BF16 = jnp.bfloat16
F32 = jnp.float32


def _params(*sem):
    return pltpu.CompilerParams(dimension_semantics=sem, vmem_limit_bytes=VMEM_LIMIT)


def _silu(x):
    return x * jax.nn.sigmoid(x)


def _dot(a, b):
    return jnp.dot(a, b, preferred_element_type=F32)


def _dot_nt(a, b):
    return lax.dot_general(a, b, (((1,), (1,)), ((), ())), preferred_element_type=F32)


def _dot_tn(a, b):
    return lax.dot_general(a, b, (((0,), (0,)), ((), ())), preferred_element_type=F32)


def _split_hi_lo(x):
    hi = x.astype(BF16)
    lo = (x - hi.astype(F32)).astype(BF16)
    return hi, lo


def _softplus(z):
    return jnp.maximum(z, 0.0) + jnp.log(1.0 + jnp.exp(-jnp.abs(z)))


def _log_sigmoid(z):
    return -_softplus(-z)


def _proj_kernel(x_ref, xs_ref, w_ref, o_ref, os_ref, *, w_is_nk):
    wb = w_ref[...].astype(BF16)
    mm = _dot_nt if w_is_nk else _dot
    o_ref[...] = mm(x_ref[...], wb)
    os_ref[...] = mm(xs_ref[...], wb)


def _proj(x, xs, w, layer, col0, n_cols, *, w_is_nk, tm, tn):
    m, k = x.shape
    ms = xs.shape[0]
    assert m % tm == 0 and n_cols % tn == 0 and col0 % tn == 0 and w.shape[2 if w_is_nk else 1] == k
    j0 = col0 // tn
    if w_is_nk:
        w_spec = pl.BlockSpec((None, tn, k), lambda i, j: (layer, j0 + j, 0))
    else:
        w_spec = pl.BlockSpec((None, k, tn), lambda i, j: (layer, 0, j0 + j))
    o, os = pl.pallas_call(
        functools.partial(_proj_kernel, w_is_nk=w_is_nk),
        grid=(m // tm, n_cols // tn),
        in_specs=[pl.BlockSpec((tm, k), lambda i, j: (i, 0), pipeline_mode=pl.Buffered(1)),
                  pl.BlockSpec((ms, k), lambda i, j: (0, 0)),
                  w_spec],
        out_specs=[pl.BlockSpec((tm, tn), lambda i, j: (i, j)),
                   pl.BlockSpec((None, ms, tn), lambda i, j: (i, 0, j))],
        out_shape=[jax.ShapeDtypeStruct((m, n_cols), F32), jax.ShapeDtypeStruct((m // tm, ms, n_cols), F32)],
        compiler_params=_params("arbitrary", "arbitrary"),
        name="proj",
    )(x, xs, w)
    return o, os[0]


def _gate_proj_kernel(x_ref, xs_ref, w_ref, o_ref, os_ref):
    wb = w_ref[...].astype(BF16)
    o_ref[...] = _dot_nt(x_ref[...], wb)
    os_ref[...] = _dot_nt(xs_ref[...], wb)


def _gate_proj(x, xs, w_nk, layer, row0, n_gate, *, tm):
    m, k = x.shape
    ms = xs.shape[0]
    assert m % tm == 0 and row0 % n_gate == 0 and n_gate % SUBLANES == 0
    o, os = pl.pallas_call(
        _gate_proj_kernel,
        grid=(m // tm,),
        in_specs=[pl.BlockSpec((tm, k), lambda i: (i, 0)),
                  pl.BlockSpec((ms, k), lambda i: (0, 0)),
                  pl.BlockSpec((None, n_gate, k), lambda i: (layer, row0 // n_gate, 0))],
        out_specs=[pl.BlockSpec((tm, n_gate), lambda i: (i, 0)),
                   pl.BlockSpec((None, ms, n_gate), lambda i: (i, 0, 0))],
        out_shape=[jax.ShapeDtypeStruct((m, n_gate), F32), jax.ShapeDtypeStruct((m // tm, ms, n_gate), F32)],
        compiler_params=_params("arbitrary"),
        name="gate_proj",
    )(x, xs, w_nk)
    return o, os[0]


def _out_proj_kernel(xa_ref, xb_ref, sa_ref, sb_ref, w_ref, o_ref, os_ref):
    ka = xa_ref.shape[1]
    wa = w_ref[:ka, :].astype(BF16)
    wb = w_ref[ka:, :].astype(BF16)
    o_ref[...] = _dot(xa_ref[...], wa) + _dot(xb_ref[...], wb)
    os_ref[...] = _dot(sa_ref[...], wa) + _dot(sb_ref[...], wb)


def _out_proj(xa, xb, sa, sb, w, layer, *, tm, tn):
    m, ka = xa.shape
    kb = xb.shape[1]
    ms = sa.shape[0]
    n = w.shape[2]
    assert m % tm == 0 and n % tn == 0 and w.shape[1] == ka + kb
    o, os = pl.pallas_call(
        _out_proj_kernel,
        grid=(m // tm, n // tn),
        in_specs=[pl.BlockSpec((tm, ka), lambda i, j: (i, 0), pipeline_mode=pl.Buffered(1)),
                  pl.BlockSpec((tm, kb), lambda i, j: (i, 0), pipeline_mode=pl.Buffered(1)),
                  pl.BlockSpec((ms, ka), lambda i, j: (0, 0)),
                  pl.BlockSpec((ms, kb), lambda i, j: (0, 0)),
                  pl.BlockSpec((None, ka + kb, tn), lambda i, j: (layer, 0, j))],
        out_specs=[pl.BlockSpec((tm, tn), lambda i, j: (i, j)),
                   pl.BlockSpec((None, ms, tn), lambda i, j: (i, 0, j))],
        out_shape=[jax.ShapeDtypeStruct((m, n), F32), jax.ShapeDtypeStruct((m // tm, ms, n), F32)],
        compiler_params=_params("arbitrary", "arbitrary"),
        name="out_proj",
    )(xa, xb, sa, sb, w)
    return o, os[0]


SHIFT_ROWS = 256


def _shift_kernel(ck_ref, cv_ref, nk_ref, nv_ref, ok_ref, ov_ref, *, per_window):
    c = pl.program_id(0)
    rows = ok_ref.shape[0]
    last = c == pl.num_programs(0) - 1
    for src, new, dst in ((ck_ref, nk_ref, ok_ref), (cv_ref, nv_ref, ov_ref)):
        @pl.when(jnp.logical_not(last))
        def _(src=src, dst=dst):
            dst[...] = src[...]

        @pl.when(last)
        def _(src=src, dst=dst):
            dst[0:rows - 1] = src[1:rows]

        @pl.when(c % per_window == per_window - 1)
        def _(new=new, dst=dst):
            dst[rows - 1:rows] = new[...]


def _shift_windows(cache_k, cache_v, new_k, new_v):
    n_layers, bsz, n_hist, n_heads, dh = cache_k.shape
    total = n_layers * bsz * n_hist
    assert n_hist % SHIFT_ROWS == 0
    per_window = n_hist // SHIFT_ROWS
    flat = (total, n_heads, dh)
    src_spec = pl.BlockSpec((pl.Element(SHIFT_ROWS), pl.Element(n_heads), pl.Element(dh)),
                            lambda c: (jnp.minimum(c * SHIFT_ROWS + 1, total - SHIFT_ROWS), 0, 0))
    new_spec = pl.BlockSpec((None, 1, n_heads, dh), lambda c: (c // per_window, 0, 0, 0))
    dst_spec = pl.BlockSpec((SHIFT_ROWS, n_heads, dh), lambda c: (c, 0, 0))
    ok, ov = pl.pallas_call(
        functools.partial(_shift_kernel, per_window=per_window),
        grid=(total // SHIFT_ROWS,),
        in_specs=[src_spec, src_spec, new_spec, new_spec],
        out_specs=[dst_spec, dst_spec],
        out_shape=[jax.ShapeDtypeStruct(flat, cache_k.dtype), jax.ShapeDtypeStruct(flat, cache_v.dtype)],
        compiler_params=_params("arbitrary"),
        name="shift_windows",
    )(cache_k.reshape(flat), cache_v.reshape(flat),
      new_k.astype(cache_k.dtype).reshape(n_layers * bsz, 1, n_heads, dh),
      new_v.astype(cache_v.dtype).reshape(n_layers * bsz, 1, n_heads, dh))
    return ok.reshape(cache_k.shape), ov.reshape(cache_v.shape)


def _ln_kernel(x_ref, y_ref, g_ref, b_ref, of_ref, ob_ref, *, alpha):
    h = alpha * x_ref[...] + y_ref[...]
    mu = jnp.mean(h, axis=-1, keepdims=True)
    c = h - mu
    var = jnp.mean(c * c, axis=-1, keepdims=True)
    r = c * lax.rsqrt(var + LN_EPS) * g_ref[...] + b_ref[...]
    of_ref[...] = r
    ob_ref[...] = r.astype(BF16)


def _residual_ln(x, y, g, b, *, alpha, tm):
    m, d = x.shape
    tm = min(tm, m)
    assert m % tm == 0
    row = pl.BlockSpec((tm, d), lambda i: (i, 0))
    vec = pl.BlockSpec((1, d), lambda i: (0, 0))
    return pl.pallas_call(
        functools.partial(_ln_kernel, alpha=alpha),
        grid=(m // tm,),
        in_specs=[row, row, vec, vec],
        out_specs=[row, row],
        out_shape=[jax.ShapeDtypeStruct((m, d), F32), jax.ShapeDtypeStruct((m, d), BF16)],
        compiler_params=_params("parallel"),
        name="residual_ln",
    )(x, y, g.reshape(1, d), b.reshape(1, d))


def _rel_bucket_static(dist):
    exact = N_REL_BUCKETS // 2
    dist = np.asarray(dist, np.int64)
    ratio = np.log(np.maximum(dist, exact) / exact) / math.log(REL_MAX_DISTANCE / exact)
    large = np.minimum(exact + (ratio * (N_REL_BUCKETS - exact)).astype(np.int64), N_REL_BUCKETS - 1)
    return np.where(dist < exact, dist, large).astype(np.int32)


def _band_bias(rel_bias):
    n_heads = rel_bias.shape[1]
    period = 2 * BAND + 1
    out = []
    for window, dil in DILATIONS:
        steps = window // dil
        m = BAND - np.arange(period)
        valid = (m >= 0) & (m <= steps)
        tab = rel_bias[jnp.asarray(_rel_bucket_static(dil * np.clip(m, 0, steps)))].astype(F32)
        tab = jnp.where(jnp.asarray(valid)[:, None], tab, NEG).T
        rows = jnp.tile(tab, (1, BAND))[:, :BAND * 2 * BAND]
        out.append(rows.reshape(n_heads, BAND, 2 * BAND))
    return jnp.stack(out)


def _sample_bias(rel_bias, n_hist, past_len):
    hist, new = [], []
    for window, dil in DILATIONS:
        steps = window // dil
        m = steps - np.arange(steps)
        idx = n_hist - dil * m
        valid = (past_len - dil * m >= 0) & (idx >= 0)
        bias = rel_bias[jnp.asarray(_rel_bucket_static(dil * m))].astype(F32)
        hist.append(jnp.where(jnp.asarray(valid)[:, None], bias, NEG))
        new.append(rel_bias[int(_rel_bucket_static(0))].astype(F32))
    return jnp.stack(hist), jnp.stack(new)


DIL_UNROLL_FIRST = 8
DIL_UNROLL_REST = 5


def _largest_divisor(n, cap):
    return max(d for d in range(1, cap + 1) if n % d == 0)


def _dil_prompt_kernel(q_ref, k_ref, v_ref, g_ref, bm_ref, o_ref, og_ref, lse_ref, *, seq):
    scale = 1.0 / math.sqrt(HEAD_DIM)

    def rows(ref, start, n, dil):
        if dil == 1:
            return ref[pl.ds(start, n), :]
        return ref[pl.ds(start, n, stride=dil), :]

    def blocks(g, dil, bases, first):
        n = range(len(bases))
        back = 0 if first else dil * BAND
        n_keys = BAND if first else 2 * BAND
        bias = bm_ref[g, :, BAND:] if first else bm_ref[g]
        qs = [(rows(q_ref, b, BAND, dil) * scale).astype(BF16) for b in bases]
        ks = [rows(k_ref, b - back, n_keys, dil).astype(BF16) for b in bases]
        s = [_dot_nt(qs[u], ks[u]) + bias for u in n]
        mx = [jnp.max(s[u], axis=-1, keepdims=True) for u in n]
        p = [jnp.exp(s[u] - mx[u]) for u in n]
        l = [jnp.sum(p[u], axis=-1, keepdims=True) for u in n]
        vs = [rows(v_ref, b - back, n_keys, dil).astype(BF16) for b in bases]
        o = [_dot(p[u].astype(BF16), vs[u]) / l[u] for u in n]
        for u, b in enumerate(bases):
            lse = jnp.broadcast_to(mx[u] + jnp.log(l[u]), (BAND, HEAD_DIM))
            if dil == 1:
                og_ref[g, pl.ds(b, BAND), :] = o[u]
                lse_ref[g, pl.ds(b, BAND), :] = lse
            else:
                og_ref[g, pl.ds(b, BAND, stride=dil), :] = o[u]
                lse_ref[g, pl.ds(b, BAND, stride=dil), :] = lse

    def run(n_items, unroll, base_of, g, dil, first):
        main = n_items // unroll

        def body(t, carry):
            blocks(g, dil, [base_of(t * unroll + u) for u in range(unroll)], first)
            return carry

        if main:
            lax.fori_loop(0, main, body, 0)
        if n_items > main * unroll:
            blocks(g, dil, [base_of(idx) for idx in range(main * unroll, n_items)], first)

    for g, (window, dil) in enumerate(DILATIONS):
        n_blk = seq // dil // BAND
        run(dil, _largest_divisor(dil, DIL_UNROLL_FIRST), lambda r: r, g, dil, True)
        if n_blk > 1:
            def rest_base(t, dil=dil, n_blk=n_blk):
                return t // (n_blk - 1) + dil * BAND * (t % (n_blk - 1) + 1)

            n_rest = dil * (n_blk - 1)
            run(n_rest, _largest_divisor(n_rest, DIL_UNROLL_REST), rest_base, g, dil, False)

    tile = 256

    def merge(t, carry):
        sl = pl.ds(pl.multiple_of(t * tile, tile), tile)
        lses = [lse_ref[g, sl, :] for g in range(len(DILATIONS))]
        mx = functools.reduce(jnp.maximum, lses)
        ws = [jnp.exp(l - mx) for l in lses]
        den = functools.reduce(jnp.add, ws)
        num = functools.reduce(jnp.add, [w * og_ref[g, sl, :] for g, w in enumerate(ws)])
        o_ref[sl, :] = (num / den * _silu(g_ref[sl, :])).astype(o_ref.dtype)
        return carry

    lax.fori_loop(0, seq // tile, merge, 0)


def _dil_prompt(q3, k3, v3, g3, band_bias, n_heads):
    bsz, seq, _ = q3.shape
    n_pat = len(DILATIONS)
    for window, dil in DILATIONS:
        assert seq % (dil * BAND) == 0 and window // dil == BAND
    col = pl.BlockSpec((None, seq, HEAD_DIM), lambda b, h: (b, 0, h))
    return pl.pallas_call(
        functools.partial(_dil_prompt_kernel, seq=seq),
        grid=(bsz, n_heads),
        in_specs=[col, col, col, col,
                  pl.BlockSpec((n_pat, None, BAND, 2 * BAND), lambda b, h: (0, h, 0, 0))],
        out_specs=col,
        out_shape=jax.ShapeDtypeStruct((bsz, seq, n_heads * HEAD_DIM), BF16),
        scratch_shapes=[pltpu.VMEM((n_pat, seq, HEAD_DIM), F32), pltpu.VMEM((n_pat, seq, HEAD_DIM), F32)],
        compiler_params=_params("parallel", "parallel"),
        name="dilated_attn_prompt",
    )(q3, k3, v3, g3, band_bias)


POOL_HALO = 16


def _pool_prompt_kernel(u_ref, halo_ref, gate_ref, pw_ref, ps_ref, o_ref, *, tile):
    t = pl.program_id(1)
    grp = pw_ref.shape[1]
    halo = jnp.where(t > 0, halo_ref[...], 0.0)
    pos = (t * tile + lax.broadcasted_iota(jnp.int32, (tile, 1), 0) + 1).astype(F32)
    for g, win in enumerate(POOL_WINDOWS):
        cols = slice(g * grp, (g + 1) * grp)
        x = u_ref[:, cols]
        ext = jnp.concatenate([halo[:, cols], x], axis=0)
        shift = 1
        while shift < win:
            ext = ext + pltpu.roll(ext, shift, 0)
            shift *= 2
        wsum = ext[POOL_HALO:, :]
        pooled = wsum / jnp.minimum(pos, float(win)) - x
        y = _dot(pooled.astype(BF16), pw_ref[g].astype(BF16)) * ps_ref[:, cols]
        o_ref[:, cols] = (y * _silu(gate_ref[:, cols])).astype(o_ref.dtype)


def _pool_prompt(z3, pool_w, pool_scale, layer, *, u_off, gate_off, tile=256):
    bsz, seq, _ = z3.shape
    _, n_grp, grp, _ = pool_w.shape
    width = n_grp * grp
    assert seq % tile == 0 and tile % POOL_HALO == 0
    for win in POOL_WINDOWS:
        assert win & (win - 1) == 0 and win - 1 <= POOL_HALO
    per = tile // POOL_HALO
    return pl.pallas_call(
        functools.partial(_pool_prompt_kernel, tile=tile),
        grid=(bsz, seq // tile),
        in_specs=[pl.BlockSpec((None, tile, width), lambda b, t: (b, t, u_off)),
                  pl.BlockSpec((None, POOL_HALO, width),
                               lambda b, t: (b, jnp.maximum(t * per - 1, 0), u_off)),
                  pl.BlockSpec((None, tile, width), lambda b, t: (b, t, gate_off)),
                  pl.BlockSpec((None, n_grp, grp, grp), lambda b, t: (layer, 0, 0, 0)),
                  pl.BlockSpec((None, 1, width), lambda b, t: (layer, 0, 0))],
        out_specs=pl.BlockSpec((None, tile, width), lambda b, t: (b, t, 0)),
        out_shape=jax.ShapeDtypeStruct((bsz, seq, width), BF16),
        compiler_params=_params("parallel", "parallel"),
        name="pool_mix_prompt",
    )(z3, z3, z3, pool_w, pool_scale.reshape(pool_scale.shape[0], 1, width))


SB_TILE = 256
SB_HEADS = 4


def _suffix_matrix(n):
    j = np.arange(n)[:, None]
    s = np.arange(n)[None, :]
    return (j >= s).astype(np.float32)


def _sb_prompt_kernel(q_ref, k_ref, v_ref, g_ref, bias_ref, uu_ref, o_ref, kb_ref, vb_ref,
                      za_ref, la_ref, zb_ref, lb_ref):
    i = pl.program_id(2)
    scale = 1.0 / math.sqrt(HEAD_DIM)

    @pl.when(i == 0)
    def _():
        kb_ref[...] = k_ref[...].astype(BF16)
        vb_ref[...] = v_ref[...].astype(BF16)

    row = lax.broadcasted_iota(jnp.int32, (SB_TILE, SB_TILE), 0)
    colid = lax.broadcasted_iota(jnp.int32, (SB_TILE, SB_TILE), 1)
    heads = [slice(h * HEAD_DIM, (h + 1) * HEAD_DIM) for h in range(SB_HEADS)]
    hs_all = range(SB_HEADS)
    qs = [(q_ref[:, heads[h]] * (scale * LOG2E)).astype(BF16) for h in hs_all]
    biases = [bias_ref[h][:, 0:1] * LOG2E for h in hs_all]

    slots = ((za_ref, la_ref), (zb_ref, lb_ref))

    def front(m, slot, diag):
        z_ref, l_ref = slots[slot]
        sl = pl.ds(pl.multiple_of((i - m) * SB_TILE, SB_TILE), SB_TILE)
        z2 = [_dot_nt(qs[h], kb_ref[sl, heads[h]]) + biases[h] for h in hs_all]
        for h in hs_all:
            sp2 = jnp.maximum(z2[h], 0.0) + jnp.log(1.0 + jnp.exp2(-jnp.abs(z2[h]))) * LOG2E
            if diag:
                sp2 = jnp.where(colid < row, sp2, 0.0)
                z_ref[h] = jnp.where(colid < row, z2[h], NEG)
            else:
                z_ref[h] = z2[h]
            hi, lo = _split_hi_lo(sp2)
            l_ref[h] = jnp.concatenate([hi, lo], axis=1)

    def back(m, slot, spent, acc):
        z_ref, l_ref = slots[slot]
        sl = pl.ds(pl.multiple_of((i - m) * SB_TILE, SB_TILE), SB_TILE)
        incl_all = _dot(l_ref[...].reshape(SB_HEADS * SB_TILE, 2 * SB_TILE), uu_ref[...])
        incl = [incl_all[h * SB_TILE:(h + 1) * SB_TILE] for h in hs_all]
        a = [jnp.exp2(z_ref[h] - incl[h] - spent[h]).astype(BF16) for h in hs_all]
        acc = [acc[h] + _dot(a[h], vb_ref[sl, heads[h]]) for h in hs_all]
        spent = [spent[h] + incl[h][:, 0:1] for h in hs_all]
        return spent, acc

    def finish(acc):
        for h in hs_all:
            o_ref[:, heads[h]] = (acc[h] * _silu(g_ref[:, heads[h]])).astype(o_ref.dtype)

    front(0, 0, True)

    def pair(n, st):
        spent, acc = list(st[:SB_HEADS]), list(st[SB_HEADS:])
        front(2 * n + 1, 1, False)
        spent, acc = back(2 * n, 0, spent, acc)
        front(2 * n + 2, 0, False)
        spent, acc = back(2 * n + 1, 1, spent, acc)
        return tuple(spent) + tuple(acc)

    zero = [jnp.zeros((SB_TILE, 1), F32) for _ in hs_all] + [jnp.zeros((SB_TILE, HEAD_DIM), F32) for _ in hs_all]
    state = lax.fori_loop(0, i // 2, pair, tuple(zero))
    spent, acc = list(state[:SB_HEADS]), list(state[SB_HEADS:])
    done = 2 * (i // 2)

    @pl.when(i % 2 == 0)
    def _():
        finish(back(done, 0, spent, acc)[1])

    @pl.when(i % 2 == 1)
    def _():
        front(done + 1, 1, False)
        sp1, acc1 = back(done, 0, spent, acc)
        finish(back(done + 1, 1, sp1, acc1)[1])


def _sb_prompt(q3, k3, v3, g3, sb_bias, n_heads):
    bsz, seq, _ = q3.shape
    assert seq % SB_TILE == 0 and n_heads % SB_HEADS == 0
    n_grp = n_heads // SB_HEADS
    width = SB_HEADS * HEAD_DIM
    u = _suffix_matrix(SB_TILE)
    uu = jnp.asarray(np.concatenate([u, u], axis=0), BF16)
    bias = jnp.broadcast_to(sb_bias.astype(F32)[:, None, None], (n_heads, 1, LANES))
    qcol = pl.BlockSpec((None, SB_TILE, width), lambda b, h, i: (b, i, h))
    kcol = pl.BlockSpec((None, seq, width), lambda b, h, i: (b, 0, h))
    return pl.pallas_call(
        _sb_prompt_kernel,
        grid=(bsz, n_grp, seq // SB_TILE),
        in_specs=[qcol, kcol, kcol, qcol,
                  pl.BlockSpec((SB_HEADS, 1, LANES), lambda b, h, i: (h, 0, 0)),
                  pl.BlockSpec((2 * SB_TILE, SB_TILE), lambda b, h, i: (0, 0))],
        out_specs=pl.BlockSpec((None, SB_TILE, width), lambda b, h, i: (b, i, h)),
        out_shape=jax.ShapeDtypeStruct((bsz, seq, n_heads * HEAD_DIM), BF16),
        scratch_shapes=[pltpu.VMEM((seq, width), BF16), pltpu.VMEM((seq, width), BF16)]
        + [pltpu.VMEM((SB_HEADS, SB_TILE, SB_TILE), F32), pltpu.VMEM((SB_HEADS, SB_TILE, 2 * SB_TILE), BF16)] * 2,
        compiler_params=_params("parallel", "parallel", "arbitrary"),
        name="stick_breaking_prompt",
    )(q3, k3, v3, g3, bias, uu)


def _mlstm_prompt_kernel(q_ref, k_ref, v_ref, o_ref, g_ref, gc_ref, gr_ref, bc_ref, br_ref,
                         mix_ref, c_ref, n_ref, m_ref):
    chunk = pl.program_id(1)
    n_heads, dh, _ = c_ref.shape
    L = q_ref.shape[0]
    kscale = 1.0 / math.sqrt(dh)

    @pl.when(chunk == 0)
    def _():
        c_ref[...] = jnp.zeros_like(c_ref)
        n_ref[...] = jnp.zeros_like(n_ref)
        m_ref[...] = jnp.zeros_like(m_ref)

    t_idx = lax.broadcasted_iota(jnp.int32, (L, L), 0)
    s_idx = lax.broadcasted_iota(jnp.int32, (L, L), 1)
    causal = s_idx <= t_idx
    gates_c = gc_ref[...] + br_ref[...]
    gates_r = gr_ref[...] + bc_ref[...]
    for h in range(n_heads):
        cols = slice(h * dh, (h + 1) * dh)
        ii_c = gates_c[:, h:h + 1]
        lf_c = _log_sigmoid(gates_c[:, n_heads + h:n_heads + h + 1])
        ii_r = gates_r[h:h + 1, :]
        lf_r = _log_sigmoid(gates_r[n_heads + h:n_heads + h + 1, :])
        b_c = jnp.sum(jnp.where(causal, lf_r, 0.0), axis=1, keepdims=True)
        b_r = jnp.sum(jnp.where(t_idx <= s_idx, lf_c, 0.0), axis=0, keepdims=True)
        g_r = ii_r - b_r
        cm_c = jnp.max(jnp.where(causal, g_r, -jnp.inf), axis=1, keepdims=True)
        m0 = m_ref[h:h + 1, 0:1]
        m_t = b_c + jnp.maximum(m0, cm_c)
        inter = jnp.exp(b_c + m0 - m_t)
        dmat = jnp.where(causal, jnp.exp((b_c - m_t) + g_r), 0.0)
        q = q_ref[:, cols].astype(BF16)
        ks = k_ref[:, cols] * kscale
        v = v_ref[:, cols].astype(BF16)
        c0 = c_ref[h]
        n0 = n_ref[h:h + 1, :]
        sc = _dot_nt(q, ks.astype(BF16)) * dmat
        num = inter * _dot(q, c0.astype(BF16)) + _dot(sc.astype(BF16), v)
        qn = jnp.sum(q_ref[:, cols] * n0, axis=1, keepdims=True)
        den = inter * qn + jnp.sum(sc, axis=1, keepdims=True)
        hid = num / jnp.maximum(jnp.abs(den), jnp.exp(-m_t))
        m_last = m_t[L - 1:L, :]
        b_last = b_c[L - 1:L, :]
        decay = jnp.exp(b_last + m0 - m_last)
        wk = jnp.exp(b_last - b_c + ii_c - m_last)
        kw = ks * wk
        c_ref[h] = decay * c0 + _dot_tn(kw.astype(BF16), v)
        n_ref[h:h + 1, :] = decay * n0 + jnp.sum(kw, axis=0, keepdims=True)
        m_ref[h:h + 1, :] = jnp.broadcast_to(m_last, (1, m_ref.shape[1]))
        out = hid * jax.nn.sigmoid(o_ref[:, cols]) * _silu(g_ref[:, cols])
        mix_ref[:, cols] = out.astype(mix_ref.dtype)


def _mlstm_prompt(z3, zg3, gate_bias, *, col0, n_heads=N_HEADS_D, chunk=MLSTM_CHUNK):
    bsz, seq, _ = z3.shape
    width = (z3.shape[2] - col0) // 5
    dh = width // n_heads
    assert seq % chunk == 0 and col0 % width == 0 and zg3.shape[2] == 2 * n_heads == SUBLANES
    base = col0 // width
    zg_rows = jnp.transpose(zg3, (0, 2, 1))
    bias_lane = gate_bias.astype(F32).reshape(1, SUBLANES)
    bias_sub = bias_lane.reshape(SUBLANES, 1)

    def col(off):
        return pl.BlockSpec((None, chunk, width), lambda b, c, off=off: (b, c, base + off))

    mix, c_f, n_f, m_f = pl.pallas_call(
        _mlstm_prompt_kernel,
        grid=(bsz, seq // chunk),
        in_specs=[col(0), col(1), col(2), col(3), col(4),
                  pl.BlockSpec((None, chunk, SUBLANES), lambda b, c: (b, c, 0)),
                  pl.BlockSpec((None, SUBLANES, chunk), lambda b, c: (b, 0, c)),
                  pl.BlockSpec((SUBLANES, 1), lambda b, c: (0, 0)),
                  pl.BlockSpec((1, SUBLANES), lambda b, c: (0, 0))],
        out_specs=[pl.BlockSpec((None, chunk, width), lambda b, c: (b, c, 0)),
                   pl.BlockSpec((None, n_heads, dh, dh), lambda b, c: (b, 0, 0, 0)),
                   pl.BlockSpec((None, n_heads, dh), lambda b, c: (b, 0, 0)),
                   pl.BlockSpec((None, n_heads, LANES), lambda b, c: (b, 0, 0))],
        out_shape=[jax.ShapeDtypeStruct((bsz, seq, width), BF16),
                   jax.ShapeDtypeStruct((bsz, n_heads, dh, dh), F32),
                   jax.ShapeDtypeStruct((bsz, n_heads, dh), F32),
                   jax.ShapeDtypeStruct((bsz, n_heads, LANES), F32)],
        compiler_params=_params("parallel", "arbitrary"),
        name="mlstm_prompt",
    )(z3, z3, z3, z3, z3, zg3, zg_rows, bias_sub, bias_lane)
    return mix, c_f, n_f, m_f[:, :, 0]


def _group_tokens(n_heads):
    assert LANES % n_heads == 0
    return LANES // n_heads


def _stack_groups(zt, n_grp):
    return jnp.concatenate([zt[:, g * LANES:(g + 1) * LANES] for g in range(n_grp)], axis=0)


def _unstack_groups(a, n_grp, n_heads):
    return jnp.concatenate([a[g * n_heads:(g + 1) * n_heads, :] for g in range(n_grp)], axis=1)


def _own_head_mask(n_grp, n_heads):
    r = lax.broadcasted_iota(jnp.int32, (n_grp * n_heads, LANES), 0)
    c = lax.broadcasted_iota(jnp.int32, (n_grp * n_heads, LANES), 1)
    return (r % n_heads) == (c % n_heads)


def _dil_sample_kernel(*refs, n_pat):
    q_ref, kn_ref, vn_ref, g_ref, bh_ref, bn_ref = refs[:6]
    k_refs = refs[6:6 + n_pat]
    v_refs = refs[6 + n_pat:6 + 2 * n_pat]
    o_ref = refs[6 + 2 * n_pat]
    steps, n_heads, dh = k_refs[0].shape
    n_grp = steps * n_heads // LANES
    scale = 1.0 / math.sqrt(dh)
    q = q_ref[...] * scale
    qb = q.astype(BF16)
    zn = jnp.sum(q * kn_ref[...], axis=-1, keepdims=True)
    z_new = [zn + bn_ref[g][:, 0:1] for g in range(n_pat)]
    z_hist = []
    for g in range(n_pat):
        k2 = k_refs[g][...].reshape(steps * n_heads, dh).astype(BF16)
        z_hist.append(_stack_groups(_dot_nt(qb, k2), n_grp) + bh_ref[g])
    mx = functools.reduce(jnp.maximum, z_new)
    for z in z_hist:
        zr = jnp.max(z, axis=-1, keepdims=True)
        for gi in range(n_grp):
            mx = jnp.maximum(mx, zr[gi * n_heads:(gi + 1) * n_heads, :])
    mx_rows = jnp.concatenate([mx] * n_grp, axis=0)
    den = functools.reduce(jnp.add, [jnp.exp(z - mx) for z in z_new])
    w_new = den
    w_hist = [jnp.exp(z - mx_rows) for z in z_hist]
    for w in w_hist:
        wr = jnp.sum(w, axis=-1, keepdims=True)
        for gi in range(n_grp):
            den = den + wr[gi * n_heads:(gi + 1) * n_heads, :]
    inv = 1.0 / den
    inv_rows = jnp.concatenate([inv] * n_grp, axis=0)
    acc = (w_new * inv) * vn_ref[...]
    for g in range(n_pat):
        p = _unstack_groups((w_hist[g] * inv_rows).astype(BF16), n_grp, n_heads)
        v2 = v_refs[g][...].reshape(steps * n_heads, dh).astype(BF16)
        acc = acc + _dot(p, v2)
    o_ref[...] = (acc * _silu(g_ref[...])).astype(o_ref.dtype)


def _dil_sample(zs, buf_k, buf_v, layer, rel_bias, past_len):
    _, bsz, n_hist, n_heads, dh = buf_k.shape
    width = n_heads * dh
    n_pat = len(DILATIONS)
    steps = BAND
    tok = _group_tokens(n_heads)
    n_grp = steps // tok
    bias_hist, bias_new = _sample_bias(rel_bias, n_hist, past_len)
    bh = bias_hist.reshape(n_pat, n_grp, 1, tok, n_heads)
    own = jnp.asarray(np.eye(n_heads, dtype=bool)).reshape(1, 1, n_heads, 1, n_heads)
    bh = jnp.where(own, bh, NEG).reshape(n_pat, n_grp * n_heads, LANES)
    bn = jnp.broadcast_to(bias_new[:, :, None], (n_pat, n_heads, LANES))
    z4 = zs[:, :4 * width].reshape(bsz, 4, n_heads, dh)
    k_views, v_views, kv_specs = [], [], []
    for window, dil in DILATIONS:
        assert window // dil == steps and n_hist % window == 0 and window % dil == 0
        shape = (buf_k.shape[0], bsz, n_hist // dil, dil, n_heads, dh)
        k_views.append(buf_k.reshape(shape))
        v_views.append(buf_v.reshape(shape))
        blk = n_hist // window - 1
        kv_specs.append(pl.BlockSpec((None, None, steps, None, n_heads, dh),
                                     lambda b, blk=blk: (layer, b, blk, 0, 0, 0)))

    def zrow(off):
        return pl.BlockSpec((None, None, n_heads, dh), lambda b, off=off: (b, off, 0, 0))

    return pl.pallas_call(
        functools.partial(_dil_sample_kernel, n_pat=n_pat),
        grid=(bsz,),
        in_specs=[zrow(0), zrow(1), zrow(2), zrow(3),
                  pl.BlockSpec((n_pat, n_grp * n_heads, LANES), lambda b: (0, 0, 0)),
                  pl.BlockSpec((n_pat, n_heads, LANES), lambda b: (0, 0, 0))] + kv_specs + kv_specs,
        out_specs=pl.BlockSpec((None, n_heads, dh), lambda b: (b, 0, 0)),
        out_shape=jax.ShapeDtypeStruct((bsz, n_heads, dh), BF16),
        compiler_params=_params("parallel"),
        name="dilated_attn_sample",
    )(z4, z4, z4, z4, bh, bn, *k_views, *v_views).reshape(bsz, width)


def _pool_sample_kernel(u_ref, hist_ref, gate_ref, pw_ref, ps_ref, o_ref):
    grp = pw_ref.shape[1]
    n_hist = hist_ref.shape[0]
    for g, win in enumerate(POOL_WINDOWS):
        cols = slice(g * grp, (g + 1) * grp)
        x = u_ref[:, cols]
        n_old = min(win - 1, n_hist)
        wsum = x
        for r in range(n_hist - n_old, n_hist):
            wsum = wsum + hist_ref[r, :, cols]
        pooled = wsum / float(n_old + 1) - x
        y = _dot(pooled.astype(BF16), pw_ref[g].astype(BF16)) * ps_ref[:, cols]
        o_ref[:, cols] = (y * _silu(gate_ref[:, cols])).astype(o_ref.dtype)


def _pool_sample(zs, hist, pool_w, pool_scale, layer, *, u_off, gate_off):
    bsz = zs.shape[0]
    _, n_grp, grp, _ = pool_w.shape
    width = n_grp * grp
    n_hist = hist.shape[1]
    hist = jnp.transpose(hist, (1, 0, 2))
    return pl.pallas_call(
        _pool_sample_kernel,
        grid=(1,),
        in_specs=[pl.BlockSpec((bsz, width), lambda i: (0, u_off)),
                  pl.BlockSpec((n_hist, bsz, width), lambda i: (0, 0, 0)),
                  pl.BlockSpec((bsz, width), lambda i: (0, gate_off)),
                  pl.BlockSpec((None, n_grp, grp, grp), lambda i: (layer, 0, 0, 0)),
                  pl.BlockSpec((None, 1, width), lambda i: (layer, 0, 0))],
        out_specs=pl.BlockSpec((bsz, width), lambda i: (0, 0)),
        out_shape=jax.ShapeDtypeStruct((bsz, width), BF16),
        compiler_params=_params("arbitrary"),
        name="pool_mix_sample",
    )(zs, hist, zs, pool_w, pool_scale.reshape(pool_scale.shape[0], 1, width))


SB_SAMPLE_PAGES = 4


def _sb_sample_kernel(pt_ref, q_ref, g_ref, bias_ref, ut_ref, *refs):
    k_refs = refs[:SB_SAMPLE_PAGES]
    v_refs = refs[SB_SAMPLE_PAGES:2 * SB_SAMPLE_PAGES]
    o_ref, carry_sc, acc_sc = refs[2 * SB_SAMPLE_PAGES:]
    p = pl.program_id(1)
    page, n_heads, dh = k_refs[0].shape
    n_grp = page * n_heads // LANES
    scale = 1.0 / math.sqrt(dh)
    pages = range(SB_SAMPLE_PAGES)

    @pl.when(p == 0)
    def _():
        carry_sc[...] = jnp.zeros_like(carry_sc)
        acc_sc[...] = jnp.zeros_like(acc_sc)

    qb = (q_ref[...] * scale).astype(BF16)
    own = _own_head_mask(n_grp, n_heads)
    z = [_stack_groups(_dot_nt(qb, k_refs[u][...].reshape(page * n_heads, dh).astype(BF16)), n_grp)
         + bias_ref[...] for u in pages]
    lhs, log_take = [], []
    for u in pages:
        sp = _softplus(z[u])
        hi, lo = _split_hi_lo(jnp.where(own, -sp, 0.0))
        lhs.append(jnp.concatenate([hi, lo], axis=1))
        log_take.append(z[u] - sp)
    sums = [_dot(lhs[u], ut_ref[...]) for u in pages]
    after = carry_sc[...]
    offs = []
    for u in pages:
        total = sums[u][:, LANES:]
        offs_u = [None] * n_grp
        for gi in range(n_grp - 1, -1, -1):
            offs_u[gi] = after
            after = after + total[gi * n_heads:(gi + 1) * n_heads, :]
        offs.append(jnp.concatenate(offs_u, axis=0))
    carry_sc[...] = after
    acc = acc_sc[...]
    for u in pages:
        a = jnp.where(own, jnp.exp(log_take[u] + sums[u][:, :LANES] + offs[u]), 0.0)
        v2 = v_refs[u][...].reshape(page * n_heads, dh).astype(BF16)
        acc = acc + _dot(_unstack_groups(a.astype(BF16), n_grp, n_heads), v2)
    acc_sc[...] = acc

    @pl.when(p == pl.num_programs(1) - 1)
    def _():
        o_ref[...] = (acc_sc[...] * _silu(g_ref[...])).astype(o_ref.dtype)


def _sb_sample(zs, pool_k, pool_v, layer, page_table, sb_bias):
    bsz, n_pages = page_table.shape
    _, _, page, n_heads, dh = pool_k.shape
    width = n_heads * dh
    tok = _group_tokens(n_heads)
    n_grp = page // tok
    lane_tok = np.arange(LANES) // n_heads
    lane_head = np.arange(LANES) % n_heads
    later = (lane_head[:, None] == lane_head[None, :]) & (lane_tok[:, None] > lane_tok[None, :])
    half = np.concatenate([later.astype(np.float32), np.ones((LANES, LANES), np.float32)], axis=1)
    ut = jnp.asarray(np.concatenate([half, half], axis=0), BF16)
    z4 = zs[:, :4 * width].reshape(bsz, 4, n_heads, dh)
    bias = jnp.tile(jnp.broadcast_to(sb_bias.astype(F32)[:, None], (n_heads, LANES)), (n_grp, 1))

    def page_spec(u):
        return pl.BlockSpec((None, None, page, n_heads, dh),
                            lambda b, p, pt: (layer, pt[b, n_pages - 1 - (p * SB_SAMPLE_PAGES + u)], 0, 0, 0))

    assert n_pages % SB_SAMPLE_PAGES == 0
    page_specs = [page_spec(u) for u in range(SB_SAMPLE_PAGES)]
    return pl.pallas_call(
        _sb_sample_kernel,
        grid_spec=pltpu.PrefetchScalarGridSpec(
            num_scalar_prefetch=1,
            grid=(bsz, n_pages // SB_SAMPLE_PAGES),
            in_specs=[pl.BlockSpec((None, None, n_heads, dh), lambda b, p, pt: (b, 0, 0, 0)),
                      pl.BlockSpec((None, None, n_heads, dh), lambda b, p, pt: (b, 3, 0, 0)),
                      pl.BlockSpec((n_grp * n_heads, LANES), lambda b, p, pt: (0, 0)),
                      pl.BlockSpec((2 * LANES, 2 * LANES), lambda b, p, pt: (0, 0))] + page_specs + page_specs,
            out_specs=pl.BlockSpec((None, n_heads, dh), lambda b, p, pt: (b, 0, 0)),
            scratch_shapes=[pltpu.VMEM((n_heads, LANES), F32), pltpu.VMEM((n_heads, dh), F32)]),
        out_shape=jax.ShapeDtypeStruct((bsz, n_heads, dh), BF16),
        compiler_params=_params("parallel", "arbitrary"),
        name="stick_breaking_sample",
    )(page_table, z4, z4, bias, ut, *([pool_k] * SB_SAMPLE_PAGES), *([pool_v] * SB_SAMPLE_PAGES)).reshape(bsz, width)


def _mlstm_sample_kernel(q_ref, k_ref, v_ref, o_ref, g_ref, sc_ref, c0_ref, n0_ref,
                         mix_ref, c_ref, n_ref, m_ref):
    dh = c0_ref.shape[0]
    kscale = 1.0 / math.sqrt(dh)
    s = sc_ref[...]
    ii = s[:, 0:1] + s[:, 2:3]
    lf = _log_sigmoid(s[:, 1:2] + s[:, 3:4])
    m0 = s[:, 4:5]
    m_t = jnp.maximum(lf + m0, ii)
    inter = jnp.exp(lf + m0 - m_t)
    wk = jnp.exp(ii - m_t)
    q = q_ref[...]
    ks = k_ref[...] * kscale
    v = v_ref[...]
    c0 = c0_ref[...]
    n0 = n0_ref[...]
    sc = jnp.sum(q * ks, axis=0, keepdims=True) * wk
    num = inter * jnp.sum(c0 * q, axis=0, keepdims=True) + sc * v
    den = inter * jnp.sum(q * n0, axis=0, keepdims=True) + sc
    hid = num / jnp.maximum(jnp.abs(den), jnp.exp(-m_t))
    c_ref[...] = inter * c0 + (ks * wk) * v
    n_ref[...] = inter * n0 + ks * wk
    m_ref[...] = jnp.broadcast_to(m_t, m_ref.shape)
    mix_ref[...] = (hid * jax.nn.sigmoid(o_ref[...]) * _silu(g_ref[...])).astype(mix_ref.dtype)


def _mlstm_sample(zs, zg, gate_bias, c0, n0, m0, layer, *, col0, n_heads=N_HEADS_D):
    bsz = zs.shape[0]
    dh = c0.shape[-1]
    width = n_heads * dh
    grp = zs[:, col0:col0 + 5 * width].reshape(bsz, 5, n_heads, dh)
    q_col = grp[:, 0].reshape(bsz, n_heads, dh, 1)
    k_col = grp[:, 1].reshape(bsz, n_heads, dh, 1)
    v_row = grp[:, 2].reshape(bsz, n_heads, 1, dh)
    o_row = grp[:, 3].reshape(bsz, n_heads, 1, dh)
    g_row = grp[:, 4].reshape(bsz, n_heads, 1, dh)
    gb = gate_bias.astype(F32)
    scal = jnp.stack([zg[:, :n_heads], zg[:, n_heads:2 * n_heads],
                      jnp.broadcast_to(gb[0][None], (bsz, n_heads)),
                      jnp.broadcast_to(gb[1][None], (bsz, n_heads)),
                      m0.astype(F32)], axis=-1)
    scal = jnp.pad(scal, ((0, 0), (0, 0), (0, LANES - scal.shape[-1]))).reshape(bsz, n_heads, 1, LANES)

    def spec(r, c):
        return pl.BlockSpec((None, None, r, c), lambda b, h: (b, h, 0, 0))

    mix, c_n, n_n, m_n = pl.pallas_call(
        _mlstm_sample_kernel,
        grid=(bsz, n_heads),
        in_specs=[spec(dh, 1), spec(dh, 1), spec(1, dh), spec(1, dh), spec(1, dh), spec(1, LANES),
                  pl.BlockSpec((None, None, None, dh, dh), lambda b, h: (layer, b, h, 0, 0)),
                  spec(dh, 1)],
        out_specs=[spec(1, dh), spec(dh, dh), spec(dh, 1), spec(1, LANES)],
        out_shape=[jax.ShapeDtypeStruct((bsz, n_heads, 1, dh), BF16),
                   jax.ShapeDtypeStruct((bsz, n_heads, dh, dh), F32),
                   jax.ShapeDtypeStruct((bsz, n_heads, dh, 1), F32),
                   jax.ShapeDtypeStruct((bsz, n_heads, 1, LANES), F32)],
        compiler_params=_params("parallel", "parallel"),
        name="mlstm_sample",
    )(q_col, k_col, v_row, o_row, g_row, scal, c0, n0.astype(F32).reshape(bsz, n_heads, dh, 1))
    return mix.reshape(bsz, width), c_n, n_n.reshape(bsz, n_heads, dh), m_n[:, :, 0, 0]


PROJ_TM = 2048
PROJ_TN = 512


def kernel(x_prompt, x_sample, cache_a_k, cache_a_v, state_pool, cache_c_k, cache_c_v, state_mlstm_c,
           state_mlstm_n, state_mlstm_m, page_table, rel_bias, w_in_even, pool_w, pool_scale, w_out_even,
           w_in_odd, sb_bias, mlstm_gate_bias, w_out_odd, ln_g, ln_b):
    bsz, seq, d_model = x_prompt.shape
    dbs, dec_seq, _ = x_sample.shape
    assert dec_seq == 1
    depth = ln_g.shape[0]
    alpha = (2 * depth) ** 0.25
    past_len = page_table.shape[1] * PAGE_SIZE
    mix_a = d_model // 2
    n_heads = mix_a // HEAD_DIM
    mix_b = d_model - mix_a
    n_rows = bsz * seq
    even_cols = w_in_even.shape[2]
    odd_main = 9 * mix_a
    n_gate = w_in_odd.shape[2] - odd_main
    w_in_odd_nk = jnp.swapaxes(w_in_odd, 1, 2)

    xp_f = x_prompt.reshape(n_rows, d_model)
    xp_b = xp_f.astype(BF16)
    xs_f = x_sample.reshape(dbs, d_model)
    xs_b = xs_f.astype(BF16)
    band_bias = _band_bias(rel_bias)
    assert mix_a == mix_b

    zp_even, zp_odd, zs_even, zs_odd = [], [], [], []
    mcp, mnp, mmp, mcs, mns, mms = [], [], [], [], [], []

    def project(w, layer_j, n_main, w_is_nk):
        parts = [_proj(xp_b, xs_b, w, layer_j, lo, hi - lo, w_is_nk=w_is_nk, tm=PROJ_TM, tn=PROJ_TN)
                 for lo, hi in ((0, mix_a), (mix_a, 2 * mix_a), (2 * mix_a, 3 * mix_a), (3 * mix_a, n_main))]
        prompt = [p.reshape(bsz, seq, -1) for p, _ in parts]
        sample = [s for _, s in parts]
        return prompt, sample, jnp.concatenate(sample[:3] + [sample[3][:, :mix_a]], axis=1)

    for layer in range(depth):
        j = layer // 2
        if layer % 2 == 0:
            zp, zs, zs4 = project(w_in_even, j, even_cols, False)
            q3, k3, v3, r3 = zp
            oa = _dil_prompt(q3, k3, v3, r3, band_bias, n_heads)
            ob = _pool_prompt(r3, pool_w, pool_scale, j, u_off=1, gate_off=2)
            oas = _dil_sample(zs4, cache_a_k, cache_a_v, j, rel_bias, past_len)
            obs = _pool_sample(zs[3], state_pool[j], pool_w, pool_scale, j, u_off=1, gate_off=2)
            yp, ys = _out_proj(oa.reshape(n_rows, mix_a), ob.reshape(n_rows, mix_b), oas, obs, w_out_even, j,
                               tm=PROJ_TM, tn=PROJ_TN)
            zp_even.append(zp)
            zs_even.append(zs)
        else:
            zp, zs, zs4 = project(w_in_odd_nk, j, odd_main, True)
            q3, k3, v3, r3 = zp
            zg, zgs = _gate_proj(xp_b, xs_b, w_in_odd_nk, j, odd_main, n_gate, tm=PROJ_TM)
            oc = _sb_prompt(q3, k3, v3, r3, sb_bias[j], n_heads)
            od, c_f, n_f, m_f = _mlstm_prompt(r3, zg.reshape(bsz, seq, n_gate), mlstm_gate_bias[j], col0=mix_a)
            ocs = _sb_sample(zs4, cache_c_k, cache_c_v, j, page_table, sb_bias[j])
            ods, c_n, n_n, m_n = _mlstm_sample(zs[3], zgs, mlstm_gate_bias[j], state_mlstm_c, state_mlstm_n[j],
                                               state_mlstm_m[j], j, col0=mix_a)
            yp, ys = _out_proj(oc.reshape(n_rows, mix_a), od.reshape(n_rows, mix_b), ocs, ods, w_out_odd, j,
                               tm=PROJ_TM, tn=PROJ_TN)
            zp_odd.append(zp)
            zs_odd.append(zs)
            mcp.append(c_f)
            mnp.append(n_f)
            mmp.append(m_f)
            mcs.append(c_n)
            mns.append(n_n)
            mms.append(m_n)
        xp_f, xp_b = _residual_ln(xp_f, yp, ln_g[layer], ln_b[layer], alpha=alpha, tm=256)
        xs_f, xs_b = _residual_ln(xs_f, ys, ln_g[layer], ln_b[layer], alpha=alpha, tm=dbs)

    keep_a = min(A_WINDOW, seq)
    keep_u = min(POOL_HIST, seq)

    k_part, v_part, rest_part = 1, 2, 3

    def heads(a):
        return a.reshape(a.shape[:-1] + (n_heads, HEAD_DIM))

    def pick(layers, part, rows=slice(None), cols=slice(None)):
        return jnp.stack([zl[part][..., rows, cols] for zl in layers])

    new_row = np.newaxis
    last_a = slice(seq - keep_a, seq)
    u_cols = slice(mix_a, mix_a + mix_b)
    a_k_prompt = heads(pick(zp_even, k_part, last_a))
    a_v_prompt = heads(pick(zp_even, v_part, last_a))
    pool_prompt = pick(zp_even, rest_part, slice(seq - keep_u, seq), u_cols)
    a_k_sample, a_v_sample = _shift_windows(cache_a_k, cache_a_v, heads(pick(zs_even, k_part))[:, :, new_row],
                                            heads(pick(zs_even, v_part))[:, :, new_row])
    pool_sample = jnp.concatenate([state_pool[:, :, 1:], pick(zs_even, rest_part, cols=u_cols)[:, :, new_row]],
                                  axis=2)
    c_k_prompt = heads(pick(zp_odd, k_part))
    c_v_prompt = heads(pick(zp_odd, v_part))
    c_k_sample = heads(pick(zs_odd, k_part))[:, :, new_row]
    c_v_sample = heads(pick(zs_odd, v_part))[:, :, new_row]
    return (xp_f.reshape(bsz, seq, d_model), xs_f.reshape(dbs, 1, d_model),
            a_k_prompt, a_v_prompt, a_k_sample, a_v_sample, pool_prompt, pool_sample,
            c_k_prompt, c_v_prompt, c_k_sample, c_v_sample,
            jnp.stack(mcp), jnp.stack(mnp), jnp.stack(mmp), jnp.stack(mcs), jnp.stack(mns), jnp.stack(mms))
```

```python
import functools
import math

import numpy as np
import jax
import jax.numpy as jnp
from jax import lax
from jax.experimental import pallas as pl
from jax.experimental.pallas import tpu as pltpu

HEAD_DIM = 128
DILATIONS = ((128, 1), (512, 4), (2048, 16))
A_WINDOW = max(w for w, _ in DILATIONS)
BAND = 128
POOL_WINDOWS = (2, 4, 8, 16)
POOL_HIST = max(POOL_WINDOWS) - 1
N_HEADS_D = 4
MLSTM_CHUNK = 128
N_REL_BUCKETS = 32
REL_MAX_DISTANCE = A_WINDOW
LN_EPS = 1e-5
LOG2E = 1.0 / math.log(2.0)
PAGE_SIZE = 128
NEG = -1e30

VMEM_LIMIT = 56 * 1024 * 1024
LANES = 128
SUBLANES = 8

BF16 = jnp.bfloat16
F32 = jnp.float32


def _params(*sem):
    return pltpu.CompilerParams(dimension_semantics=sem, vmem_limit_bytes=VMEM_LIMIT)


def _silu(x):
    return x * jax.nn.sigmoid(x)


def _dot(a, b):
    return jnp.dot(a, b, preferred_element_type=F32)


def _dot_nt(a, b):
    return lax.dot_general(a, b, (((1,), (1,)), ((), ())), preferred_element_type=F32)


def _dot_tn(a, b):
    return lax.dot_general(a, b, (((0,), (0,)), ((), ())), preferred_element_type=F32)


def _split_hi_lo(x):
    hi = x.astype(BF16)
    lo = (x - hi.astype(F32)).astype(BF16)
    return hi, lo


def _softplus(z):
    return jnp.maximum(z, 0.0) + jnp.log(1.0 + jnp.exp(-jnp.abs(z)))


def _log_sigmoid(z):
    return -_softplus(-z)


def _proj_kernel(x_ref, xs_ref, w_ref, o_ref, os_ref, *, w_is_nk):
    wb = w_ref[...].astype(BF16)
    mm = _dot_nt if w_is_nk else _dot
    o_ref[...] = mm(x_ref[...], wb)
    os_ref[...] = mm(xs_ref[...], wb)


def _proj(x, xs, w, layer, n_parts, part, *, w_is_nk, tm, tn):
    m, k = x.shape
    ms = xs.shape[0]
    assert m % tm == 0 and part % tn == 0 and w.shape[2 if w_is_nk else 1] == k
    per = part // tn
    if w_is_nk:
        w_spec = pl.BlockSpec((None, tn, k), lambda i, j: (layer, j, 0))
    else:
        w_spec = pl.BlockSpec((None, k, tn), lambda i, j: (layer, 0, j))
    o, os = pl.pallas_call(
        functools.partial(_proj_kernel, w_is_nk=w_is_nk),
        grid=(m // tm, n_parts * per),
        in_specs=[pl.BlockSpec((tm, k), lambda i, j: (i, 0), pipeline_mode=pl.Buffered(1)),
                  pl.BlockSpec((ms, k), lambda i, j: (0, 0)),
                  w_spec],
        out_specs=[pl.BlockSpec((None, tm, tn), lambda i, j: (j // per, i, j % per)),
                   pl.BlockSpec((None, None, ms, tn), lambda i, j: (j // per, i, 0, j % per))],
        out_shape=[jax.ShapeDtypeStruct((n_parts, m, part), F32),
                   jax.ShapeDtypeStruct((n_parts, m // tm, ms, part), F32)],
        compiler_params=_params("arbitrary", "arbitrary"),
        name="proj",
    )(x, xs, w)
    return o, os[:, 0]


def _gate_proj_kernel(x_ref, xs_ref, w_ref, o_ref, os_ref):
    wb = w_ref[...].astype(BF16)
    o_ref[...] = _dot_nt(x_ref[...], wb)
    os_ref[...] = _dot_nt(xs_ref[...], wb)


def _gate_proj(x, xs, w_nk, layer, row0, n_gate, *, tm):
    m, k = x.shape
    ms = xs.shape[0]
    assert m % tm == 0 and row0 % n_gate == 0 and n_gate % SUBLANES == 0
    o, os = pl.pallas_call(
        _gate_proj_kernel,
        grid=(m // tm,),
        in_specs=[pl.BlockSpec((tm, k), lambda i: (i, 0)),
                  pl.BlockSpec((ms, k), lambda i: (0, 0)),
                  pl.BlockSpec((None, n_gate, k), lambda i: (layer, row0 // n_gate, 0))],
        out_specs=[pl.BlockSpec((tm, n_gate), lambda i: (i, 0)),
                   pl.BlockSpec((None, ms, n_gate), lambda i: (i, 0, 0))],
        out_shape=[jax.ShapeDtypeStruct((m, n_gate), F32), jax.ShapeDtypeStruct((m // tm, ms, n_gate), F32)],
        compiler_params=_params("arbitrary"),
        name="gate_proj",
    )(x, xs, w_nk)
    return o, os[0]


def _out_proj_kernel(xa_ref, xb_ref, sa_ref, sb_ref, w_ref, o_ref, os_ref):
    ka = xa_ref.shape[1]
    wa = w_ref[:ka, :].astype(BF16)
    wb = w_ref[ka:, :].astype(BF16)
    o_ref[...] = _dot(xa_ref[...], wa) + _dot(xb_ref[...], wb)
    os_ref[...] = _dot(sa_ref[...], wa) + _dot(sb_ref[...], wb)


def _out_proj(xa, xb, sa, sb, w, layer, *, tm, tn):
    m, ka = xa.shape
    kb = xb.shape[1]
    ms = sa.shape[0]
    n = w.shape[2]
    assert m % tm == 0 and n % tn == 0 and w.shape[1] == ka + kb
    o, os = pl.pallas_call(
        _out_proj_kernel,
        grid=(m // tm, n // tn),
        in_specs=[pl.BlockSpec((tm, ka), lambda i, j: (i, 0), pipeline_mode=pl.Buffered(1)),
                  pl.BlockSpec((tm, kb), lambda i, j: (i, 0), pipeline_mode=pl.Buffered(1)),
                  pl.BlockSpec((ms, ka), lambda i, j: (0, 0)),
                  pl.BlockSpec((ms, kb), lambda i, j: (0, 0)),
                  pl.BlockSpec((None, ka + kb, tn), lambda i, j: (layer, 0, j))],
        out_specs=[pl.BlockSpec((tm, tn), lambda i, j: (i, j)),
                   pl.BlockSpec((None, ms, tn), lambda i, j: (i, 0, j))],
        out_shape=[jax.ShapeDtypeStruct((m, n), F32), jax.ShapeDtypeStruct((m // tm, ms, n), F32)],
        compiler_params=_params("arbitrary", "arbitrary"),
        name="out_proj",
    )(xa, xb, sa, sb, w)
    return o, os[0]


SHIFT_ROWS = 256


def _shift_kernel(ck_ref, cv_ref, nk_ref, nv_ref, ok_ref, ov_ref, *, per_window):
    c = pl.program_id(0)
    rows = ok_ref.shape[0]
    last = c == pl.num_programs(0) - 1
    for src, new, dst in ((ck_ref, nk_ref, ok_ref), (cv_ref, nv_ref, ov_ref)):
        @pl.when(jnp.logical_not(last))
        def _(src=src, dst=dst):
            dst[...] = src[...]

        @pl.when(last)
        def _(src=src, dst=dst):
            dst[0:rows - 1] = src[1:rows]

        @pl.when(c % per_window == per_window - 1)
        def _(new=new, dst=dst):
            dst[rows - 1:rows] = new[...]


def _shift_windows(cache_k, cache_v, new_k, new_v):
    n_layers, bsz, n_hist, n_heads, dh = cache_k.shape
    total = n_layers * bsz * n_hist
    assert n_hist % SHIFT_ROWS == 0
    per_window = n_hist // SHIFT_ROWS
    flat = (total, n_heads, dh)
    src_spec = pl.BlockSpec((pl.Element(SHIFT_ROWS), pl.Element(n_heads), pl.Element(dh)),
                            lambda c: (jnp.minimum(c * SHIFT_ROWS + 1, total - SHIFT_ROWS), 0, 0))
    new_spec = pl.BlockSpec((None, 1, n_heads, dh), lambda c: (c // per_window, 0, 0, 0))
    dst_spec = pl.BlockSpec((SHIFT_ROWS, n_heads, dh), lambda c: (c, 0, 0))
    ok, ov = pl.pallas_call(
        functools.partial(_shift_kernel, per_window=per_window),
        grid=(total // SHIFT_ROWS,),
        in_specs=[src_spec, src_spec, new_spec, new_spec],
        out_specs=[dst_spec, dst_spec],
        out_shape=[jax.ShapeDtypeStruct(flat, cache_k.dtype), jax.ShapeDtypeStruct(flat, cache_v.dtype)],
        compiler_params=_params("arbitrary"),
        name="shift_windows",
    )(cache_k.reshape(flat), cache_v.reshape(flat),
      new_k.astype(cache_k.dtype).reshape(n_layers * bsz, 1, n_heads, dh),
      new_v.astype(cache_v.dtype).reshape(n_layers * bsz, 1, n_heads, dh))
    return ok.reshape(cache_k.shape), ov.reshape(cache_v.shape)


def _ln_kernel(x_ref, y_ref, g_ref, b_ref, of_ref, ob_ref, *, alpha):
    h = alpha * x_ref[...] + y_ref[...]
    mu = jnp.mean(h, axis=-1, keepdims=True)
    c = h - mu
    var = jnp.mean(c * c, axis=-1, keepdims=True)
    r = c * lax.rsqrt(var + LN_EPS) * g_ref[...] + b_ref[...]
    of_ref[...] = r
    ob_ref[...] = r.astype(BF16)


def _residual_ln(x, y, g, b, *, alpha, tm):
    m, d = x.shape
    tm = min(tm, m)
    assert m % tm == 0
    row = pl.BlockSpec((tm, d), lambda i: (i, 0))
    vec = pl.BlockSpec((1, d), lambda i: (0, 0))
    return pl.pallas_call(
        functools.partial(_ln_kernel, alpha=alpha),
        grid=(m // tm,),
        in_specs=[row, row, vec, vec],
        out_specs=[row, row],
        out_shape=[jax.ShapeDtypeStruct((m, d), F32), jax.ShapeDtypeStruct((m, d), BF16)],
        compiler_params=_params("parallel"),
        name="residual_ln",
    )(x, y, g.reshape(1, d), b.reshape(1, d))


def _rel_bucket_static(dist):
    exact = N_REL_BUCKETS // 2
    dist = np.asarray(dist, np.int64)
    ratio = np.log(np.maximum(dist, exact) / exact) / math.log(REL_MAX_DISTANCE / exact)
    large = np.minimum(exact + (ratio * (N_REL_BUCKETS - exact)).astype(np.int64), N_REL_BUCKETS - 1)
    return np.where(dist < exact, dist, large).astype(np.int32)


def _band_bias(rel_bias):
    n_heads = rel_bias.shape[1]
    period = 2 * BAND + 1
    out = []
    for window, dil in DILATIONS:
        steps = window // dil
        m = BAND - np.arange(period)
        valid = (m >= 0) & (m <= steps)
        tab = rel_bias[jnp.asarray(_rel_bucket_static(dil * np.clip(m, 0, steps)))].astype(F32)
        tab = jnp.where(jnp.asarray(valid)[:, None], tab, NEG).T
        rows = jnp.tile(tab, (1, BAND))[:, :BAND * 2 * BAND]
        out.append(rows.reshape(n_heads, BAND, 2 * BAND))
    return jnp.stack(out)


def _sample_bias(rel_bias, n_hist, past_len):
    hist, new = [], []
    for window, dil in DILATIONS:
        steps = window // dil
        m = steps - np.arange(steps)
        idx = n_hist - dil * m
        valid = (past_len - dil * m >= 0) & (idx >= 0)
        bias = rel_bias[jnp.asarray(_rel_bucket_static(dil * m))].astype(F32)
        hist.append(jnp.where(jnp.asarray(valid)[:, None], bias, NEG))
        new.append(rel_bias[int(_rel_bucket_static(0))].astype(F32))
    return jnp.stack(hist), jnp.stack(new)


DIL_UNROLL_FIRST = 8
DIL_UNROLL_REST = 5


def _largest_divisor(n, cap):
    return max(d for d in range(1, cap + 1) if n % d == 0)


def _dil_prompt_kernel(q_ref, k_ref, v_ref, g_ref, bm_ref, o_ref, og_ref, lse_ref, *, seq):
    scale = 1.0 / math.sqrt(HEAD_DIM)

    def rows(ref, start, n, dil):
        if dil == 1:
            return ref[pl.ds(start, n), :]
        return ref[pl.ds(start, n, stride=dil), :]

    def blocks(g, dil, bases, first):
        n = range(len(bases))
        back = 0 if first else dil * BAND
        n_keys = BAND if first else 2 * BAND
        bias = bm_ref[g, :, BAND:] if first else bm_ref[g]
        qs = [(rows(q_ref, b, BAND, dil) * scale).astype(BF16) for b in bases]
        ks = [rows(k_ref, b - back, n_keys, dil).astype(BF16) for b in bases]
        s = [_dot_nt(qs[u], ks[u]) + bias for u in n]
        mx = [jnp.max(s[u], axis=-1, keepdims=True) for u in n]
        p = [jnp.exp(s[u] - mx[u]) for u in n]
        l = [jnp.sum(p[u], axis=-1, keepdims=True) for u in n]
        vs = [rows(v_ref, b - back, n_keys, dil).astype(BF16) for b in bases]
        o = [_dot(p[u].astype(BF16), vs[u]) / l[u] for u in n]
        for u, b in enumerate(bases):
            lse = jnp.broadcast_to(mx[u] + jnp.log(l[u]), (BAND, HEAD_DIM))
            if dil == 1:
                og_ref[g, pl.ds(b, BAND), :] = o[u]
                lse_ref[g, pl.ds(b, BAND), :] = lse
            else:
                og_ref[g, pl.ds(b, BAND, stride=dil), :] = o[u]
                lse_ref[g, pl.ds(b, BAND, stride=dil), :] = lse

    def run(n_items, unroll, base_of, g, dil, first):
        main = n_items // unroll

        def body(t, carry):
            blocks(g, dil, [base_of(t * unroll + u) for u in range(unroll)], first)
            return carry

        if main:
            lax.fori_loop(0, main, body, 0)
        if n_items > main * unroll:
            blocks(g, dil, [base_of(idx) for idx in range(main * unroll, n_items)], first)

    for g, (window, dil) in enumerate(DILATIONS):
        n_blk = seq // dil // BAND
        run(dil, _largest_divisor(dil, DIL_UNROLL_FIRST), lambda r: r, g, dil, True)
        if n_blk > 1:
            def rest_base(t, dil=dil, n_blk=n_blk):
                return t // (n_blk - 1) + dil * BAND * (t % (n_blk - 1) + 1)

            n_rest = dil * (n_blk - 1)
            run(n_rest, _largest_divisor(n_rest, DIL_UNROLL_REST), rest_base, g, dil, False)

    tile = 256

    def merge(t, carry):
        sl = pl.ds(pl.multiple_of(t * tile, tile), tile)
        lses = [lse_ref[g, sl, :] for g in range(len(DILATIONS))]
        mx = functools.reduce(jnp.maximum, lses)
        ws = [jnp.exp(l - mx) for l in lses]
        den = functools.reduce(jnp.add, ws)
        num = functools.reduce(jnp.add, [w * og_ref[g, sl, :] for g, w in enumerate(ws)])
        o_ref[sl, :] = (num / den * _silu(g_ref[sl, :])).astype(o_ref.dtype)
        return carry

    lax.fori_loop(0, seq // tile, merge, 0)


def _dil_prompt(z4, band_bias, n_heads, *, parts):
    _, bsz, seq, _ = z4.shape
    n_pat = len(DILATIONS)
    for window, dil in DILATIONS:
        assert seq % (dil * BAND) == 0 and window // dil == BAND
    cols = [pl.BlockSpec((None, None, seq, HEAD_DIM), lambda b, h, p=p: (p, b, 0, h)) for p in parts]
    return pl.pallas_call(
        functools.partial(_dil_prompt_kernel, seq=seq),
        grid=(bsz, n_heads),
        in_specs=cols + [pl.BlockSpec((n_pat, None, BAND, 2 * BAND), lambda b, h: (0, h, 0, 0))],
        out_specs=pl.BlockSpec((None, seq, HEAD_DIM), lambda b, h: (b, 0, h)),
        out_shape=jax.ShapeDtypeStruct((bsz, seq, n_heads * HEAD_DIM), BF16),
        scratch_shapes=[pltpu.VMEM((n_pat, seq, HEAD_DIM), F32), pltpu.VMEM((n_pat, seq, HEAD_DIM), F32)],
        compiler_params=_params("parallel", "parallel"),
        name="dilated_attn_prompt",
    )(z4, z4, z4, z4, band_bias)


POOL_HALO = 16


def _pool_prompt_kernel(u_ref, halo_ref, gate_ref, pw_ref, ps_ref, o_ref, *, tile):
    t = pl.program_id(1)
    grp = pw_ref.shape[1]
    halo = jnp.where(t > 0, halo_ref[...], 0.0)
    pos = (t * tile + lax.broadcasted_iota(jnp.int32, (tile, 1), 0) + 1).astype(F32)
    for g, win in enumerate(POOL_WINDOWS):
        cols = slice(g * grp, (g + 1) * grp)
        x = u_ref[:, cols]
        ext = jnp.concatenate([halo[:, cols], x], axis=0)
        shift = 1
        while shift < win:
            ext = ext + pltpu.roll(ext, shift, 0)
            shift *= 2
        wsum = ext[POOL_HALO:, :]
        pooled = wsum / jnp.minimum(pos, float(win)) - x
        y = _dot(pooled.astype(BF16), pw_ref[g].astype(BF16)) * ps_ref[:, cols]
        o_ref[:, cols] = (y * _silu(gate_ref[:, cols])).astype(o_ref.dtype)


def _pool_prompt(z4, pool_w, pool_scale, layer, *, u_part, gate_part, tile=256):
    _, bsz, seq, _ = z4.shape
    _, n_grp, grp, _ = pool_w.shape
    width = n_grp * grp
    assert seq % tile == 0 and tile % POOL_HALO == 0
    for win in POOL_WINDOWS:
        assert win & (win - 1) == 0 and win - 1 <= POOL_HALO
    per = tile // POOL_HALO
    return pl.pallas_call(
        functools.partial(_pool_prompt_kernel, tile=tile),
        grid=(bsz, seq // tile),
        in_specs=[pl.BlockSpec((None, None, tile, width), lambda b, t: (u_part, b, t, 0)),
                  pl.BlockSpec((None, None, POOL_HALO, width),
                               lambda b, t: (u_part, b, jnp.maximum(t * per - 1, 0), 0)),
                  pl.BlockSpec((None, None, tile, width), lambda b, t: (gate_part, b, t, 0)),
                  pl.BlockSpec((None, n_grp, grp, grp), lambda b, t: (layer, 0, 0, 0)),
                  pl.BlockSpec((None, 1, width), lambda b, t: (layer, 0, 0))],
        out_specs=pl.BlockSpec((None, tile, width), lambda b, t: (b, t, 0)),
        out_shape=jax.ShapeDtypeStruct((bsz, seq, width), BF16),
        compiler_params=_params("parallel", "parallel"),
        name="pool_mix_prompt",
    )(z4, z4, z4, pool_w, pool_scale.reshape(pool_scale.shape[0], 1, width))


SB_TILE = 256
SB_HEADS = 4


def _suffix_matrix(n):
    j = np.arange(n)[:, None]
    s = np.arange(n)[None, :]
    return (j >= s).astype(np.float32)


def _sb_prompt_kernel(q_ref, k_ref, v_ref, g_ref, bias_ref, uu_ref, o_ref, kb_ref, vb_ref,
                      za_ref, la_ref, zb_ref, lb_ref):
    i = pl.program_id(2)
    scale = 1.0 / math.sqrt(HEAD_DIM)

    @pl.when(i == 0)
    def _():
        kb_ref[...] = k_ref[...].astype(BF16)
        vb_ref[...] = v_ref[...].astype(BF16)

    row = lax.broadcasted_iota(jnp.int32, (SB_TILE, SB_TILE), 0)
    colid = lax.broadcasted_iota(jnp.int32, (SB_TILE, SB_TILE), 1)
    heads = [slice(h * HEAD_DIM, (h + 1) * HEAD_DIM) for h in range(SB_HEADS)]
    hs_all = range(SB_HEADS)
    qs = [(q_ref[:, heads[h]] * (scale * LOG2E)).astype(BF16) for h in hs_all]
    biases = [bias_ref[h][:, 0:1] * LOG2E for h in hs_all]

    slots = ((za_ref, la_ref), (zb_ref, lb_ref))

    def front(m, slot, diag):
        z_ref, l_ref = slots[slot]
        sl = pl.ds(pl.multiple_of((i - m) * SB_TILE, SB_TILE), SB_TILE)
        z2 = [_dot_nt(qs[h], kb_ref[sl, heads[h]]) + biases[h] for h in hs_all]
        for h in hs_all:
            sp2 = jnp.maximum(z2[h], 0.0) + jnp.log(1.0 + jnp.exp2(-jnp.abs(z2[h]))) * LOG2E
            if diag:
                sp2 = jnp.where(colid < row, sp2, 0.0)
                z_ref[h] = jnp.where(colid < row, z2[h], NEG)
            else:
                z_ref[h] = z2[h]
            hi, lo = _split_hi_lo(sp2)
            l_ref[h] = jnp.concatenate([hi, lo], axis=1)

    def back(m, slot, spent, acc):
        z_ref, l_ref = slots[slot]
        sl = pl.ds(pl.multiple_of((i - m) * SB_TILE, SB_TILE), SB_TILE)
        incl_all = _dot(l_ref[...].reshape(SB_HEADS * SB_TILE, 2 * SB_TILE), uu_ref[...])
        incl = [incl_all[h * SB_TILE:(h + 1) * SB_TILE] for h in hs_all]
        a = [jnp.exp2(z_ref[h] - incl[h] - spent[h]).astype(BF16) for h in hs_all]
        acc = [acc[h] + _dot(a[h], vb_ref[sl, heads[h]]) for h in hs_all]
        spent = [spent[h] + incl[h][:, 0:1] for h in hs_all]
        return spent, acc

    def finish(acc):
        for h in hs_all:
            o_ref[:, heads[h]] = (acc[h] * _silu(g_ref[:, heads[h]])).astype(o_ref.dtype)

    front(0, 0, True)

    def pair(n, st):
        spent, acc = list(st[:SB_HEADS]), list(st[SB_HEADS:])
        front(2 * n + 1, 1, False)
        spent, acc = back(2 * n, 0, spent, acc)
        front(2 * n + 2, 0, False)
        spent, acc = back(2 * n + 1, 1, spent, acc)
        return tuple(spent) + tuple(acc)

    zero = [jnp.zeros((SB_TILE, 1), F32) for _ in hs_all] + [jnp.zeros((SB_TILE, HEAD_DIM), F32) for _ in hs_all]
    state = lax.fori_loop(0, i // 2, pair, tuple(zero))
    spent, acc = list(state[:SB_HEADS]), list(state[SB_HEADS:])
    done = 2 * (i // 2)

    @pl.when(i % 2 == 0)
    def _():
        finish(back(done, 0, spent, acc)[1])

    @pl.when(i % 2 == 1)
    def _():
        front(done + 1, 1, False)
        sp1, acc1 = back(done, 0, spent, acc)
        finish(back(done + 1, 1, sp1, acc1)[1])


def _sb_prompt(z4, sb_bias, n_heads, *, parts):
    _, bsz, seq, _ = z4.shape
    assert seq % SB_TILE == 0 and n_heads % SB_HEADS == 0
    n_grp = n_heads // SB_HEADS
    width = SB_HEADS * HEAD_DIM
    u = _suffix_matrix(SB_TILE)
    uu = jnp.asarray(np.concatenate([u, u], axis=0), BF16)
    bias = jnp.broadcast_to(sb_bias.astype(F32)[:, None, None], (n_heads, 1, LANES))
    def qcol(p):
        return pl.BlockSpec((None, None, SB_TILE, width), lambda b, h, i: (p, b, i, h))

    def kcol(p):
        return pl.BlockSpec((None, None, seq, width), lambda b, h, i: (p, b, 0, h))

    return pl.pallas_call(
        _sb_prompt_kernel,
        grid=(bsz, n_grp, seq // SB_TILE),
        in_specs=[qcol(parts[0]), kcol(parts[1]), kcol(parts[2]), qcol(parts[3]),
                  pl.BlockSpec((SB_HEADS, 1, LANES), lambda b, h, i: (h, 0, 0)),
                  pl.BlockSpec((2 * SB_TILE, SB_TILE), lambda b, h, i: (0, 0))],
        out_specs=pl.BlockSpec((None, SB_TILE, width), lambda b, h, i: (b, i, h)),
        out_shape=jax.ShapeDtypeStruct((bsz, seq, n_heads * HEAD_DIM), BF16),
        scratch_shapes=[pltpu.VMEM((seq, width), BF16), pltpu.VMEM((seq, width), BF16)]
        + [pltpu.VMEM((SB_HEADS, SB_TILE, SB_TILE), F32), pltpu.VMEM((SB_HEADS, SB_TILE, 2 * SB_TILE), BF16)] * 2,
        compiler_params=_params("parallel", "parallel", "arbitrary"),
        name="stick_breaking_prompt",
    )(z4, z4, z4, z4, bias, uu)


def _mlstm_prompt_kernel(q_ref, k_ref, v_ref, o_ref, g_ref, gc_ref, gr_ref, bc_ref, br_ref,
                         mix_ref, c_ref, n_ref, m_ref):
    chunk = pl.program_id(1)
    n_heads, dh, _ = c_ref.shape
    L = q_ref.shape[0]
    kscale = 1.0 / math.sqrt(dh)

    @pl.when(chunk == 0)
    def _():
        c_ref[...] = jnp.zeros_like(c_ref)
        n_ref[...] = jnp.zeros_like(n_ref)
        m_ref[...] = jnp.zeros_like(m_ref)

    t_idx = lax.broadcasted_iota(jnp.int32, (L, L), 0)
    s_idx = lax.broadcasted_iota(jnp.int32, (L, L), 1)
    causal = s_idx <= t_idx
    gates_c = gc_ref[...] + br_ref[...]
    gates_r = gr_ref[...] + bc_ref[...]
    for h in range(n_heads):
        cols = slice(h * dh, (h + 1) * dh)
        ii_c = gates_c[:, h:h + 1]
        lf_c = _log_sigmoid(gates_c[:, n_heads + h:n_heads + h + 1])
        ii_r = gates_r[h:h + 1, :]
        lf_r = _log_sigmoid(gates_r[n_heads + h:n_heads + h + 1, :])
        b_c = jnp.sum(jnp.where(causal, lf_r, 0.0), axis=1, keepdims=True)
        b_r = jnp.sum(jnp.where(t_idx <= s_idx, lf_c, 0.0), axis=0, keepdims=True)
        g_r = ii_r - b_r
        cm_c = jnp.max(jnp.where(causal, g_r, -jnp.inf), axis=1, keepdims=True)
        m0 = m_ref[h:h + 1, 0:1]
        m_t = b_c + jnp.maximum(m0, cm_c)
        inter = jnp.exp(b_c + m0 - m_t)
        dmat = jnp.where(causal, jnp.exp((b_c - m_t) + g_r), 0.0)
        q = q_ref[:, cols].astype(BF16)
        ks = k_ref[:, cols] * kscale
        v = v_ref[:, cols].astype(BF16)
        c0 = c_ref[h]
        n0 = n_ref[h:h + 1, :]
        sc = _dot_nt(q, ks.astype(BF16)) * dmat
        num = inter * _dot(q, c0.astype(BF16)) + _dot(sc.astype(BF16), v)
        qn = jnp.sum(q_ref[:, cols] * n0, axis=1, keepdims=True)
        den = inter * qn + jnp.sum(sc, axis=1, keepdims=True)
        hid = num / jnp.maximum(jnp.abs(den), jnp.exp(-m_t))
        m_last = m_t[L - 1:L, :]
        b_last = b_c[L - 1:L, :]
        decay = jnp.exp(b_last + m0 - m_last)
        wk = jnp.exp(b_last - b_c + ii_c - m_last)
        kw = ks * wk
        c_ref[h] = decay * c0 + _dot_tn(kw.astype(BF16), v)
        n_ref[h:h + 1, :] = decay * n0 + jnp.sum(kw, axis=0, keepdims=True)
        m_ref[h:h + 1, :] = jnp.broadcast_to(m_last, (1, m_ref.shape[1]))
        out = hid * jax.nn.sigmoid(o_ref[:, cols]) * _silu(g_ref[:, cols])
        mix_ref[:, cols] = out.astype(mix_ref.dtype)


def _mlstm_prompt(z4, zg3, gate_bias, *, first_part, n_heads=N_HEADS_D, chunk=MLSTM_CHUNK):
    _, bsz, seq, width = z4.shape
    dh = width // n_heads
    assert seq % chunk == 0 and zg3.shape[2] == 2 * n_heads == SUBLANES
    zg_rows = jnp.transpose(zg3, (0, 2, 1))
    bias_lane = gate_bias.astype(F32).reshape(1, SUBLANES)
    bias_sub = bias_lane.reshape(SUBLANES, 1)

    def col(off):
        return pl.BlockSpec((None, None, chunk, width), lambda b, c, off=off: (first_part + off, b, c, 0))

    mix, c_f, n_f, m_f = pl.pallas_call(
        _mlstm_prompt_kernel,
        grid=(bsz, seq // chunk),
        in_specs=[col(0), col(1), col(2), col(3), col(4),
                  pl.BlockSpec((None, chunk, SUBLANES), lambda b, c: (b, c, 0)),
                  pl.BlockSpec((None, SUBLANES, chunk), lambda b, c: (b, 0, c)),
                  pl.BlockSpec((SUBLANES, 1), lambda b, c: (0, 0)),
                  pl.BlockSpec((1, SUBLANES), lambda b, c: (0, 0))],
        out_specs=[pl.BlockSpec((None, chunk, width), lambda b, c: (b, c, 0)),
                   pl.BlockSpec((None, n_heads, dh, dh), lambda b, c: (b, 0, 0, 0)),
                   pl.BlockSpec((None, n_heads, dh), lambda b, c: (b, 0, 0)),
                   pl.BlockSpec((None, n_heads, LANES), lambda b, c: (b, 0, 0))],
        out_shape=[jax.ShapeDtypeStruct((bsz, seq, width), BF16),
                   jax.ShapeDtypeStruct((bsz, n_heads, dh, dh), F32),
                   jax.ShapeDtypeStruct((bsz, n_heads, dh), F32),
                   jax.ShapeDtypeStruct((bsz, n_heads, LANES), F32)],
        compiler_params=_params("parallel", "arbitrary"),
        name="mlstm_prompt",
    )(z4, z4, z4, z4, z4, zg3, zg_rows, bias_sub, bias_lane)
    return mix, c_f, n_f, m_f[:, :, 0]


def _group_tokens(n_heads):
    assert LANES % n_heads == 0
    return LANES // n_heads


def _stack_groups(zt, n_grp):
    return jnp.concatenate([zt[:, g * LANES:(g + 1) * LANES] for g in range(n_grp)], axis=0)


def _unstack_groups(a, n_grp, n_heads):
    return jnp.concatenate([a[g * n_heads:(g + 1) * n_heads, :] for g in range(n_grp)], axis=1)


def _own_head_mask(n_grp, n_heads):
    r = lax.broadcasted_iota(jnp.int32, (n_grp * n_heads, LANES), 0)
    c = lax.broadcasted_iota(jnp.int32, (n_grp * n_heads, LANES), 1)
    return (r % n_heads) == (c % n_heads)


def _dil_sample_kernel(*refs, n_pat):
    q_ref, kn_ref, vn_ref, g_ref, bh_ref, bn_ref = refs[:6]
    k_refs = refs[6:6 + n_pat]
    v_refs = refs[6 + n_pat:6 + 2 * n_pat]
    o_ref = refs[6 + 2 * n_pat]
    steps, n_heads, dh = k_refs[0].shape
    n_grp = steps * n_heads // LANES
    scale = 1.0 / math.sqrt(dh)
    q = q_ref[...] * scale
    qb = q.astype(BF16)
    zn = jnp.sum(q * kn_ref[...], axis=-1, keepdims=True)
    z_new = [zn + bn_ref[g][:, 0:1] for g in range(n_pat)]
    z_hist = []
    for g in range(n_pat):
        k2 = k_refs[g][...].reshape(steps * n_heads, dh).astype(BF16)
        z_hist.append(_stack_groups(_dot_nt(qb, k2), n_grp) + bh_ref[g])
    mx = functools.reduce(jnp.maximum, z_new)
    for z in z_hist:
        zr = jnp.max(z, axis=-1, keepdims=True)
        for gi in range(n_grp):
            mx = jnp.maximum(mx, zr[gi * n_heads:(gi + 1) * n_heads, :])
    mx_rows = jnp.concatenate([mx] * n_grp, axis=0)
    den = functools.reduce(jnp.add, [jnp.exp(z - mx) for z in z_new])
    w_new = den
    w_hist = [jnp.exp(z - mx_rows) for z in z_hist]
    for w in w_hist:
        wr = jnp.sum(w, axis=-1, keepdims=True)
        for gi in range(n_grp):
            den = den + wr[gi * n_heads:(gi + 1) * n_heads, :]
    inv = 1.0 / den
    inv_rows = jnp.concatenate([inv] * n_grp, axis=0)
    acc = (w_new * inv) * vn_ref[...]
    for g in range(n_pat):
        p = _unstack_groups((w_hist[g] * inv_rows).astype(BF16), n_grp, n_heads)
        v2 = v_refs[g][...].reshape(steps * n_heads, dh).astype(BF16)
        acc = acc + _dot(p, v2)
    o_ref[...] = (acc * _silu(g_ref[...])).astype(o_ref.dtype)


def _dil_sample(zs, buf_k, buf_v, layer, rel_bias, past_len):
    _, bsz, n_hist, n_heads, dh = buf_k.shape
    width = n_heads * dh
    n_pat = len(DILATIONS)
    steps = BAND
    tok = _group_tokens(n_heads)
    n_grp = steps // tok
    bias_hist, bias_new = _sample_bias(rel_bias, n_hist, past_len)
    bh = bias_hist.reshape(n_pat, n_grp, 1, tok, n_heads)
    own = jnp.asarray(np.eye(n_heads, dtype=bool)).reshape(1, 1, n_heads, 1, n_heads)
    bh = jnp.where(own, bh, NEG).reshape(n_pat, n_grp * n_heads, LANES)
    bn = jnp.broadcast_to(bias_new[:, :, None], (n_pat, n_heads, LANES))
    z4 = zs[:, :4 * width].reshape(bsz, 4, n_heads, dh)
    k_views, v_views, kv_specs = [], [], []
    for window, dil in DILATIONS:
        assert window // dil == steps and n_hist % window == 0 and window % dil == 0
        shape = (buf_k.shape[0], bsz, n_hist // dil, dil, n_heads, dh)
        k_views.append(buf_k.reshape(shape))
        v_views.append(buf_v.reshape(shape))
        blk = n_hist // window - 1
        kv_specs.append(pl.BlockSpec((None, None, steps, None, n_heads, dh),
                                     lambda b, blk=blk: (layer, b, blk, 0, 0, 0)))

    def zrow(off):
        return pl.BlockSpec((None, None, n_heads, dh), lambda b, off=off: (b, off, 0, 0))

    return pl.pallas_call(
        functools.partial(_dil_sample_kernel, n_pat=n_pat),
        grid=(bsz,),
        in_specs=[zrow(0), zrow(1), zrow(2), zrow(3),
                  pl.BlockSpec((n_pat, n_grp * n_heads, LANES), lambda b: (0, 0, 0)),
                  pl.BlockSpec((n_pat, n_heads, LANES), lambda b: (0, 0, 0))] + kv_specs + kv_specs,
        out_specs=pl.BlockSpec((None, n_heads, dh), lambda b: (b, 0, 0)),
        out_shape=jax.ShapeDtypeStruct((bsz, n_heads, dh), BF16),
        compiler_params=_params("parallel"),
        name="dilated_attn_sample",
    )(z4, z4, z4, z4, bh, bn, *k_views, *v_views).reshape(bsz, width)


def _pool_sample_kernel(u_ref, hist_ref, gate_ref, pw_ref, ps_ref, o_ref):
    grp = pw_ref.shape[1]
    n_hist = hist_ref.shape[0]
    for g, win in enumerate(POOL_WINDOWS):
        cols = slice(g * grp, (g + 1) * grp)
        x = u_ref[:, cols]
        n_old = min(win - 1, n_hist)
        wsum = x
        for r in range(n_hist - n_old, n_hist):
            wsum = wsum + hist_ref[r, :, cols]
        pooled = wsum / float(n_old + 1) - x
        y = _dot(pooled.astype(BF16), pw_ref[g].astype(BF16)) * ps_ref[:, cols]
        o_ref[:, cols] = (y * _silu(gate_ref[:, cols])).astype(o_ref.dtype)


def _pool_sample(zs, hist, pool_w, pool_scale, layer, *, u_off, gate_off):
    bsz = zs.shape[0]
    _, n_grp, grp, _ = pool_w.shape
    width = n_grp * grp
    n_hist = hist.shape[1]
    hist = jnp.transpose(hist, (1, 0, 2))
    return pl.pallas_call(
        _pool_sample_kernel,
        grid=(1,),
        in_specs=[pl.BlockSpec((bsz, width), lambda i: (0, u_off)),
                  pl.BlockSpec((n_hist, bsz, width), lambda i: (0, 0, 0)),
                  pl.BlockSpec((bsz, width), lambda i: (0, gate_off)),
                  pl.BlockSpec((None, n_grp, grp, grp), lambda i: (layer, 0, 0, 0)),
                  pl.BlockSpec((None, 1, width), lambda i: (layer, 0, 0))],
        out_specs=pl.BlockSpec((bsz, width), lambda i: (0, 0)),
        out_shape=jax.ShapeDtypeStruct((bsz, width), BF16),
        compiler_params=_params("arbitrary"),
        name="pool_mix_sample",
    )(zs, hist, zs, pool_w, pool_scale.reshape(pool_scale.shape[0], 1, width))


SB_SAMPLE_PAGES = 4


def _sb_sample_kernel(pt_ref, q_ref, g_ref, bias_ref, ut_ref, *refs):
    k_refs = refs[:SB_SAMPLE_PAGES]
    v_refs = refs[SB_SAMPLE_PAGES:2 * SB_SAMPLE_PAGES]
    o_ref, carry_sc, acc_sc = refs[2 * SB_SAMPLE_PAGES:]
    p = pl.program_id(1)
    page, n_heads, dh = k_refs[0].shape
    n_grp = page * n_heads // LANES
    scale = 1.0 / math.sqrt(dh)
    pages = range(SB_SAMPLE_PAGES)

    @pl.when(p == 0)
    def _():
        carry_sc[...] = jnp.zeros_like(carry_sc)
        acc_sc[...] = jnp.zeros_like(acc_sc)

    qb = (q_ref[...] * scale).astype(BF16)
    own = _own_head_mask(n_grp, n_heads)
    z = [_stack_groups(_dot_nt(qb, k_refs[u][...].reshape(page * n_heads, dh).astype(BF16)), n_grp)
         + bias_ref[...] for u in pages]
    lhs, log_take = [], []
    for u in pages:
        sp = _softplus(z[u])
        hi, lo = _split_hi_lo(jnp.where(own, -sp, 0.0))
        lhs.append(jnp.concatenate([hi, lo], axis=1))
        log_take.append(z[u] - sp)
    sums = [_dot(lhs[u], ut_ref[...]) for u in pages]
    after = carry_sc[...]
    offs = []
    for u in pages:
        total = sums[u][:, LANES:]
        offs_u = [None] * n_grp
        for gi in range(n_grp - 1, -1, -1):
            offs_u[gi] = after
            after = after + total[gi * n_heads:(gi + 1) * n_heads, :]
        offs.append(jnp.concatenate(offs_u, axis=0))
    carry_sc[...] = after
    acc = acc_sc[...]
    for u in pages:
        a = jnp.where(own, jnp.exp(log_take[u] + sums[u][:, :LANES] + offs[u]), 0.0)
        v2 = v_refs[u][...].reshape(page * n_heads, dh).astype(BF16)
        acc = acc + _dot(_unstack_groups(a.astype(BF16), n_grp, n_heads), v2)
    acc_sc[...] = acc

    @pl.when(p == pl.num_programs(1) - 1)
    def _():
        o_ref[...] = (acc_sc[...] * _silu(g_ref[...])).astype(o_ref.dtype)


def _sb_sample(zs, pool_k, pool_v, layer, page_table, sb_bias):
    bsz, n_pages = page_table.shape
    _, _, page, n_heads, dh = pool_k.shape
    width = n_heads * dh
    tok = _group_tokens(n_heads)
    n_grp = page // tok
    lane_tok = np.arange(LANES) // n_heads
    lane_head = np.arange(LANES) % n_heads
    later = (lane_head[:, None] == lane_head[None, :]) & (lane_tok[:, None] > lane_tok[None, :])
    half = np.concatenate([later.astype(np.float32), np.ones((LANES, LANES), np.float32)], axis=1)
    ut = jnp.asarray(np.concatenate([half, half], axis=0), BF16)
    z4 = zs[:, :4 * width].reshape(bsz, 4, n_heads, dh)
    bias = jnp.tile(jnp.broadcast_to(sb_bias.astype(F32)[:, None], (n_heads, LANES)), (n_grp, 1))

    def page_spec(u):
        return pl.BlockSpec((None, None, page, n_heads, dh),
                            lambda b, p, pt: (layer, pt[b, n_pages - 1 - (p * SB_SAMPLE_PAGES + u)], 0, 0, 0))

    assert n_pages % SB_SAMPLE_PAGES == 0
    page_specs = [page_spec(u) for u in range(SB_SAMPLE_PAGES)]
    return pl.pallas_call(
        _sb_sample_kernel,
        grid_spec=pltpu.PrefetchScalarGridSpec(
            num_scalar_prefetch=1,
            grid=(bsz, n_pages // SB_SAMPLE_PAGES),
            in_specs=[pl.BlockSpec((None, None, n_heads, dh), lambda b, p, pt: (b, 0, 0, 0)),
                      pl.BlockSpec((None, None, n_heads, dh), lambda b, p, pt: (b, 3, 0, 0)),
                      pl.BlockSpec((n_grp * n_heads, LANES), lambda b, p, pt: (0, 0)),
                      pl.BlockSpec((2 * LANES, 2 * LANES), lambda b, p, pt: (0, 0))] + page_specs + page_specs,
            out_specs=pl.BlockSpec((None, n_heads, dh), lambda b, p, pt: (b, 0, 0)),
            scratch_shapes=[pltpu.VMEM((n_heads, LANES), F32), pltpu.VMEM((n_heads, dh), F32)]),
        out_shape=jax.ShapeDtypeStruct((bsz, n_heads, dh), BF16),
        compiler_params=_params("parallel", "arbitrary"),
        name="stick_breaking_sample",
    )(page_table, z4, z4, bias, ut, *([pool_k] * SB_SAMPLE_PAGES), *([pool_v] * SB_SAMPLE_PAGES)).reshape(bsz, width)


def _mlstm_sample_kernel(q_ref, k_ref, v_ref, o_ref, g_ref, sc_ref, c0_ref, n0_ref,
                         mix_ref, c_ref, n_ref, m_ref):
    dh = c0_ref.shape[0]
    kscale = 1.0 / math.sqrt(dh)
    s = sc_ref[...]
    ii = s[:, 0:1] + s[:, 2:3]
    lf = _log_sigmoid(s[:, 1:2] + s[:, 3:4])
    m0 = s[:, 4:5]
    m_t = jnp.maximum(lf + m0, ii)
    inter = jnp.exp(lf + m0 - m_t)
    wk = jnp.exp(ii - m_t)
    q = q_ref[...]
    ks = k_ref[...] * kscale
    v = v_ref[...]
    c0 = c0_ref[...]
    n0 = n0_ref[...]
    sc = jnp.sum(q * ks, axis=0, keepdims=True) * wk
    num = inter * jnp.sum(c0 * q, axis=0, keepdims=True) + sc * v
    den = inter * jnp.sum(q * n0, axis=0, keepdims=True) + sc
    hid = num / jnp.maximum(jnp.abs(den), jnp.exp(-m_t))
    c_ref[...] = inter * c0 + (ks * wk) * v
    n_ref[...] = inter * n0 + ks * wk
    m_ref[...] = jnp.broadcast_to(m_t, m_ref.shape)
    mix_ref[...] = (hid * jax.nn.sigmoid(o_ref[...]) * _silu(g_ref[...])).astype(mix_ref.dtype)


def _mlstm_sample(zs, zg, gate_bias, c0, n0, m0, layer, *, col0, n_heads=N_HEADS_D):
    bsz = zs.shape[0]
    dh = c0.shape[-1]
    width = n_heads * dh
    grp = zs[:, col0:col0 + 5 * width].reshape(bsz, 5, n_heads, dh)
    q_col = grp[:, 0].reshape(bsz, n_heads, dh, 1)
    k_col = grp[:, 1].reshape(bsz, n_heads, dh, 1)
    v_row = grp[:, 2].reshape(bsz, n_heads, 1, dh)
    o_row = grp[:, 3].reshape(bsz, n_heads, 1, dh)
    g_row = grp[:, 4].reshape(bsz, n_heads, 1, dh)
    gb = gate_bias.astype(F32)
    scal = jnp.stack([zg[:, :n_heads], zg[:, n_heads:2 * n_heads],
                      jnp.broadcast_to(gb[0][None], (bsz, n_heads)),
                      jnp.broadcast_to(gb[1][None], (bsz, n_heads)),
                      m0.astype(F32)], axis=-1)
    scal = jnp.pad(scal, ((0, 0), (0, 0), (0, LANES - scal.shape[-1]))).reshape(bsz, n_heads, 1, LANES)

    def spec(r, c):
        return pl.BlockSpec((None, None, r, c), lambda b, h: (b, h, 0, 0))

    mix, c_n, n_n, m_n = pl.pallas_call(
        _mlstm_sample_kernel,
        grid=(bsz, n_heads),
        in_specs=[spec(dh, 1), spec(dh, 1), spec(1, dh), spec(1, dh), spec(1, dh), spec(1, LANES),
                  pl.BlockSpec((None, None, None, dh, dh), lambda b, h: (layer, b, h, 0, 0)),
                  spec(dh, 1)],
        out_specs=[spec(1, dh), spec(dh, dh), spec(dh, 1), spec(1, LANES)],
        out_shape=[jax.ShapeDtypeStruct((bsz, n_heads, 1, dh), BF16),
                   jax.ShapeDtypeStruct((bsz, n_heads, dh, dh), F32),
                   jax.ShapeDtypeStruct((bsz, n_heads, dh, 1), F32),
                   jax.ShapeDtypeStruct((bsz, n_heads, 1, LANES), F32)],
        compiler_params=_params("parallel", "parallel"),
        name="mlstm_sample",
    )(q_col, k_col, v_row, o_row, g_row, scal, c0, n0.astype(F32).reshape(bsz, n_heads, dh, 1))
    return mix.reshape(bsz, width), c_n, n_n.reshape(bsz, n_heads, dh), m_n[:, :, 0, 0]


PROJ_TM = 2048
PROJ_TN = 512


def kernel(x_prompt, x_sample, cache_a_k, cache_a_v, state_pool, cache_c_k, cache_c_v, state_mlstm_c,
           state_mlstm_n, state_mlstm_m, page_table, rel_bias, w_in_even, pool_w, pool_scale, w_out_even,
           w_in_odd, sb_bias, mlstm_gate_bias, w_out_odd, ln_g, ln_b):
    bsz, seq, d_model = x_prompt.shape
    dbs, dec_seq, _ = x_sample.shape
    assert dec_seq == 1
    depth = ln_g.shape[0]
    alpha = (2 * depth) ** 0.25
    past_len = page_table.shape[1] * PAGE_SIZE
    mix_a = d_model // 2
    n_heads = mix_a // HEAD_DIM
    mix_b = d_model - mix_a
    n_rows = bsz * seq
    even_cols = w_in_even.shape[2]
    odd_main = 9 * mix_a
    n_gate = w_in_odd.shape[2] - odd_main
    w_in_odd_nk = jnp.swapaxes(w_in_odd, 1, 2)

    xp_f = x_prompt.reshape(n_rows, d_model)
    xp_b = xp_f.astype(BF16)
    xs_f = x_sample.reshape(dbs, d_model)
    xs_b = xs_f.astype(BF16)
    band_bias = _band_bias(rel_bias)
    assert mix_a == mix_b

    zp_even, zp_odd, zs_even, zs_odd = [], [], [], []
    mcp, mnp, mmp, mcs, mns, mms = [], [], [], [], [], []
    q_part, k_part, v_part, gate_part = 0, 1, 2, 3
    u_part, gate_b_part, mlstm_part = 4, 5, 4

    def project(w, layer_j, n_main, w_is_nk):
        zp, zs = _proj(xp_b, xs_b, w, layer_j, n_main // mix_a, mix_a, w_is_nk=w_is_nk, tm=PROJ_TM, tn=PROJ_TN)
        return zp.reshape(-1, bsz, seq, mix_a), zs, jnp.transpose(zs, (1, 0, 2)).reshape(dbs, n_main)

    for layer in range(depth):
        j = layer // 2
        if layer % 2 == 0:
            zp, zs, zsf = project(w_in_even, j, even_cols, False)
            oa = _dil_prompt(zp, band_bias, n_heads, parts=(q_part, k_part, v_part, gate_part))
            ob = _pool_prompt(zp, pool_w, pool_scale, j, u_part=u_part, gate_part=gate_b_part)
            oas = _dil_sample(zsf, cache_a_k, cache_a_v, j, rel_bias, past_len)
            obs = _pool_sample(zsf, state_pool[j], pool_w, pool_scale, j, u_off=u_part, gate_off=gate_b_part)
            yp, ys = _out_proj(oa.reshape(n_rows, mix_a), ob.reshape(n_rows, mix_b), oas, obs, w_out_even, j,
                               tm=PROJ_TM, tn=PROJ_TN)
            zp_even.append(zp)
            zs_even.append(zs)
        else:
            zp, zs, zsf = project(w_in_odd_nk, j, odd_main, True)
            zg, zgs = _gate_proj(xp_b, xs_b, w_in_odd_nk, j, odd_main, n_gate, tm=PROJ_TM)
            oc = _sb_prompt(zp, sb_bias[j], n_heads, parts=(q_part, k_part, v_part, gate_part))
            od, c_f, n_f, m_f = _mlstm_prompt(zp, zg.reshape(bsz, seq, n_gate), mlstm_gate_bias[j],
                                              first_part=mlstm_part)
            ocs = _sb_sample(zsf, cache_c_k, cache_c_v, j, page_table, sb_bias[j])
            ods, c_n, n_n, m_n = _mlstm_sample(zsf, zgs, mlstm_gate_bias[j], state_mlstm_c, state_mlstm_n[j],
                                               state_mlstm_m[j], j, col0=mlstm_part * mix_a)
            yp, ys = _out_proj(oc.reshape(n_rows, mix_a), od.reshape(n_rows, mix_b), ocs, ods, w_out_odd, j,
                               tm=PROJ_TM, tn=PROJ_TN)
            zp_odd.append(zp)
            zs_odd.append(zs)
            mcp.append(c_f)
            mnp.append(n_f)
            mmp.append(m_f)
            mcs.append(c_n)
            mns.append(n_n)
            mms.append(m_n)
        xp_f, xp_b = _residual_ln(xp_f, yp, ln_g[layer], ln_b[layer], alpha=alpha, tm=256)
        xs_f, xs_b = _residual_ln(xs_f, ys, ln_g[layer], ln_b[layer], alpha=alpha, tm=dbs)

    keep_a = min(A_WINDOW, seq)
    keep_u = min(POOL_HIST, seq)

    def heads(a):
        return a.reshape(a.shape[:-1] + (n_heads, HEAD_DIM))

    def pick(layers, part, rows=slice(None)):
        return jnp.stack([zl[part][..., rows, :] for zl in layers])

    new_row = np.newaxis
    last_a = slice(seq - keep_a, seq)
    a_k_prompt = heads(pick(zp_even, k_part, last_a))
    a_v_prompt = heads(pick(zp_even, v_part, last_a))
    pool_prompt = pick(zp_even, u_part, slice(seq - keep_u, seq))
    a_k_sample, a_v_sample = _shift_windows(cache_a_k, cache_a_v, heads(pick(zs_even, k_part))[:, :, new_row],
                                            heads(pick(zs_even, v_part))[:, :, new_row])
    pool_sample = jnp.concatenate([state_pool[:, :, 1:], pick(zs_even, u_part)[:, :, new_row]], axis=2)
    c_k_prompt = heads(pick(zp_odd, k_part))
    c_v_prompt = heads(pick(zp_odd, v_part))
    c_k_sample = heads(pick(zs_odd, k_part))[:, :, new_row]
    c_v_sample = heads(pick(zs_odd, v_part))[:, :, new_row]
    return (xp_f.reshape(bsz, seq, d_model), xs_f.reshape(dbs, 1, d_model),
            a_k_prompt, a_v_prompt, a_k_sample, a_v_sample, pool_prompt, pool_sample,
            c_k_prompt, c_v_prompt, c_k_sample, c_v_sample,
            jnp.stack(mcp), jnp.stack(mnp), jnp.stack(mmp), jnp.stack(mcs), jnp.stack(mns), jnp.stack(mms))
```

```python
import functools
import math

import numpy as np
import jax
import jax.numpy as jnp
from jax import lax
from jax.experimental import pallas as pl
from jax.experimental.pallas import tpu as pltpu

HEAD_DIM = 128
DILATIONS = ((128, 1), (512, 4), (2048, 16))
A_WINDOW = max(w for w, _ in DILATIONS)
BAND = 128
POOL_WINDOWS = (2, 4, 8, 16)
POOL_HIST = max(POOL_WINDOWS) - 1
N_HEADS_D = 4
MLSTM_CHUNK = 128
N_REL_BUCKETS = 32
REL_MAX_DISTANCE = A_WINDOW
LN_EPS = 1e-5
LOG2E = 1.0 / math.log(2.0)
PAGE_SIZE = 128
NEG = -1e30

VMEM_LIMIT = 56 * 1024 * 1024
LANES = 128
SUBLANES = 8

BF16 = jnp.bfloat16
F32 = jnp.float32


def _params(*sem):
    return pltpu.CompilerParams(dimension_semantics=sem, vmem_limit_bytes=VMEM_LIMIT)


def _silu(x):
    return x * jax.nn.sigmoid(x)


def _dot(a, b):
    return jnp.dot(a, b, preferred_element_type=F32)


def _dot_nt(a, b):
    return lax.dot_general(a, b, (((1,), (1,)), ((), ())), preferred_element_type=F32)


def _dot_tn(a, b):
    return lax.dot_general(a, b, (((0,), (0,)), ((), ())), preferred_element_type=F32)


def _split_hi_lo(x):
    hi = x.astype(BF16)
    lo = (x - hi.astype(F32)).astype(BF16)
    return hi, lo


def _softplus(z):
    return jnp.maximum(z, 0.0) + jnp.log(1.0 + jnp.exp(-jnp.abs(z)))


def _log_sigmoid(z):
    return -_softplus(-z)


BF16_ROWS = 16


def _pad_rows(xs):
    return jnp.pad(xs, ((0, -xs.shape[0] % BF16_ROWS), (0, 0)))


def _stack_rows(i, pairs, sem):
    copies = []
    for n, (x_hbm, xs_ref, cat_ref) in enumerate(pairs):
        tm = cat_ref.shape[0] - xs_ref.shape[0]
        cp = pltpu.make_async_copy(x_hbm.at[pl.ds(pl.multiple_of(i * tm, tm), tm)], cat_ref.at[pl.ds(0, tm)],
                                   sem.at[n])
        cp.start()
        copies.append(cp)
        cat_ref[tm:, :] = xs_ref[...]
    for cp in copies:
        cp.wait()


def _proj_kernel(x_hbm, xs_ref, w_ref, o_ref, os_ref, cat_ref, sem, *, w_is_nk):
    @pl.when(pl.program_id(1) == 0)
    def _():
        _stack_rows(pl.program_id(0), [(x_hbm, xs_ref, cat_ref)], sem)

    tm, ms = o_ref.shape[0], os_ref.shape[0]
    wb = w_ref[...].astype(BF16)
    res = (_dot_nt if w_is_nk else _dot)(cat_ref[...], wb)
    o_ref[...] = res[:tm]
    os_ref[...] = res[tm:tm + ms]


def _proj(x, xs, w, layer, n_cols, *, w_is_nk, tm, tn):
    m, k = x.shape
    ms = xs.shape[0]
    assert m % tm == 0 and n_cols % tn == 0 and w.shape[2 if w_is_nk else 1] == k
    xs = _pad_rows(xs)
    if w_is_nk:
        w_spec = pl.BlockSpec((None, tn, k), lambda i, j: (layer, j, 0))
    else:
        w_spec = pl.BlockSpec((None, k, tn), lambda i, j: (layer, 0, j))
    o, os = pl.pallas_call(
        functools.partial(_proj_kernel, w_is_nk=w_is_nk),
        grid=(m // tm, n_cols // tn),
        in_specs=[pl.BlockSpec(memory_space=pl.ANY),
                  pl.BlockSpec(xs.shape, lambda i, j: (0, 0)),
                  w_spec],
        out_specs=[pl.BlockSpec((tm, tn), lambda i, j: (i, j)),
                   pl.BlockSpec((None, ms, tn), lambda i, j: (i, 0, j))],
        out_shape=[jax.ShapeDtypeStruct((m, n_cols), F32), jax.ShapeDtypeStruct((m // tm, ms, n_cols), F32)],
        scratch_shapes=[pltpu.VMEM((tm + xs.shape[0], k), BF16), pltpu.SemaphoreType.DMA((1,))],
        compiler_params=_params("arbitrary", "arbitrary"),
        name="proj",
    )(x, xs, w)
    return o, os[0]


def _gate_proj_kernel(x_ref, xs_ref, w_ref, o_ref, os_ref):
    wb = w_ref[...].astype(BF16)
    o_ref[...] = _dot_nt(x_ref[...], wb)
    os_ref[...] = _dot_nt(xs_ref[...], wb)


def _gate_proj(x, xs, w_nk, layer, row0, n_gate, *, tm):
    m, k = x.shape
    ms = xs.shape[0]
    assert m % tm == 0 and row0 % n_gate == 0 and n_gate % SUBLANES == 0
    o, os = pl.pallas_call(
        _gate_proj_kernel,
        grid=(m // tm,),
        in_specs=[pl.BlockSpec((tm, k), lambda i: (i, 0)),
                  pl.BlockSpec((ms, k), lambda i: (0, 0)),
                  pl.BlockSpec((None, n_gate, k), lambda i: (layer, row0 // n_gate, 0))],
        out_specs=[pl.BlockSpec((tm, n_gate), lambda i: (i, 0)),
                   pl.BlockSpec((None, ms, n_gate), lambda i: (i, 0, 0))],
        out_shape=[jax.ShapeDtypeStruct((m, n_gate), F32), jax.ShapeDtypeStruct((m // tm, ms, n_gate), F32)],
        compiler_params=_params("arbitrary"),
        name="gate_proj",
    )(x, xs, w_nk)
    return o, os[0]


def _out_proj_kernel(xa_hbm, xb_hbm, sa_ref, sb_ref, w_ref, o_ref, os_ref, cata_ref, catb_ref, sem):
    @pl.when(pl.program_id(1) == 0)
    def _():
        _stack_rows(pl.program_id(0), [(xa_hbm, sa_ref, cata_ref), (xb_hbm, sb_ref, catb_ref)], sem)

    tm, ms = o_ref.shape[0], os_ref.shape[0]
    ka = cata_ref.shape[1]
    res = (_dot(cata_ref[...], w_ref[:ka, :].astype(BF16))
           + _dot(catb_ref[...], w_ref[ka:, :].astype(BF16)))
    o_ref[...] = res[:tm]
    os_ref[...] = res[tm:tm + ms]


def _out_proj(xa, xb, sa, sb, w, layer, *, tm, tn):
    m, ka = xa.shape
    kb = xb.shape[1]
    ms = sa.shape[0]
    n = w.shape[2]
    assert m % tm == 0 and n % tn == 0 and w.shape[1] == ka + kb
    sa, sb = _pad_rows(sa), _pad_rows(sb)
    rows = tm + sa.shape[0]
    o, os = pl.pallas_call(
        _out_proj_kernel,
        grid=(m // tm, n // tn),
        in_specs=[pl.BlockSpec(memory_space=pl.ANY),
                  pl.BlockSpec(memory_space=pl.ANY),
                  pl.BlockSpec(sa.shape, lambda i, j: (0, 0)),
                  pl.BlockSpec(sb.shape, lambda i, j: (0, 0)),
                  pl.BlockSpec((None, ka + kb, tn), lambda i, j: (layer, 0, j))],
        out_specs=[pl.BlockSpec((tm, tn), lambda i, j: (i, j)),
                   pl.BlockSpec((None, ms, tn), lambda i, j: (i, 0, j))],
        out_shape=[jax.ShapeDtypeStruct((m, n), F32), jax.ShapeDtypeStruct((m // tm, ms, n), F32)],
        scratch_shapes=[pltpu.VMEM((rows, ka), BF16), pltpu.VMEM((rows, kb), BF16), pltpu.SemaphoreType.DMA((2,))],
        compiler_params=_params("arbitrary", "arbitrary"),
        name="out_proj",
    )(xa, xb, sa, sb, w)
    return o, os[0]


SHIFT_ROWS = 256


def _shift_kernel(ck_ref, cv_ref, nk_ref, nv_ref, ok_ref, ov_ref, *, per_window):
    c = pl.program_id(0)
    rows = ok_ref.shape[0]
    last = c == pl.num_programs(0) - 1
    for src, new, dst in ((ck_ref, nk_ref, ok_ref), (cv_ref, nv_ref, ov_ref)):
        @pl.when(jnp.logical_not(last))
        def _(src=src, dst=dst):
            dst[...] = src[...]

        @pl.when(last)
        def _(src=src, dst=dst):
            dst[0:rows - 1] = src[1:rows]

        @pl.when(c % per_window == per_window - 1)
        def _(new=new, dst=dst):
            dst[rows - 1:rows] = new[...]


def _shift_windows(cache_k, cache_v, new_k, new_v):
    n_layers, bsz, n_hist, n_heads, dh = cache_k.shape
    total = n_layers * bsz * n_hist
    assert n_hist % SHIFT_ROWS == 0
    per_window = n_hist // SHIFT_ROWS
    flat = (total, n_heads, dh)
    src_spec = pl.BlockSpec((pl.Element(SHIFT_ROWS), pl.Element(n_heads), pl.Element(dh)),
                            lambda c: (jnp.minimum(c * SHIFT_ROWS + 1, total - SHIFT_ROWS), 0, 0))
    new_spec = pl.BlockSpec((None, 1, n_heads, dh), lambda c: (c // per_window, 0, 0, 0))
    dst_spec = pl.BlockSpec((SHIFT_ROWS, n_heads, dh), lambda c: (c, 0, 0))
    ok, ov = pl.pallas_call(
        functools.partial(_shift_kernel, per_window=per_window),
        grid=(total // SHIFT_ROWS,),
        in_specs=[src_spec, src_spec, new_spec, new_spec],
        out_specs=[dst_spec, dst_spec],
        out_shape=[jax.ShapeDtypeStruct(flat, cache_k.dtype), jax.ShapeDtypeStruct(flat, cache_v.dtype)],
        compiler_params=_params("arbitrary"),
        name="shift_windows",
    )(cache_k.reshape(flat), cache_v.reshape(flat),
      new_k.astype(cache_k.dtype).reshape(n_layers * bsz, 1, n_heads, dh),
      new_v.astype(cache_v.dtype).reshape(n_layers * bsz, 1, n_heads, dh))
    return ok.reshape(cache_k.shape), ov.reshape(cache_v.shape)


def _ln_kernel(x_ref, y_ref, g_ref, b_ref, of_ref, ob_ref, *, alpha):
    h = alpha * x_ref[...] + y_ref[...]
    mu = jnp.mean(h, axis=-1, keepdims=True)
    c = h - mu
    var = jnp.mean(c * c, axis=-1, keepdims=True)
    r = c * lax.rsqrt(var + LN_EPS) * g_ref[...] + b_ref[...]
    of_ref[...] = r
    ob_ref[...] = r.astype(BF16)


def _residual_ln(x, y, g, b, *, alpha, tm):
    m, d = x.shape
    tm = min(tm, m)
    assert m % tm == 0
    row = pl.BlockSpec((tm, d), lambda i: (i, 0))
    vec = pl.BlockSpec((1, d), lambda i: (0, 0))
    return pl.pallas_call(
        functools.partial(_ln_kernel, alpha=alpha),
        grid=(m // tm,),
        in_specs=[row, row, vec, vec],
        out_specs=[row, row],
        out_shape=[jax.ShapeDtypeStruct((m, d), F32), jax.ShapeDtypeStruct((m, d), BF16)],
        compiler_params=_params("parallel"),
        name="residual_ln",
    )(x, y, g.reshape(1, d), b.reshape(1, d))


def _rel_bucket_static(dist):
    exact = N_REL_BUCKETS // 2
    dist = np.asarray(dist, np.int64)
    ratio = np.log(np.maximum(dist, exact) / exact) / math.log(REL_MAX_DISTANCE / exact)
    large = np.minimum(exact + (ratio * (N_REL_BUCKETS - exact)).astype(np.int64), N_REL_BUCKETS - 1)
    return np.where(dist < exact, dist, large).astype(np.int32)


def _band_bias(rel_bias):
    n_heads = rel_bias.shape[1]
    period = 2 * BAND + 1
    out = []
    for window, dil in DILATIONS:
        steps = window // dil
        m = BAND - np.arange(period)
        valid = (m >= 0) & (m <= steps)
        tab = rel_bias[jnp.asarray(_rel_bucket_static(dil * np.clip(m, 0, steps)))].astype(F32)
        tab = jnp.where(jnp.asarray(valid)[:, None], tab, NEG).T
        rows = jnp.tile(tab, (1, BAND))[:, :BAND * 2 * BAND]
        out.append(rows.reshape(n_heads, BAND, 2 * BAND))
    return jnp.stack(out)


def _sample_bias(rel_bias, n_hist, past_len):
    hist, new = [], []
    for window, dil in DILATIONS:
        steps = window // dil
        m = steps - np.arange(steps)
        idx = n_hist - dil * m
        valid = (past_len - dil * m >= 0) & (idx >= 0)
        bias = rel_bias[jnp.asarray(_rel_bucket_static(dil * m))].astype(F32)
        hist.append(jnp.where(jnp.asarray(valid)[:, None], bias, NEG))
        new.append(rel_bias[int(_rel_bucket_static(0))].astype(F32))
    return jnp.stack(hist), jnp.stack(new)


DIL_UNROLL_FIRST = 8
DIL_UNROLL_REST = 5


def _largest_divisor(n, cap):
    return max(d for d in range(1, cap + 1) if n % d == 0)


def _dil_prompt_kernel(q_ref, k_ref, v_ref, g_ref, bm_ref, o_ref, og_ref, lse_ref, *, seq):
    scale = 1.0 / math.sqrt(HEAD_DIM)

    def rows(ref, start, n, dil):
        if dil == 1:
            return ref[pl.ds(start, n), :]
        return ref[pl.ds(start, n, stride=dil), :]

    def blocks(g, dil, bases, first):
        n = range(len(bases))
        back = 0 if first else dil * BAND
        n_keys = BAND if first else 2 * BAND
        bias = bm_ref[g, :, BAND:] if first else bm_ref[g]
        qs = [(rows(q_ref, b, BAND, dil) * scale).astype(BF16) for b in bases]
        ks = [rows(k_ref, b - back, n_keys, dil).astype(BF16) for b in bases]
        s = [_dot_nt(qs[u], ks[u]) + bias for u in n]
        mx = [jnp.max(s[u], axis=-1, keepdims=True) for u in n]
        p = [jnp.exp(s[u] - mx[u]) for u in n]
        l = [jnp.sum(p[u], axis=-1, keepdims=True) for u in n]
        vs = [rows(v_ref, b - back, n_keys, dil).astype(BF16) for b in bases]
        o = [_dot(p[u].astype(BF16), vs[u]) / l[u] for u in n]
        for u, b in enumerate(bases):
            lse = jnp.broadcast_to(mx[u] + jnp.log(l[u]), (BAND, HEAD_DIM))
            if dil == 1:
                og_ref[g, pl.ds(b, BAND), :] = o[u]
                lse_ref[g, pl.ds(b, BAND), :] = lse
            else:
                og_ref[g, pl.ds(b, BAND, stride=dil), :] = o[u]
                lse_ref[g, pl.ds(b, BAND, stride=dil), :] = lse

    def run(n_items, unroll, base_of, g, dil, first):
        main = n_items // unroll

        def body(t, carry):
            blocks(g, dil, [base_of(t * unroll + u) for u in range(unroll)], first)
            return carry

        if main:
            lax.fori_loop(0, main, body, 0)
        if n_items > main * unroll:
            blocks(g, dil, [base_of(idx) for idx in range(main * unroll, n_items)], first)

    for g, (window, dil) in enumerate(DILATIONS):
        n_blk = seq // dil // BAND
        run(dil, _largest_divisor(dil, DIL_UNROLL_FIRST), lambda r: r, g, dil, True)
        if n_blk > 1:
            def rest_base(t, dil=dil, n_blk=n_blk):
                return t // (n_blk - 1) + dil * BAND * (t % (n_blk - 1) + 1)

            n_rest = dil * (n_blk - 1)
            run(n_rest, _largest_divisor(n_rest, DIL_UNROLL_REST), rest_base, g, dil, False)

    tile = 256

    def merge(t, carry):
        sl = pl.ds(pl.multiple_of(t * tile, tile), tile)
        lses = [lse_ref[g, sl, :] for g in range(len(DILATIONS))]
        mx = functools.reduce(jnp.maximum, lses)
        ws = [jnp.exp(l - mx) for l in lses]
        den = functools.reduce(jnp.add, ws)
        num = functools.reduce(jnp.add, [w * og_ref[g, sl, :] for g, w in enumerate(ws)])
        o_ref[sl, :] = (num / den * _silu(g_ref[sl, :])).astype(o_ref.dtype)
        return carry

    lax.fori_loop(0, seq // tile, merge, 0)


def _dil_prompt(z3, band_bias, n_heads):
    bsz, seq, _ = z3.shape
    n_pat = len(DILATIONS)
    for window, dil in DILATIONS:
        assert seq % (dil * BAND) == 0 and window // dil == BAND

    def col(off):
        return pl.BlockSpec((None, seq, HEAD_DIM), lambda b, h, off=off: (b, 0, off * n_heads + h))

    return pl.pallas_call(
        functools.partial(_dil_prompt_kernel, seq=seq),
        grid=(bsz, n_heads),
        in_specs=[col(0), col(1), col(2), col(3),
                  pl.BlockSpec((n_pat, None, BAND, 2 * BAND), lambda b, h: (0, h, 0, 0))],
        out_specs=pl.BlockSpec((None, seq, HEAD_DIM), lambda b, h: (b, 0, h)),
        out_shape=jax.ShapeDtypeStruct((bsz, seq, n_heads * HEAD_DIM), BF16),
        scratch_shapes=[pltpu.VMEM((n_pat, seq, HEAD_DIM), F32), pltpu.VMEM((n_pat, seq, HEAD_DIM), F32)],
        compiler_params=_params("parallel", "parallel"),
        name="dilated_attn_prompt",
    )(z3, z3, z3, z3, band_bias)


POOL_HALO = 16


def _pool_prompt_kernel(u_ref, halo_ref, gate_ref, pw_ref, ps_ref, o_ref, *, tile):
    t = pl.program_id(1)
    grp = pw_ref.shape[1]
    halo = jnp.where(t > 0, halo_ref[...], 0.0)
    pos = (t * tile + lax.broadcasted_iota(jnp.int32, (tile, 1), 0) + 1).astype(F32)
    for g, win in enumerate(POOL_WINDOWS):
        cols = slice(g * grp, (g + 1) * grp)
        x = u_ref[:, cols]
        ext = jnp.concatenate([halo[:, cols], x], axis=0)
        shift = 1
        while shift < win:
            ext = ext + pltpu.roll(ext, shift, 0)
            shift *= 2
        wsum = ext[POOL_HALO:, :]
        pooled = wsum / jnp.minimum(pos, float(win)) - x
        y = _dot(pooled.astype(BF16), pw_ref[g].astype(BF16)) * ps_ref[:, cols]
        o_ref[:, cols] = (y * _silu(gate_ref[:, cols])).astype(o_ref.dtype)


def _pool_prompt(z3, pool_w, pool_scale, layer, *, u_off, gate_off, tile=256):
    bsz, seq, _ = z3.shape
    _, n_grp, grp, _ = pool_w.shape
    width = n_grp * grp
    assert seq % tile == 0 and tile % POOL_HALO == 0
    for win in POOL_WINDOWS:
        assert win & (win - 1) == 0 and win - 1 <= POOL_HALO
    per = tile // POOL_HALO
    return pl.pallas_call(
        functools.partial(_pool_prompt_kernel, tile=tile),
        grid=(bsz, seq // tile),
        in_specs=[pl.BlockSpec((None, tile, width), lambda b, t: (b, t, u_off)),
                  pl.BlockSpec((None, POOL_HALO, width),
                               lambda b, t: (b, jnp.maximum(t * per - 1, 0), u_off)),
                  pl.BlockSpec((None, tile, width), lambda b, t: (b, t, gate_off)),
                  pl.BlockSpec((None, n_grp, grp, grp), lambda b, t: (layer, 0, 0, 0)),
                  pl.BlockSpec((None, 1, width), lambda b, t: (layer, 0, 0))],
        out_specs=pl.BlockSpec((None, tile, width), lambda b, t: (b, t, 0)),
        out_shape=jax.ShapeDtypeStruct((bsz, seq, width), BF16),
        compiler_params=_params("parallel", "parallel"),
        name="pool_mix_prompt",
    )(z3, z3, z3, pool_w, pool_scale.reshape(pool_scale.shape[0], 1, width))


SB_TILE = 256
SB_HEADS = 4


def _suffix_matrix(n):
    j = np.arange(n)[:, None]
    s = np.arange(n)[None, :]
    return (j >= s).astype(np.float32)


def _sb_prompt_kernel(q_ref, k_ref, v_ref, g_ref, bias_ref, uu_ref, o_ref, kb_ref, vb_ref,
                      za_ref, la_ref, zb_ref, lb_ref):
    i = pl.program_id(2)
    scale = 1.0 / math.sqrt(HEAD_DIM)

    @pl.when(i == 0)
    def _():
        kb_ref[...] = k_ref[...].astype(BF16)
        vb_ref[...] = v_ref[...].astype(BF16)

    row = lax.broadcasted_iota(jnp.int32, (SB_TILE, SB_TILE), 0)
    colid = lax.broadcasted_iota(jnp.int32, (SB_TILE, SB_TILE), 1)
    heads = [slice(h * HEAD_DIM, (h + 1) * HEAD_DIM) for h in range(SB_HEADS)]
    hs_all = range(SB_HEADS)
    qs = [(q_ref[:, heads[h]] * (scale * LOG2E)).astype(BF16) for h in hs_all]
    biases = [bias_ref[h][:, 0:1] * LOG2E for h in hs_all]

    slots = ((za_ref, la_ref), (zb_ref, lb_ref))

    def front(m, slot, diag):
        z_ref, l_ref = slots[slot]
        sl = pl.ds(pl.multiple_of((i - m) * SB_TILE, SB_TILE), SB_TILE)
        z2 = [_dot_nt(qs[h], kb_ref[sl, heads[h]]) + biases[h] for h in hs_all]
        for h in hs_all:
            sp2 = jnp.maximum(z2[h], 0.0) + jnp.log(1.0 + jnp.exp2(-jnp.abs(z2[h]))) * LOG2E
            if diag:
                sp2 = jnp.where(colid < row, sp2, 0.0)
                z_ref[h] = jnp.where(colid < row, z2[h], NEG)
            else:
                z_ref[h] = z2[h]
            hi, lo = _split_hi_lo(sp2)
            l_ref[h] = jnp.concatenate([hi, lo], axis=1)

    def back(m, slot, spent, acc):
        z_ref, l_ref = slots[slot]
        sl = pl.ds(pl.multiple_of((i - m) * SB_TILE, SB_TILE), SB_TILE)
        incl = [_dot(l_ref[h], uu_ref[...]) for h in hs_all]
        a = [jnp.exp2(z_ref[h] - incl[h] - spent[h]).astype(BF16) for h in hs_all]
        acc = [acc[h] + _dot(a[h], vb_ref[sl, heads[h]]) for h in hs_all]
        spent = [spent[h] + incl[h][:, 0:1] for h in hs_all]
        return spent, acc

    def finish(acc):
        for h in hs_all:
            o_ref[:, heads[h]] = (acc[h] * _silu(g_ref[:, heads[h]])).astype(o_ref.dtype)

    front(0, 0, True)

    def pair(n, st):
        spent, acc = list(st[:SB_HEADS]), list(st[SB_HEADS:])
        front(2 * n + 1, 1, False)
        spent, acc = back(2 * n, 0, spent, acc)
        front(2 * n + 2, 0, False)
        spent, acc = back(2 * n + 1, 1, spent, acc)
        return tuple(spent) + tuple(acc)

    zero = [jnp.zeros((SB_TILE, 1), F32) for _ in hs_all] + [jnp.zeros((SB_TILE, HEAD_DIM), F32) for _ in hs_all]
    state = lax.fori_loop(0, i // 2, pair, tuple(zero))
    spent, acc = list(state[:SB_HEADS]), list(state[SB_HEADS:])
    done = 2 * (i // 2)

    @pl.when(i % 2 == 0)
    def _():
        finish(back(done, 0, spent, acc)[1])

    @pl.when(i % 2 == 1)
    def _():
        front(done + 1, 1, False)
        sp1, acc1 = back(done, 0, spent, acc)
        finish(back(done + 1, 1, sp1, acc1)[1])


def _sb_prompt(z3, sb_bias, n_heads):
    bsz, seq, _ = z3.shape
    assert seq % SB_TILE == 0 and n_heads % SB_HEADS == 0
    n_grp = n_heads // SB_HEADS
    width = SB_HEADS * HEAD_DIM
    u = _suffix_matrix(SB_TILE)
    uu = jnp.asarray(np.concatenate([u, u], axis=0), BF16)
    bias = jnp.broadcast_to(sb_bias.astype(F32)[:, None, None], (n_heads, 1, LANES))

    def qcol(off):
        return pl.BlockSpec((None, SB_TILE, width), lambda b, h, i, off=off: (b, i, off * n_grp + h))

    def kcol(off):
        return pl.BlockSpec((None, seq, width), lambda b, h, i, off=off: (b, 0, off * n_grp + h))

    return pl.pallas_call(
        _sb_prompt_kernel,
        grid=(bsz, n_grp, seq // SB_TILE),
        in_specs=[qcol(0), kcol(1), kcol(2), qcol(3),
                  pl.BlockSpec((SB_HEADS, 1, LANES), lambda b, h, i: (h, 0, 0)),
                  pl.BlockSpec((2 * SB_TILE, SB_TILE), lambda b, h, i: (0, 0))],
        out_specs=pl.BlockSpec((None, SB_TILE, width), lambda b, h, i: (b, i, h)),
        out_shape=jax.ShapeDtypeStruct((bsz, seq, n_heads * HEAD_DIM), BF16),
        scratch_shapes=[pltpu.VMEM((seq, width), BF16), pltpu.VMEM((seq, width), BF16)]
        + [pltpu.VMEM((SB_HEADS, SB_TILE, SB_TILE), F32), pltpu.VMEM((SB_HEADS, SB_TILE, 2 * SB_TILE), BF16)] * 2,
        compiler_params=_params("parallel", "parallel", "arbitrary"),
        name="stick_breaking_prompt",
    )(z3, z3, z3, z3, bias, uu)


def _mlstm_prompt_kernel(q_ref, k_ref, v_ref, o_ref, g_ref, gc_ref, gr_ref, bc_ref, br_ref,
                         mix_ref, c_ref, n_ref, m_ref):
    chunk = pl.program_id(1)
    n_heads, dh, _ = c_ref.shape
    L = q_ref.shape[0]
    kscale = 1.0 / math.sqrt(dh)

    @pl.when(chunk == 0)
    def _():
        c_ref[...] = jnp.zeros_like(c_ref)
        n_ref[...] = jnp.zeros_like(n_ref)
        m_ref[...] = jnp.zeros_like(m_ref)

    t_idx = lax.broadcasted_iota(jnp.int32, (L, L), 0)
    s_idx = lax.broadcasted_iota(jnp.int32, (L, L), 1)
    causal = s_idx <= t_idx
    gates_c = gc_ref[...] + br_ref[...]
    gates_r = gr_ref[...] + bc_ref[...]
    for h in range(n_heads):
        cols = slice(h * dh, (h + 1) * dh)
        ii_c = gates_c[:, h:h + 1]
        lf_c = _log_sigmoid(gates_c[:, n_heads + h:n_heads + h + 1])
        ii_r = gates_r[h:h + 1, :]
        lf_r = _log_sigmoid(gates_r[n_heads + h:n_heads + h + 1, :])
        b_c = jnp.sum(jnp.where(causal, lf_r, 0.0), axis=1, keepdims=True)
        b_r = jnp.sum(jnp.where(t_idx <= s_idx, lf_c, 0.0), axis=0, keepdims=True)
        g_r = ii_r - b_r
        cm_c = jnp.max(jnp.where(causal, g_r, -jnp.inf), axis=1, keepdims=True)
        m0 = m_ref[h:h + 1, 0:1]
        m_t = b_c + jnp.maximum(m0, cm_c)
        inter = jnp.exp(b_c + m0 - m_t)
        dmat = jnp.where(causal, jnp.exp((b_c - m_t) + g_r), 0.0)
        q = q_ref[:, cols].astype(BF16)
        ks = k_ref[:, cols] * kscale
        v = v_ref[:, cols].astype(BF16)
        c0 = c_ref[h]
        n0 = n_ref[h:h + 1, :]
        sc = _dot_nt(q, ks.astype(BF16)) * dmat
        num = inter * _dot(q, c0.astype(BF16)) + _dot(sc.astype(BF16), v)
        qn = jnp.sum(q_ref[:, cols] * n0, axis=1, keepdims=True)
        den = inter * qn + jnp.sum(sc, axis=1, keepdims=True)
        hid = num / jnp.maximum(jnp.abs(den), jnp.exp(-m_t))
        m_last = m_t[L - 1:L, :]
        b_last = b_c[L - 1:L, :]
        decay = jnp.exp(b_last + m0 - m_last)
        wk = jnp.exp(b_last - b_c + ii_c - m_last)
        kw = ks * wk
        c_ref[h] = decay * c0 + _dot_tn(kw.astype(BF16), v)
        n_ref[h:h + 1, :] = decay * n0 + jnp.sum(kw, axis=0, keepdims=True)
        m_ref[h:h + 1, :] = jnp.broadcast_to(m_last, (1, m_ref.shape[1]))
        out = hid * jax.nn.sigmoid(o_ref[:, cols]) * _silu(g_ref[:, cols])
        mix_ref[:, cols] = out.astype(mix_ref.dtype)


def _mlstm_prompt(z3, zg3, gate_bias, *, col0, n_heads=N_HEADS_D, chunk=MLSTM_CHUNK):
    bsz, seq, _ = z3.shape
    width = (z3.shape[2] - col0) // 5
    dh = width // n_heads
    assert seq % chunk == 0 and col0 % width == 0 and zg3.shape[2] == 2 * n_heads == SUBLANES
    base = col0 // width
    zg_rows = jnp.transpose(zg3, (0, 2, 1))
    bias_lane = gate_bias.astype(F32).reshape(1, SUBLANES)
    bias_sub = bias_lane.reshape(SUBLANES, 1)

    def col(off):
        return pl.BlockSpec((None, chunk, width), lambda b, c, off=off: (b, c, base + off))

    mix, c_f, n_f, m_f = pl.pallas_call(
        _mlstm_prompt_kernel,
        grid=(bsz, seq // chunk),
        in_specs=[col(0), col(1), col(2), col(3), col(4),
                  pl.BlockSpec((None, chunk, SUBLANES), lambda b, c: (b, c, 0)),
                  pl.BlockSpec((None, SUBLANES, chunk), lambda b, c: (b, 0, c)),
                  pl.BlockSpec((SUBLANES, 1), lambda b, c: (0, 0)),
                  pl.BlockSpec((1, SUBLANES), lambda b, c: (0, 0))],
        out_specs=[pl.BlockSpec((None, chunk, width), lambda b, c: (b, c, 0)),
                   pl.BlockSpec((None, n_heads, dh, dh), lambda b, c: (b, 0, 0, 0)),
                   pl.BlockSpec((None, n_heads, dh), lambda b, c: (b, 0, 0)),
                   pl.BlockSpec((None, n_heads, LANES), lambda b, c: (b, 0, 0))],
        out_shape=[jax.ShapeDtypeStruct((bsz, seq, width), BF16),
                   jax.ShapeDtypeStruct((bsz, n_heads, dh, dh), F32),
                   jax.ShapeDtypeStruct((bsz, n_heads, dh), F32),
                   jax.ShapeDtypeStruct((bsz, n_heads, LANES), F32)],
        compiler_params=_params("parallel", "arbitrary"),
        name="mlstm_prompt",
    )(z3, z3, z3, z3, z3, zg3, zg_rows, bias_sub, bias_lane)
    return mix, c_f, n_f, m_f[:, :, 0]


def _group_tokens(n_heads):
    assert LANES % n_heads == 0
    return LANES // n_heads


def _stack_groups(zt, n_grp):
    return jnp.concatenate([zt[:, g * LANES:(g + 1) * LANES] for g in range(n_grp)], axis=0)


def _unstack_groups(a, n_grp, n_heads):
    return jnp.concatenate([a[g * n_heads:(g + 1) * n_heads, :] for g in range(n_grp)], axis=1)


def _own_head_mask(n_grp, n_heads):
    r = lax.broadcasted_iota(jnp.int32, (n_grp * n_heads, LANES), 0)
    c = lax.broadcasted_iota(jnp.int32, (n_grp * n_heads, LANES), 1)
    return (r % n_heads) == (c % n_heads)


def _dil_sample_kernel(*refs, n_pat):
    q_ref, kn_ref, vn_ref, g_ref, bh_ref, bn_ref = refs[:6]
    k_refs = refs[6:6 + n_pat]
    v_refs = refs[6 + n_pat:6 + 2 * n_pat]
    o_ref = refs[6 + 2 * n_pat]
    steps, n_heads, dh = k_refs[0].shape
    n_grp = steps * n_heads // LANES
    scale = 1.0 / math.sqrt(dh)
    q = q_ref[...] * scale
    qb = q.astype(BF16)
    zn = jnp.sum(q * kn_ref[...], axis=-1, keepdims=True)
    z_new = [zn + bn_ref[g][:, 0:1] for g in range(n_pat)]
    z_hist = []
    for g in range(n_pat):
        k2 = k_refs[g][...].reshape(steps * n_heads, dh).astype(BF16)
        z_hist.append(_stack_groups(_dot_nt(qb, k2), n_grp) + bh_ref[g])
    mx = functools.reduce(jnp.maximum, z_new)
    for z in z_hist:
        zr = jnp.max(z, axis=-1, keepdims=True)
        for gi in range(n_grp):
            mx = jnp.maximum(mx, zr[gi * n_heads:(gi + 1) * n_heads, :])
    mx_rows = jnp.concatenate([mx] * n_grp, axis=0)
    den = functools.reduce(jnp.add, [jnp.exp(z - mx) for z in z_new])
    w_new = den
    w_hist = [jnp.exp(z - mx_rows) for z in z_hist]
    for w in w_hist:
        wr = jnp.sum(w, axis=-1, keepdims=True)
        for gi in range(n_grp):
            den = den + wr[gi * n_heads:(gi + 1) * n_heads, :]
    inv = 1.0 / den
    inv_rows = jnp.concatenate([inv] * n_grp, axis=0)
    acc = (w_new * inv) * vn_ref[...]
    for g in range(n_pat):
        p = _unstack_groups((w_hist[g] * inv_rows).astype(BF16), n_grp, n_heads)
        v2 = v_refs[g][...].reshape(steps * n_heads, dh).astype(BF16)
        acc = acc + _dot(p, v2)
    o_ref[...] = (acc * _silu(g_ref[...])).astype(o_ref.dtype)


def _dil_sample(zs, buf_k, buf_v, layer, rel_bias, past_len):
    _, bsz, n_hist, n_heads, dh = buf_k.shape
    width = n_heads * dh
    n_pat = len(DILATIONS)
    steps = BAND
    tok = _group_tokens(n_heads)
    n_grp = steps // tok
    bias_hist, bias_new = _sample_bias(rel_bias, n_hist, past_len)
    bh = bias_hist.reshape(n_pat, n_grp, 1, tok, n_heads)
    own = jnp.asarray(np.eye(n_heads, dtype=bool)).reshape(1, 1, n_heads, 1, n_heads)
    bh = jnp.where(own, bh, NEG).reshape(n_pat, n_grp * n_heads, LANES)
    bn = jnp.broadcast_to(bias_new[:, :, None], (n_pat, n_heads, LANES))
    z4 = zs[:, :4 * width].reshape(bsz, 4, n_heads, dh)
    k_views, v_views, kv_specs = [], [], []
    for window, dil in DILATIONS:
        assert window // dil == steps and n_hist % window == 0 and window % dil == 0
        shape = (buf_k.shape[0], bsz, n_hist // dil, dil, n_heads, dh)
        k_views.append(buf_k.reshape(shape))
        v_views.append(buf_v.reshape(shape))
        blk = n_hist // window - 1
        kv_specs.append(pl.BlockSpec((None, None, steps, None, n_heads, dh),
                                     lambda b, blk=blk: (layer, b, blk, 0, 0, 0)))

    def zrow(off):
        return pl.BlockSpec((None, None, n_heads, dh), lambda b, off=off: (b, off, 0, 0))

    return pl.pallas_call(
        functools.partial(_dil_sample_kernel, n_pat=n_pat),
        grid=(bsz,),
        in_specs=[zrow(0), zrow(1), zrow(2), zrow(3),
                  pl.BlockSpec((n_pat, n_grp * n_heads, LANES), lambda b: (0, 0, 0)),
                  pl.BlockSpec((n_pat, n_heads, LANES), lambda b: (0, 0, 0))] + kv_specs + kv_specs,
        out_specs=pl.BlockSpec((None, n_heads, dh), lambda b: (b, 0, 0)),
        out_shape=jax.ShapeDtypeStruct((bsz, n_heads, dh), BF16),
        compiler_params=_params("parallel"),
        name="dilated_attn_sample",
    )(z4, z4, z4, z4, bh, bn, *k_views, *v_views).reshape(bsz, width)


def _pool_sample_kernel(u_ref, hist_ref, gate_ref, pw_ref, ps_ref, o_ref):
    grp = pw_ref.shape[1]
    n_hist = hist_ref.shape[0]
    for g, win in enumerate(POOL_WINDOWS):
        cols = slice(g * grp, (g + 1) * grp)
        x = u_ref[:, cols]
        n_old = min(win - 1, n_hist)
        wsum = x
        for r in range(n_hist - n_old, n_hist):
            wsum = wsum + hist_ref[r, :, cols]
        pooled = wsum / float(n_old + 1) - x
        y = _dot(pooled.astype(BF16), pw_ref[g].astype(BF16)) * ps_ref[:, cols]
        o_ref[:, cols] = (y * _silu(gate_ref[:, cols])).astype(o_ref.dtype)


def _pool_sample(zs, hist, pool_w, pool_scale, layer, *, u_off, gate_off):
    bsz = zs.shape[0]
    _, n_grp, grp, _ = pool_w.shape
    width = n_grp * grp
    n_hist = hist.shape[1]
    hist = jnp.transpose(hist, (1, 0, 2))
    return pl.pallas_call(
        _pool_sample_kernel,
        grid=(1,),
        in_specs=[pl.BlockSpec((bsz, width), lambda i: (0, u_off)),
                  pl.BlockSpec((n_hist, bsz, width), lambda i: (0, 0, 0)),
                  pl.BlockSpec((bsz, width), lambda i: (0, gate_off)),
                  pl.BlockSpec((None, n_grp, grp, grp), lambda i: (layer, 0, 0, 0)),
                  pl.BlockSpec((None, 1, width), lambda i: (layer, 0, 0))],
        out_specs=pl.BlockSpec((bsz, width), lambda i: (0, 0)),
        out_shape=jax.ShapeDtypeStruct((bsz, width), BF16),
        compiler_params=_params("arbitrary"),
        name="pool_mix_sample",
    )(zs, hist, zs, pool_w, pool_scale.reshape(pool_scale.shape[0], 1, width))


SB_SAMPLE_PAGES = 4


def _sb_sample_kernel(pt_ref, q_ref, g_ref, bias_ref, ut_ref, *refs):
    k_refs = refs[:SB_SAMPLE_PAGES]
    v_refs = refs[SB_SAMPLE_PAGES:2 * SB_SAMPLE_PAGES]
    o_ref, carry_sc, acc_sc = refs[2 * SB_SAMPLE_PAGES:]
    p = pl.program_id(1)
    page, n_heads, dh = k_refs[0].shape
    n_grp = page * n_heads // LANES
    scale = 1.0 / math.sqrt(dh)
    pages = range(SB_SAMPLE_PAGES)

    @pl.when(p == 0)
    def _():
        carry_sc[...] = jnp.zeros_like(carry_sc)
        acc_sc[...] = jnp.zeros_like(acc_sc)

    qb = (q_ref[...] * scale).astype(BF16)
    own = _own_head_mask(n_grp, n_heads)
    z = [_stack_groups(_dot_nt(qb, k_refs[u][...].reshape(page * n_heads, dh).astype(BF16)), n_grp)
         + bias_ref[...] for u in pages]
    lhs, log_take = [], []
    for u in pages:
        sp = _softplus(z[u])
        hi, lo = _split_hi_lo(jnp.where(own, -sp, 0.0))
        lhs.append(jnp.concatenate([hi, lo], axis=1))
        log_take.append(z[u] - sp)
    sums = [_dot(lhs[u], ut_ref[...]) for u in pages]
    after = carry_sc[...]
    offs = []
    for u in pages:
        total = sums[u][:, LANES:]
        offs_u = [None] * n_grp
        for gi in range(n_grp - 1, -1, -1):
            offs_u[gi] = after
            after = after + total[gi * n_heads:(gi + 1) * n_heads, :]
        offs.append(jnp.concatenate(offs_u, axis=0))
    carry_sc[...] = after
    acc = acc_sc[...]
    for u in pages:
        a = jnp.where(own, jnp.exp(log_take[u] + sums[u][:, :LANES] + offs[u]), 0.0)
        v2 = v_refs[u][...].reshape(page * n_heads, dh).astype(BF16)
        acc = acc + _dot(_unstack_groups(a.astype(BF16), n_grp, n_heads), v2)
    acc_sc[...] = acc

    @pl.when(p == pl.num_programs(1) - 1)
    def _():
        o_ref[...] = (acc_sc[...] * _silu(g_ref[...])).astype(o_ref.dtype)


def _sb_sample(zs, pool_k, pool_v, layer, page_table, sb_bias):
    bsz, n_pages = page_table.shape
    _, _, page, n_heads, dh = pool_k.shape
    width = n_heads * dh
    tok = _group_tokens(n_heads)
    n_grp = page // tok
    lane_tok = np.arange(LANES) // n_heads
    lane_head = np.arange(LANES) % n_heads
    later = (lane_head[:, None] == lane_head[None, :]) & (lane_tok[:, None] > lane_tok[None, :])
    half = np.concatenate([later.astype(np.float32), np.ones((LANES, LANES), np.float32)], axis=1)
    ut = jnp.asarray(np.concatenate([half, half], axis=0), BF16)
    z4 = zs[:, :4 * width].reshape(bsz, 4, n_heads, dh)
    bias = jnp.tile(jnp.broadcast_to(sb_bias.astype(F32)[:, None], (n_heads, LANES)), (n_grp, 1))

    def page_spec(u):
        return pl.BlockSpec((None, None, page, n_heads, dh),
                            lambda b, p, pt: (layer, pt[b, n_pages - 1 - (p * SB_SAMPLE_PAGES + u)], 0, 0, 0))

    assert n_pages % SB_SAMPLE_PAGES == 0
    page_specs = [page_spec(u) for u in range(SB_SAMPLE_PAGES)]
    return pl.pallas_call(
        _sb_sample_kernel,
        grid_spec=pltpu.PrefetchScalarGridSpec(
            num_scalar_prefetch=1,
            grid=(bsz, n_pages // SB_SAMPLE_PAGES),
            in_specs=[pl.BlockSpec((None, None, n_heads, dh), lambda b, p, pt: (b, 0, 0, 0)),
                      pl.BlockSpec((None, None, n_heads, dh), lambda b, p, pt: (b, 3, 0, 0)),
                      pl.BlockSpec((n_grp * n_heads, LANES), lambda b, p, pt: (0, 0)),
                      pl.BlockSpec((2 * LANES, 2 * LANES), lambda b, p, pt: (0, 0))] + page_specs + page_specs,
            out_specs=pl.BlockSpec((None, n_heads, dh), lambda b, p, pt: (b, 0, 0)),
            scratch_shapes=[pltpu.VMEM((n_heads, LANES), F32), pltpu.VMEM((n_heads, dh), F32)]),
        out_shape=jax.ShapeDtypeStruct((bsz, n_heads, dh), BF16),
        compiler_params=_params("parallel", "arbitrary"),
        name="stick_breaking_sample",
    )(page_table, z4, z4, bias, ut, *([pool_k] * SB_SAMPLE_PAGES), *([pool_v] * SB_SAMPLE_PAGES)).reshape(bsz, width)


def _mlstm_sample_kernel(q_ref, k_ref, v_ref, o_ref, g_ref, sc_ref, c0_ref, n0_ref,
                         mix_ref, c_ref, n_ref, m_ref):
    dh = c0_ref.shape[0]
    kscale = 1.0 / math.sqrt(dh)
    s = sc_ref[...]
    ii = s[:, 0:1] + s[:, 2:3]
    lf = _log_sigmoid(s[:, 1:2] + s[:, 3:4])
    m0 = s[:, 4:5]
    m_t = jnp.maximum(lf + m0, ii)
    inter = jnp.exp(lf + m0 - m_t)
    wk = jnp.exp(ii - m_t)
    q = q_ref[...]
    ks = k_ref[...] * kscale
    v = v_ref[...]
    c0 = c0_ref[...]
    n0 = n0_ref[...]
    sc = jnp.sum(q * ks, axis=0, keepdims=True) * wk
    num = inter * jnp.sum(c0 * q, axis=0, keepdims=True) + sc * v
    den = inter * jnp.sum(q * n0, axis=0, keepdims=True) + sc
    hid = num / jnp.maximum(jnp.abs(den), jnp.exp(-m_t))
    c_ref[...] = inter * c0 + (ks * wk) * v
    n_ref[...] = inter * n0 + ks * wk
    m_ref[...] = jnp.broadcast_to(m_t, m_ref.shape)
    mix_ref[...] = (hid * jax.nn.sigmoid(o_ref[...]) * _silu(g_ref[...])).astype(mix_ref.dtype)


def _mlstm_sample(zs, zg, gate_bias, c0, n0, m0, layer, *, col0, n_heads=N_HEADS_D):
    bsz = zs.shape[0]
    dh = c0.shape[-1]
    width = n_heads * dh
    grp = zs[:, col0:col0 + 5 * width].reshape(bsz, 5, n_heads, dh)
    q_col = grp[:, 0].reshape(bsz, n_heads, dh, 1)
    k_col = grp[:, 1].reshape(bsz, n_heads, dh, 1)
    v_row = grp[:, 2].reshape(bsz, n_heads, 1, dh)
    o_row = grp[:, 3].reshape(bsz, n_heads, 1, dh)
    g_row = grp[:, 4].reshape(bsz, n_heads, 1, dh)
    gb = gate_bias.astype(F32)
    scal = jnp.stack([zg[:, :n_heads], zg[:, n_heads:2 * n_heads],
                      jnp.broadcast_to(gb[0][None], (bsz, n_heads)),
                      jnp.broadcast_to(gb[1][None], (bsz, n_heads)),
                      m0.astype(F32)], axis=-1)
    scal = jnp.pad(scal, ((0, 0), (0, 0), (0, LANES - scal.shape[-1]))).reshape(bsz, n_heads, 1, LANES)

    def spec(r, c):
        return pl.BlockSpec((None, None, r, c), lambda b, h: (b, h, 0, 0))

    mix, c_n, n_n, m_n = pl.pallas_call(
        _mlstm_sample_kernel,
        grid=(bsz, n_heads),
        in_specs=[spec(dh, 1), spec(dh, 1), spec(1, dh), spec(1, dh), spec(1, dh), spec(1, LANES),
                  pl.BlockSpec((None, None, None, dh, dh), lambda b, h: (layer, b, h, 0, 0)),
                  spec(dh, 1)],
        out_specs=[spec(1, dh), spec(dh, dh), spec(dh, 1), spec(1, LANES)],
        out_shape=[jax.ShapeDtypeStruct((bsz, n_heads, 1, dh), BF16),
                   jax.ShapeDtypeStruct((bsz, n_heads, dh, dh), F32),
                   jax.ShapeDtypeStruct((bsz, n_heads, dh, 1), F32),
                   jax.ShapeDtypeStruct((bsz, n_heads, 1, LANES), F32)],
        compiler_params=_params("parallel", "parallel"),
        name="mlstm_sample",
    )(q_col, k_col, v_row, o_row, g_row, scal, c0, n0.astype(F32).reshape(bsz, n_heads, dh, 1))
    return mix.reshape(bsz, width), c_n, n_n.reshape(bsz, n_heads, dh), m_n[:, :, 0, 0]


PROJ_TM = 2048
PROJ_TN = 512


def kernel(x_prompt, x_sample, cache_a_k, cache_a_v, state_pool, cache_c_k, cache_c_v, state_mlstm_c,
           state_mlstm_n, state_mlstm_m, page_table, rel_bias, w_in_even, pool_w, pool_scale, w_out_even,
           w_in_odd, sb_bias, mlstm_gate_bias, w_out_odd, ln_g, ln_b):
    bsz, seq, d_model = x_prompt.shape
    dbs, dec_seq, _ = x_sample.shape
    assert dec_seq == 1
    depth = ln_g.shape[0]
    alpha = (2 * depth) ** 0.25
    past_len = page_table.shape[1] * PAGE_SIZE
    mix_a = d_model // 2
    n_heads = mix_a // HEAD_DIM
    mix_b = d_model - mix_a
    n_rows = bsz * seq
    even_cols = w_in_even.shape[2]
    odd_main = 9 * mix_a
    n_gate = w_in_odd.shape[2] - odd_main
    w_in_odd_nk = jnp.swapaxes(w_in_odd, 1, 2)

    xp_f = x_prompt.reshape(n_rows, d_model)
    xp_b = xp_f.astype(BF16)
    xs_f = x_sample.reshape(dbs, d_model)
    xs_b = xs_f.astype(BF16)
    band_bias = _band_bias(rel_bias)
    u_blk = 4 * mix_a // mix_b

    zp_even, zp_odd, zs_even, zs_odd = [], [], [], []
    mcp, mnp, mmp, mcs, mns, mms = [], [], [], [], [], []
    for layer in range(depth):
        j = layer // 2
        if layer % 2 == 0:
            z, zs = _proj(xp_b, xs_b, w_in_even, j, even_cols, w_is_nk=False, tm=PROJ_TM, tn=PROJ_TN)
            z3 = z.reshape(bsz, seq, -1)
            oa = _dil_prompt(z3, band_bias, n_heads)
            ob = _pool_prompt(z3, pool_w, pool_scale, j, u_off=u_blk, gate_off=u_blk + 1)
            oas = _dil_sample(zs, cache_a_k, cache_a_v, j, rel_bias, past_len)
            obs = _pool_sample(zs, state_pool[j], pool_w, pool_scale, j, u_off=u_blk, gate_off=u_blk + 1)
            yp, ys = _out_proj(oa.reshape(n_rows, mix_a), ob.reshape(n_rows, mix_b), oas, obs, w_out_even, j,
                               tm=PROJ_TM, tn=PROJ_TN)
            zp_even.append(z3)
            zs_even.append(zs)
        else:
            z, zs = _proj(xp_b, xs_b, w_in_odd_nk, j, odd_main, w_is_nk=True, tm=PROJ_TM, tn=PROJ_TN)
            zg, zgs = _gate_proj(xp_b, xs_b, w_in_odd_nk, j, odd_main, n_gate, tm=PROJ_TM)
            z3 = z.reshape(bsz, seq, -1)
            oc = _sb_prompt(z3, sb_bias[j], n_heads)
            od, c_f, n_f, m_f = _mlstm_prompt(z3, zg.reshape(bsz, seq, n_gate), mlstm_gate_bias[j], col0=4 * mix_a)
            ocs = _sb_sample(zs, cache_c_k, cache_c_v, j, page_table, sb_bias[j])
            ods, c_n, n_n, m_n = _mlstm_sample(zs, zgs, mlstm_gate_bias[j], state_mlstm_c, state_mlstm_n[j],
                                               state_mlstm_m[j], j, col0=4 * mix_a)
            yp, ys = _out_proj(oc.reshape(n_rows, mix_a), od.reshape(n_rows, mix_b), ocs, ods, w_out_odd, j,
                               tm=PROJ_TM, tn=PROJ_TN)
            zp_odd.append(z3)
            zs_odd.append(zs)
            mcp.append(c_f)
            mnp.append(n_f)
            mmp.append(m_f)
            mcs.append(c_n)
            mns.append(n_n)
            mms.append(m_n)
        xp_f, xp_b = _residual_ln(xp_f, yp, ln_g[layer], ln_b[layer], alpha=alpha, tm=256)
        xs_f, xs_b = _residual_ln(xs_f, ys, ln_g[layer], ln_b[layer], alpha=alpha, tm=dbs)

    keep_a = min(A_WINDOW, seq)
    keep_u = min(POOL_HIST, seq)

    def heads(a):
        return a.reshape(a.shape[:-1] + (n_heads, HEAD_DIM))

    def pick(zs, rows, lo, hi):
        return jnp.stack([z[..., rows, lo:hi] for z in zs])

    every = slice(None)
    new_row = np.newaxis
    a_k_prompt = heads(pick(zp_even, slice(seq - keep_a, seq), mix_a, 2 * mix_a))
    a_v_prompt = heads(pick(zp_even, slice(seq - keep_a, seq), 2 * mix_a, 3 * mix_a))
    pool_prompt = pick(zp_even, slice(seq - keep_u, seq), 4 * mix_a, 4 * mix_a + mix_b)
    a_k_sample, a_v_sample = _shift_windows(
        cache_a_k, cache_a_v, heads(pick(zs_even, every, mix_a, 2 * mix_a))[:, :, new_row],
        heads(pick(zs_even, every, 2 * mix_a, 3 * mix_a))[:, :, new_row])
    pool_sample = jnp.concatenate([state_pool[:, :, 1:],
                                   pick(zs_even, every, 4 * mix_a, 4 * mix_a + mix_b)[:, :, new_row]], axis=2)
    c_k_prompt = heads(pick(zp_odd, every, mix_a, 2 * mix_a))
    c_v_prompt = heads(pick(zp_odd, every, 2 * mix_a, 3 * mix_a))
    c_k_sample = heads(pick(zs_odd, every, mix_a, 2 * mix_a))[:, :, new_row]
    c_v_sample = heads(pick(zs_odd, every, 2 * mix_a, 3 * mix_a))[:, :, new_row]
    return (xp_f.reshape(bsz, seq, d_model), xs_f.reshape(dbs, 1, d_model),
            a_k_prompt, a_v_prompt, a_k_sample, a_v_sample, pool_prompt, pool_sample,
            c_k_prompt, c_v_prompt, c_k_sample, c_v_sample,
            jnp.stack(mcp), jnp.stack(mnp), jnp.stack(mmp), jnp.stack(mcs), jnp.stack(mns), jnp.stack(mms))
```

```python
import functools
import math

import numpy as np
import jax
import jax.numpy as jnp
from jax import lax
from jax.experimental import pallas as pl
from jax.experimental.pallas import tpu as pltpu

HEAD_DIM = 128
DILATIONS = ((128, 1), (512, 4), (2048, 16))
A_WINDOW = max(w for w, _ in DILATIONS)
BAND = 128
POOL_WINDOWS = (2, 4, 8, 16)
POOL_HIST = max(POOL_WINDOWS) - 1
N_HEADS_D = 4
MLSTM_CHUNK = 128
N_REL_BUCKETS = 32
REL_MAX_DISTANCE = A_WINDOW
LN_EPS = 1e-5
LOG2E = 1.0 / math.log(2.0)
PAGE_SIZE = 128
NEG = -1e30

VMEM_LIMIT = 56 * 1024 * 1024
LANES = 128
SUBLANES = 8

BF16 = jnp.bfloat16
F32 = jnp.float32


def _params(*sem):
    return pltpu.CompilerParams(dimension_semantics=sem, vmem_limit_bytes=VMEM_LIMIT)


def _silu(x):
    return x * jax.nn.sigmoid(x)


def _dot(a, b):
    return jnp.dot(a, b, preferred_element_type=F32)


def _dot_nt(a, b):
    return lax.dot_general(a, b, (((1,), (1,)), ((), ())), preferred_element_type=F32)


def _dot_tn(a, b):
    return lax.dot_general(a, b, (((0,), (0,)), ((), ())), preferred_element_type=F32)


def _split_hi_lo(x):
    hi = x.astype(BF16)
    lo = (x - hi.astype(F32)).astype(BF16)
    return hi, lo


def _softplus(z):
    return jnp.maximum(z, 0.0) + jnp.log(1.0 + jnp.exp(-jnp.abs(z)))


def _log_sigmoid(z):
    return -_softplus(-z)


BF16_ROWS = 16


def _pad_rows(xs):
    return jnp.pad(xs, ((0, -xs.shape[0] % BF16_ROWS), (0, 0)))


def _stack_rows(i, pairs, sem):
    copies = []
    for n, (x_hbm, xs_ref, cat_ref) in enumerate(pairs):
        tm = cat_ref.shape[0] - xs_ref.shape[0]
        cp = pltpu.make_async_copy(x_hbm.at[pl.ds(pl.multiple_of(i * tm, tm), tm)], cat_ref.at[pl.ds(0, tm)],
                                   sem.at[n])
        cp.start()
        copies.append(cp)
        cat_ref[tm:, :] = xs_ref[...]
    for cp in copies:
        cp.wait()


def _proj_kernel(x_hbm, xs_ref, w_ref, o_ref, os_ref, cat_ref, sem, *, w_is_nk):
    @pl.when(pl.program_id(1) == 0)
    def _():
        _stack_rows(pl.program_id(0), [(x_hbm, xs_ref, cat_ref)], sem)

    tm, ms = o_ref.shape[0], os_ref.shape[0]
    wb = w_ref[...].astype(BF16)
    res = (_dot_nt if w_is_nk else _dot)(cat_ref[...], wb)
    o_ref[...] = res[:tm]
    os_ref[...] = res[tm:tm + ms]


def _proj(x, xs, w, layer, n_cols, *, w_is_nk, tm, tn):
    m, k = x.shape
    ms = xs.shape[0]
    assert m % tm == 0 and n_cols % tn == 0 and w.shape[2 if w_is_nk else 1] == k
    xs = _pad_rows(xs)
    if w_is_nk:
        w_spec = pl.BlockSpec((None, tn, k), lambda i, j: (layer, j, 0))
    else:
        w_spec = pl.BlockSpec((None, k, tn), lambda i, j: (layer, 0, j))
    o, os = pl.pallas_call(
        functools.partial(_proj_kernel, w_is_nk=w_is_nk),
        grid=(m // tm, n_cols // tn),
        in_specs=[pl.BlockSpec(memory_space=pl.ANY),
                  pl.BlockSpec(xs.shape, lambda i, j: (0, 0)),
                  w_spec],
        out_specs=[pl.BlockSpec((tm, tn), lambda i, j: (i, j)),
                   pl.BlockSpec((None, ms, tn), lambda i, j: (i, 0, j))],
        out_shape=[jax.ShapeDtypeStruct((m, n_cols), F32), jax.ShapeDtypeStruct((m // tm, ms, n_cols), F32)],
        scratch_shapes=[pltpu.VMEM((tm + xs.shape[0], k), BF16), pltpu.SemaphoreType.DMA((1,))],
        compiler_params=_params("arbitrary", "arbitrary"),
        name="proj",
    )(x, xs, w)
    return o, os[0]


def _gate_proj_kernel(x_ref, xs_ref, w_ref, o_ref, os_ref):
    wb = w_ref[...].astype(BF16)
    o_ref[...] = _dot_nt(x_ref[...], wb)
    os_ref[...] = _dot_nt(xs_ref[...], wb)


def _gate_proj(x, xs, w_nk, layer, row0, n_gate, *, tm):
    m, k = x.shape
    ms = xs.shape[0]
    assert m % tm == 0 and row0 % n_gate == 0 and n_gate % SUBLANES == 0
    o, os = pl.pallas_call(
        _gate_proj_kernel,
        grid=(m // tm,),
        in_specs=[pl.BlockSpec((tm, k), lambda i: (i, 0)),
                  pl.BlockSpec((ms, k), lambda i: (0, 0)),
                  pl.BlockSpec((None, n_gate, k), lambda i: (layer, row0 // n_gate, 0))],
        out_specs=[pl.BlockSpec((tm, n_gate), lambda i: (i, 0)),
                   pl.BlockSpec((None, ms, n_gate), lambda i: (i, 0, 0))],
        out_shape=[jax.ShapeDtypeStruct((m, n_gate), F32), jax.ShapeDtypeStruct((m // tm, ms, n_gate), F32)],
        compiler_params=_params("arbitrary"),
        name="gate_proj",
    )(x, xs, w_nk)
    return o, os[0]


def _out_proj_kernel(xa_hbm, xb_hbm, sa_ref, sb_ref, w_ref, o_ref, os_ref, cata_ref, catb_ref, sem):
    @pl.when(pl.program_id(1) == 0)
    def _():
        _stack_rows(pl.program_id(0), [(xa_hbm, sa_ref, cata_ref), (xb_hbm, sb_ref, catb_ref)], sem)

    tm, ms = o_ref.shape[0], os_ref.shape[0]
    ka = cata_ref.shape[1]
    res = (_dot(cata_ref[...], w_ref[:ka, :].astype(BF16))
           + _dot(catb_ref[...], w_ref[ka:, :].astype(BF16)))
    o_ref[...] = res[:tm]
    os_ref[...] = res[tm:tm + ms]


def _out_proj(xa, xb, sa, sb, w, layer, *, tm, tn):
    m, ka = xa.shape
    kb = xb.shape[1]
    ms = sa.shape[0]
    n = w.shape[2]
    assert m % tm == 0 and n % tn == 0 and w.shape[1] == ka + kb
    sa, sb = _pad_rows(sa), _pad_rows(sb)
    rows = tm + sa.shape[0]
    o, os = pl.pallas_call(
        _out_proj_kernel,
        grid=(m // tm, n // tn),
        in_specs=[pl.BlockSpec(memory_space=pl.ANY),
                  pl.BlockSpec(memory_space=pl.ANY),
                  pl.BlockSpec(sa.shape, lambda i, j: (0, 0)),
                  pl.BlockSpec(sb.shape, lambda i, j: (0, 0)),
                  pl.BlockSpec((None, ka + kb, tn), lambda i, j: (layer, 0, j))],
        out_specs=[pl.BlockSpec((tm, tn), lambda i, j: (i, j)),
                   pl.BlockSpec((None, ms, tn), lambda i, j: (i, 0, j))],
        out_shape=[jax.ShapeDtypeStruct((m, n), F32), jax.ShapeDtypeStruct((m // tm, ms, n), F32)],
        scratch_shapes=[pltpu.VMEM((rows, ka), BF16), pltpu.VMEM((rows, kb), BF16), pltpu.SemaphoreType.DMA((2,))],
        compiler_params=_params("arbitrary", "arbitrary"),
        name="out_proj",
    )(xa, xb, sa, sb, w)
    return o, os[0]


SHIFT_ROWS = 256


def _shift_kernel(ck_ref, cv_ref, nk_ref, nv_ref, ok_ref, ov_ref, *, per_window):
    c = pl.program_id(0)
    rows = ok_ref.shape[0]
    last = c == pl.num_programs(0) - 1
    for src, new, dst in ((ck_ref, nk_ref, ok_ref), (cv_ref, nv_ref, ov_ref)):
        @pl.when(jnp.logical_not(last))
        def _(src=src, dst=dst):
            dst[...] = src[...]

        @pl.when(last)
        def _(src=src, dst=dst):
            dst[0:rows - 1] = src[1:rows]

        @pl.when(c % per_window == per_window - 1)
        def _(new=new, dst=dst):
            dst[rows - 1:rows] = new[...]


def _shift_windows(cache_k, cache_v, new_k, new_v):
    n_layers, bsz, n_hist, n_heads, dh = cache_k.shape
    total = n_layers * bsz * n_hist
    assert n_hist % SHIFT_ROWS == 0
    per_window = n_hist // SHIFT_ROWS
    flat = (total, n_heads, dh)
    src_spec = pl.BlockSpec((pl.Element(SHIFT_ROWS), pl.Element(n_heads), pl.Element(dh)),
                            lambda c: (jnp.minimum(c * SHIFT_ROWS + 1, total - SHIFT_ROWS), 0, 0))
    new_spec = pl.BlockSpec((None, 1, n_heads, dh), lambda c: (c // per_window, 0, 0, 0))
    dst_spec = pl.BlockSpec((SHIFT_ROWS, n_heads, dh), lambda c: (c, 0, 0))
    ok, ov = pl.pallas_call(
        functools.partial(_shift_kernel, per_window=per_window),
        grid=(total // SHIFT_ROWS,),
        in_specs=[src_spec, src_spec, new_spec, new_spec],
        out_specs=[dst_spec, dst_spec],
        out_shape=[jax.ShapeDtypeStruct(flat, cache_k.dtype), jax.ShapeDtypeStruct(flat, cache_v.dtype)],
        compiler_params=_params("arbitrary"),
        name="shift_windows",
    )(cache_k.reshape(flat), cache_v.reshape(flat),
      new_k.astype(cache_k.dtype).reshape(n_layers * bsz, 1, n_heads, dh),
      new_v.astype(cache_v.dtype).reshape(n_layers * bsz, 1, n_heads, dh))
    return ok.reshape(cache_k.shape), ov.reshape(cache_v.shape)


def _ln_kernel(x_ref, y_ref, g_ref, b_ref, of_ref, ob_ref, *, alpha):
    h = alpha * x_ref[...] + y_ref[...]
    mu = jnp.mean(h, axis=-1, keepdims=True)
    c = h - mu
    var = jnp.mean(c * c, axis=-1, keepdims=True)
    r = c * lax.rsqrt(var + LN_EPS) * g_ref[...] + b_ref[...]
    of_ref[...] = r
    ob_ref[...] = r.astype(BF16)


def _residual_ln(x, y, g, b, *, alpha, tm):
    m, d = x.shape
    tm = min(tm, m)
    assert m % tm == 0
    row = pl.BlockSpec((tm, d), lambda i: (i, 0))
    vec = pl.BlockSpec((1, d), lambda i: (0, 0))
    return pl.pallas_call(
        functools.partial(_ln_kernel, alpha=alpha),
        grid=(m // tm,),
        in_specs=[row, row, vec, vec],
        out_specs=[row, row],
        out_shape=[jax.ShapeDtypeStruct((m, d), F32), jax.ShapeDtypeStruct((m, d), BF16)],
        compiler_params=_params("parallel"),
        name="residual_ln",
    )(x, y, g.reshape(1, d), b.reshape(1, d))


def _rel_bucket_static(dist):
    exact = N_REL_BUCKETS // 2
    dist = np.asarray(dist, np.int64)
    ratio = np.log(np.maximum(dist, exact) / exact) / math.log(REL_MAX_DISTANCE / exact)
    large = np.minimum(exact + (ratio * (N_REL_BUCKETS - exact)).astype(np.int64), N_REL_BUCKETS - 1)
    return np.where(dist < exact, dist, large).astype(np.int32)


def _band_bias(rel_bias):
    n_heads = rel_bias.shape[1]
    period = 2 * BAND + 1
    out = []
    for window, dil in DILATIONS:
        steps = window // dil
        m = BAND - np.arange(period)
        valid = (m >= 0) & (m <= steps)
        tab = rel_bias[jnp.asarray(_rel_bucket_static(dil * np.clip(m, 0, steps)))].astype(F32)
        tab = jnp.where(jnp.asarray(valid)[:, None], tab, NEG).T
        rows = jnp.tile(tab, (1, BAND))[:, :BAND * 2 * BAND]
        out.append(rows.reshape(n_heads, BAND, 2 * BAND))
    return jnp.stack(out)


def _sample_bias(rel_bias, n_hist, past_len):
    hist, new = [], []
    for window, dil in DILATIONS:
        steps = window // dil
        m = steps - np.arange(steps)
        idx = n_hist - dil * m
        valid = (past_len - dil * m >= 0) & (idx >= 0)
        bias = rel_bias[jnp.asarray(_rel_bucket_static(dil * m))].astype(F32)
        hist.append(jnp.where(jnp.asarray(valid)[:, None], bias, NEG))
        new.append(rel_bias[int(_rel_bucket_static(0))].astype(F32))
    return jnp.stack(hist), jnp.stack(new)


DIL_UNROLL_FIRST = 8
DIL_UNROLL_REST = 5


def _largest_divisor(n, cap):
    return max(d for d in range(1, cap + 1) if n % d == 0)


def _dil_prompt_kernel(q_ref, k_ref, v_ref, g_ref, bm_ref, o_ref, og_ref, lse_ref, *, seq):
    scale = 1.0 / math.sqrt(HEAD_DIM)

    def rows(ref, start, n, dil):
        if dil == 1:
            return ref[pl.ds(start, n), :]
        return ref[pl.ds(start, n, stride=dil), :]

    def blocks(g, dil, bases, first):
        n = range(len(bases))
        back = 0 if first else dil * BAND
        n_keys = BAND if first else 2 * BAND
        bias = bm_ref[g, :, BAND:] if first else bm_ref[g]
        qs = [(rows(q_ref, b, BAND, dil) * scale).astype(BF16) for b in bases]
        ks = [rows(k_ref, b - back, n_keys, dil).astype(BF16) for b in bases]
        s = [_dot_nt(qs[u], ks[u]) + bias for u in n]
        mx = [jnp.max(s[u], axis=-1, keepdims=True) for u in n]
        p = [jnp.exp(s[u] - mx[u]) for u in n]
        l = [jnp.sum(p[u], axis=-1, keepdims=True) for u in n]
        vs = [rows(v_ref, b - back, n_keys, dil).astype(BF16) for b in bases]
        o = [_dot(p[u].astype(BF16), vs[u]) / l[u] for u in n]
        for u, b in enumerate(bases):
            lse = jnp.broadcast_to(mx[u] + jnp.log(l[u]), (BAND, HEAD_DIM))
            if dil == 1:
                og_ref[g, pl.ds(b, BAND), :] = o[u]
                lse_ref[g, pl.ds(b, BAND), :] = lse
            else:
                og_ref[g, pl.ds(b, BAND, stride=dil), :] = o[u]
                lse_ref[g, pl.ds(b, BAND, stride=dil), :] = lse

    def run(n_items, unroll, base_of, g, dil, first):
        main = n_items // unroll

        def body(t, carry):
            blocks(g, dil, [base_of(t * unroll + u) for u in range(unroll)], first)
            return carry

        if main:
            lax.fori_loop(0, main, body, 0)
        if n_items > main * unroll:
            blocks(g, dil, [base_of(idx) for idx in range(main * unroll, n_items)], first)

    for g, (window, dil) in enumerate(DILATIONS):
        n_blk = seq // dil // BAND
        run(dil, _largest_divisor(dil, DIL_UNROLL_FIRST), lambda r: r, g, dil, True)
        if n_blk > 1:
            def rest_base(t, dil=dil, n_blk=n_blk):
                return t // (n_blk - 1) + dil * BAND * (t % (n_blk - 1) + 1)

            n_rest = dil * (n_blk - 1)
            run(n_rest, _largest_divisor(n_rest, DIL_UNROLL_REST), rest_base, g, dil, False)

    tile = 256

    def merge(t, carry):
        sl = pl.ds(pl.multiple_of(t * tile, tile), tile)
        lses = [lse_ref[g, sl, :] for g in range(len(DILATIONS))]
        mx = functools.reduce(jnp.maximum, lses)
        ws = [jnp.exp(l - mx) for l in lses]
        den = functools.reduce(jnp.add, ws)
        num = functools.reduce(jnp.add, [w * og_ref[g, sl, :] for g, w in enumerate(ws)])
        o_ref[sl, :] = (num / den * _silu(g_ref[sl, :])).astype(o_ref.dtype)
        return carry

    lax.fori_loop(0, seq // tile, merge, 0)


def _dil_prompt(z3, band_bias, n_heads):
    bsz, seq, _ = z3.shape
    n_pat = len(DILATIONS)
    for window, dil in DILATIONS:
        assert seq % (dil * BAND) == 0 and window // dil == BAND

    def col(off):
        return pl.BlockSpec((None, seq, HEAD_DIM), lambda b, h, off=off: (b, 0, off * n_heads + h))

    return pl.pallas_call(
        functools.partial(_dil_prompt_kernel, seq=seq),
        grid=(bsz, n_heads),
        in_specs=[col(0), col(1), col(2), col(3),
                  pl.BlockSpec((n_pat, None, BAND, 2 * BAND), lambda b, h: (0, h, 0, 0))],
        out_specs=pl.BlockSpec((None, seq, HEAD_DIM), lambda b, h: (b, 0, h)),
        out_shape=jax.ShapeDtypeStruct((bsz, seq, n_heads * HEAD_DIM), BF16),
        scratch_shapes=[pltpu.VMEM((n_pat, seq, HEAD_DIM), F32), pltpu.VMEM((n_pat, seq, HEAD_DIM), F32)],
        compiler_params=_params("parallel", "parallel"),
        name="dilated_attn_prompt",
    )(z3, z3, z3, z3, band_bias)


POOL_HALO = 16


def _pool_prompt_kernel(u_ref, halo_ref, gate_ref, pw_ref, ps_ref, o_ref, *, tile):
    t = pl.program_id(1)
    grp = pw_ref.shape[1]
    halo = jnp.where(t > 0, halo_ref[...], 0.0)
    pos = (t * tile + lax.broadcasted_iota(jnp.int32, (tile, 1), 0) + 1).astype(F32)
    for g, win in enumerate(POOL_WINDOWS):
        cols = slice(g * grp, (g + 1) * grp)
        x = u_ref[:, cols]
        ext = jnp.concatenate([halo[:, cols], x], axis=0)
        shift = 1
        while shift < win:
            ext = ext + pltpu.roll(ext, shift, 0)
            shift *= 2
        wsum = ext[POOL_HALO:, :]
        pooled = wsum / jnp.minimum(pos, float(win)) - x
        y = _dot(pooled.astype(BF16), pw_ref[g].astype(BF16)) * ps_ref[:, cols]
        o_ref[:, cols] = (y * _silu(gate_ref[:, cols])).astype(o_ref.dtype)


def _pool_prompt(z3, pool_w, pool_scale, layer, *, u_off, gate_off, tile=256):
    bsz, seq, _ = z3.shape
    _, n_grp, grp, _ = pool_w.shape
    width = n_grp * grp
    assert seq % tile == 0 and tile % POOL_HALO == 0
    for win in POOL_WINDOWS:
        assert win & (win - 1) == 0 and win - 1 <= POOL_HALO
    per = tile // POOL_HALO
    return pl.pallas_call(
        functools.partial(_pool_prompt_kernel, tile=tile),
        grid=(bsz, seq // tile),
        in_specs=[pl.BlockSpec((None, tile, width), lambda b, t: (b, t, u_off)),
                  pl.BlockSpec((None, POOL_HALO, width),
                               lambda b, t: (b, jnp.maximum(t * per - 1, 0), u_off)),
                  pl.BlockSpec((None, tile, width), lambda b, t: (b, t, gate_off)),
                  pl.BlockSpec((None, n_grp, grp, grp), lambda b, t: (layer, 0, 0, 0)),
                  pl.BlockSpec((None, 1, width), lambda b, t: (layer, 0, 0))],
        out_specs=pl.BlockSpec((None, tile, width), lambda b, t: (b, t, 0)),
        out_shape=jax.ShapeDtypeStruct((bsz, seq, width), BF16),
        compiler_params=_params("parallel", "parallel"),
        name="pool_mix_prompt",
    )(z3, z3, z3, pool_w, pool_scale.reshape(pool_scale.shape[0], 1, width))


SB_TILE = 256
SB_HEADS = 4


def _suffix_matrix(n):
    j = np.arange(n)[:, None]
    s = np.arange(n)[None, :]
    return (j >= s).astype(np.float32)


def _sb_prompt_kernel(q_ref, k_ref, v_ref, g_ref, bias_ref, uu_ref, o_ref, kb_ref, vb_ref,
                      za_ref, la_ref, zb_ref, lb_ref):
    i = pl.program_id(2)
    scale = 1.0 / math.sqrt(HEAD_DIM)

    @pl.when(i == 0)
    def _():
        kb_ref[...] = k_ref[...].astype(BF16)
        vb_ref[...] = v_ref[...].astype(BF16)

    row = lax.broadcasted_iota(jnp.int32, (SB_TILE, SB_TILE), 0)
    colid = lax.broadcasted_iota(jnp.int32, (SB_TILE, SB_TILE), 1)
    heads = [slice(h * HEAD_DIM, (h + 1) * HEAD_DIM) for h in range(SB_HEADS)]
    hs_all = range(SB_HEADS)
    qs = [(q_ref[:, heads[h]] * (scale * LOG2E)).astype(BF16) for h in hs_all]
    biases = [bias_ref[h][:, 0:1] * LOG2E for h in hs_all]

    slots = ((za_ref, la_ref), (zb_ref, lb_ref))

    def front(m, slot, diag):
        z_ref, l_ref = slots[slot]
        sl = pl.ds(pl.multiple_of((i - m) * SB_TILE, SB_TILE), SB_TILE)
        z2 = [_dot_nt(qs[h], kb_ref[sl, heads[h]]) + biases[h] for h in hs_all]
        for h in hs_all:
            sp2 = jnp.maximum(z2[h], 0.0) + jnp.log(1.0 + jnp.exp2(-jnp.abs(z2[h]))) * LOG2E
            if diag:
                sp2 = jnp.where(colid < row, sp2, 0.0)
                z_ref[h] = jnp.where(colid < row, z2[h], NEG)
            else:
                z_ref[h] = z2[h]
            l_ref[h] = sp2.astype(BF16)

    def back(m, slot, spent, acc):
        z_ref, l_ref = slots[slot]
        sl = pl.ds(pl.multiple_of((i - m) * SB_TILE, SB_TILE), SB_TILE)
        incl = [_dot(l_ref[h], uu_ref[...]) for h in hs_all]
        a = [jnp.exp2(z_ref[h] - incl[h] - spent[h]).astype(BF16) for h in hs_all]
        acc = [acc[h] + _dot(a[h], vb_ref[sl, heads[h]]) for h in hs_all]
        spent = [spent[h] + incl[h][:, 0:1] for h in hs_all]
        return spent, acc

    def finish(acc):
        for h in hs_all:
            o_ref[:, heads[h]] = (acc[h] * _silu(g_ref[:, heads[h]])).astype(o_ref.dtype)

    front(0, 0, True)

    def pair(n, st):
        spent, acc = list(st[:SB_HEADS]), list(st[SB_HEADS:])
        front(2 * n + 1, 1, False)
        spent, acc = back(2 * n, 0, spent, acc)
        front(2 * n + 2, 0, False)
        spent, acc = back(2 * n + 1, 1, spent, acc)
        return tuple(spent) + tuple(acc)

    zero = [jnp.zeros((SB_TILE, 1), F32) for _ in hs_all] + [jnp.zeros((SB_TILE, HEAD_DIM), F32) for _ in hs_all]
    state = lax.fori_loop(0, i // 2, pair, tuple(zero))
    spent, acc = list(state[:SB_HEADS]), list(state[SB_HEADS:])
    done = 2 * (i // 2)

    @pl.when(i % 2 == 0)
    def _():
        finish(back(done, 0, spent, acc)[1])

    @pl.when(i % 2 == 1)
    def _():
        front(done + 1, 1, False)
        sp1, acc1 = back(done, 0, spent, acc)
        finish(back(done + 1, 1, sp1, acc1)[1])


def _sb_prompt(z3, sb_bias, n_heads):
    bsz, seq, _ = z3.shape
    assert seq % SB_TILE == 0 and n_heads % SB_HEADS == 0
    n_grp = n_heads // SB_HEADS
    width = SB_HEADS * HEAD_DIM
    u = _suffix_matrix(SB_TILE)
    uu = jnp.asarray(u, BF16)
    bias = jnp.broadcast_to(sb_bias.astype(F32)[:, None, None], (n_heads, 1, LANES))

    def qcol(off):
        return pl.BlockSpec((None, SB_TILE, width), lambda b, h, i, off=off: (b, i, off * n_grp + h))

    def kcol(off):
        return pl.BlockSpec((None, seq, width), lambda b, h, i, off=off: (b, 0, off * n_grp + h))

    return pl.pallas_call(
        _sb_prompt_kernel,
        grid=(bsz, n_grp, seq // SB_TILE),
        in_specs=[qcol(0), kcol(1), kcol(2), qcol(3),
                  pl.BlockSpec((SB_HEADS, 1, LANES), lambda b, h, i: (h, 0, 0)),
                  pl.BlockSpec((SB_TILE, SB_TILE), lambda b, h, i: (0, 0))],
        out_specs=pl.BlockSpec((None, SB_TILE, width), lambda b, h, i: (b, i, h)),
        out_shape=jax.ShapeDtypeStruct((bsz, seq, n_heads * HEAD_DIM), BF16),
        scratch_shapes=[pltpu.VMEM((seq, width), BF16), pltpu.VMEM((seq, width), BF16)]
        + [pltpu.VMEM((SB_HEADS, SB_TILE, SB_TILE), F32), pltpu.VMEM((SB_HEADS, SB_TILE, SB_TILE), BF16)] * 2,
        compiler_params=_params("parallel", "parallel", "arbitrary"),
        name="stick_breaking_prompt",
    )(z3, z3, z3, z3, bias, uu)


def _mlstm_prompt_kernel(q_ref, k_ref, v_ref, o_ref, g_ref, gc_ref, gr_ref, bc_ref, br_ref,
                         mix_ref, c_ref, n_ref, m_ref):
    chunk = pl.program_id(1)
    n_heads, dh, _ = c_ref.shape
    L = q_ref.shape[0]
    kscale = 1.0 / math.sqrt(dh)

    @pl.when(chunk == 0)
    def _():
        c_ref[...] = jnp.zeros_like(c_ref)
        n_ref[...] = jnp.zeros_like(n_ref)
        m_ref[...] = jnp.zeros_like(m_ref)

    t_idx = lax.broadcasted_iota(jnp.int32, (L, L), 0)
    s_idx = lax.broadcasted_iota(jnp.int32, (L, L), 1)
    causal = s_idx <= t_idx
    gates_c = gc_ref[...] + br_ref[...]
    gates_r = gr_ref[...] + bc_ref[...]
    for h in range(n_heads):
        cols = slice(h * dh, (h + 1) * dh)
        ii_c = gates_c[:, h:h + 1]
        lf_c = _log_sigmoid(gates_c[:, n_heads + h:n_heads + h + 1])
        ii_r = gates_r[h:h + 1, :]
        lf_r = _log_sigmoid(gates_r[n_heads + h:n_heads + h + 1, :])
        b_c = jnp.sum(jnp.where(causal, lf_r, 0.0), axis=1, keepdims=True)
        b_r = jnp.sum(jnp.where(t_idx <= s_idx, lf_c, 0.0), axis=0, keepdims=True)
        g_r = ii_r - b_r
        cm_c = jnp.max(jnp.where(causal, g_r, -jnp.inf), axis=1, keepdims=True)
        m0 = m_ref[h:h + 1, 0:1]
        m_t = b_c + jnp.maximum(m0, cm_c)
        inter = jnp.exp(b_c + m0 - m_t)
        dmat = jnp.where(causal, jnp.exp((b_c - m_t) + g_r), 0.0)
        q = q_ref[:, cols].astype(BF16)
        ks = k_ref[:, cols] * kscale
        v = v_ref[:, cols].astype(BF16)
        c0 = c_ref[h]
        n0 = n_ref[h:h + 1, :]
        sc = _dot_nt(q, ks.astype(BF16)) * dmat
        num = inter * _dot(q, c0.astype(BF16)) + _dot(sc.astype(BF16), v)
        qn = jnp.sum(q_ref[:, cols] * n0, axis=1, keepdims=True)
        den = inter * qn + jnp.sum(sc, axis=1, keepdims=True)
        hid = num / jnp.maximum(jnp.abs(den), jnp.exp(-m_t))
        m_last = m_t[L - 1:L, :]
        b_last = b_c[L - 1:L, :]
        decay = jnp.exp(b_last + m0 - m_last)
        wk = jnp.exp(b_last - b_c + ii_c - m_last)
        kw = ks * wk
        c_ref[h] = decay * c0 + _dot_tn(kw.astype(BF16), v)
        n_ref[h:h + 1, :] = decay * n0 + jnp.sum(kw, axis=0, keepdims=True)
        m_ref[h:h + 1, :] = jnp.broadcast_to(m_last, (1, m_ref.shape[1]))
        out = hid * jax.nn.sigmoid(o_ref[:, cols]) * _silu(g_ref[:, cols])
        mix_ref[:, cols] = out.astype(mix_ref.dtype)


def _mlstm_prompt(z3, zg3, gate_bias, *, col0, n_heads=N_HEADS_D, chunk=MLSTM_CHUNK):
    bsz, seq, _ = z3.shape
    width = (z3.shape[2] - col0) // 5
    dh = width // n_heads
    assert seq % chunk == 0 and col0 % width == 0 and zg3.shape[2] == 2 * n_heads == SUBLANES
    base = col0 // width
    zg_rows = jnp.transpose(zg3, (0, 2, 1))
    bias_lane = gate_bias.astype(F32).reshape(1, SUBLANES)
    bias_sub = bias_lane.reshape(SUBLANES, 1)

    def col(off):
        return pl.BlockSpec((None, chunk, width), lambda b, c, off=off: (b, c, base + off))

    mix, c_f, n_f, m_f = pl.pallas_call(
        _mlstm_prompt_kernel,
        grid=(bsz, seq // chunk),
        in_specs=[col(0), col(1), col(2), col(3), col(4),
                  pl.BlockSpec((None, chunk, SUBLANES), lambda b, c: (b, c, 0)),
                  pl.BlockSpec((None, SUBLANES, chunk), lambda b, c: (b, 0, c)),
                  pl.BlockSpec((SUBLANES, 1), lambda b, c: (0, 0)),
                  pl.BlockSpec((1, SUBLANES), lambda b, c: (0, 0))],
        out_specs=[pl.BlockSpec((None, chunk, width), lambda b, c: (b, c, 0)),
                   pl.BlockSpec((None, n_heads, dh, dh), lambda b, c: (b, 0, 0, 0)),
                   pl.BlockSpec((None, n_heads, dh), lambda b, c: (b, 0, 0)),
                   pl.BlockSpec((None, n_heads, LANES), lambda b, c: (b, 0, 0))],
        out_shape=[jax.ShapeDtypeStruct((bsz, seq, width), BF16),
                   jax.ShapeDtypeStruct((bsz, n_heads, dh, dh), F32),
                   jax.ShapeDtypeStruct((bsz, n_heads, dh), F32),
                   jax.ShapeDtypeStruct((bsz, n_heads, LANES), F32)],
        compiler_params=_params("parallel", "arbitrary"),
        name="mlstm_prompt",
    )(z3, z3, z3, z3, z3, zg3, zg_rows, bias_sub, bias_lane)
    return mix, c_f, n_f, m_f[:, :, 0]


def _group_tokens(n_heads):
    assert LANES % n_heads == 0
    return LANES // n_heads


def _stack_groups(zt, n_grp):
    return jnp.concatenate([zt[:, g * LANES:(g + 1) * LANES] for g in range(n_grp)], axis=0)


def _unstack_groups(a, n_grp, n_heads):
    return jnp.concatenate([a[g * n_heads:(g + 1) * n_heads, :] for g in range(n_grp)], axis=1)


def _own_head_mask(n_grp, n_heads):
    r = lax.broadcasted_iota(jnp.int32, (n_grp * n_heads, LANES), 0)
    c = lax.broadcasted_iota(jnp.int32, (n_grp * n_heads, LANES), 1)
    return (r % n_heads) == (c % n_heads)


def _dil_sample_kernel(*refs, n_pat):
    q_ref, kn_ref, vn_ref, g_ref, bh_ref, bn_ref = refs[:6]
    k_refs = refs[6:6 + n_pat]
    v_refs = refs[6 + n_pat:6 + 2 * n_pat]
    o_ref = refs[6 + 2 * n_pat]
    steps, n_heads, dh = k_refs[0].shape
    n_grp = steps * n_heads // LANES
    scale = 1.0 / math.sqrt(dh)
    q = q_ref[...] * scale
    qb = q.astype(BF16)
    zn = jnp.sum(q * kn_ref[...], axis=-1, keepdims=True)
    z_new = [zn + bn_ref[g][:, 0:1] for g in range(n_pat)]
    z_hist = []
    for g in range(n_pat):
        k2 = k_refs[g][...].reshape(steps * n_heads, dh).astype(BF16)
        z_hist.append(_stack_groups(_dot_nt(qb, k2), n_grp) + bh_ref[g])
    mx = functools.reduce(jnp.maximum, z_new)
    for z in z_hist:
        zr = jnp.max(z, axis=-1, keepdims=True)
        for gi in range(n_grp):
            mx = jnp.maximum(mx, zr[gi * n_heads:(gi + 1) * n_heads, :])
    mx_rows = jnp.concatenate([mx] * n_grp, axis=0)
    den = functools.reduce(jnp.add, [jnp.exp(z - mx) for z in z_new])
    w_new = den
    w_hist = [jnp.exp(z - mx_rows) for z in z_hist]
    for w in w_hist:
        wr = jnp.sum(w, axis=-1, keepdims=True)
        for gi in range(n_grp):
            den = den + wr[gi * n_heads:(gi + 1) * n_heads, :]
    inv = 1.0 / den
    inv_rows = jnp.concatenate([inv] * n_grp, axis=0)
    acc = (w_new * inv) * vn_ref[...]
    for g in range(n_pat):
        p = _unstack_groups((w_hist[g] * inv_rows).astype(BF16), n_grp, n_heads)
        v2 = v_refs[g][...].reshape(steps * n_heads, dh).astype(BF16)
        acc = acc + _dot(p, v2)
    o_ref[...] = (acc * _silu(g_ref[...])).astype(o_ref.dtype)


def _dil_sample(zs, buf_k, buf_v, layer, rel_bias, past_len):
    _, bsz, n_hist, n_heads, dh = buf_k.shape
    width = n_heads * dh
    n_pat = len(DILATIONS)
    steps = BAND
    tok = _group_tokens(n_heads)
    n_grp = steps // tok
    bias_hist, bias_new = _sample_bias(rel_bias, n_hist, past_len)
    bh = bias_hist.reshape(n_pat, n_grp, 1, tok, n_heads)
    own = jnp.asarray(np.eye(n_heads, dtype=bool)).reshape(1, 1, n_heads, 1, n_heads)
    bh = jnp.where(own, bh, NEG).reshape(n_pat, n_grp * n_heads, LANES)
    bn = jnp.broadcast_to(bias_new[:, :, None], (n_pat, n_heads, LANES))
    z4 = zs[:, :4 * width].reshape(bsz, 4, n_heads, dh)
    k_views, v_views, kv_specs = [], [], []
    for window, dil in DILATIONS:
        assert window // dil == steps and n_hist % window == 0 and window % dil == 0
        shape = (buf_k.shape[0], bsz, n_hist // dil, dil, n_heads, dh)
        k_views.append(buf_k.reshape(shape))
        v_views.append(buf_v.reshape(shape))
        blk = n_hist // window - 1
        kv_specs.append(pl.BlockSpec((None, None, steps, None, n_heads, dh),
                                     lambda b, blk=blk: (layer, b, blk, 0, 0, 0)))

    def zrow(off):
        return pl.BlockSpec((None, None, n_heads, dh), lambda b, off=off: (b, off, 0, 0))

    return pl.pallas_call(
        functools.partial(_dil_sample_kernel, n_pat=n_pat),
        grid=(bsz,),
        in_specs=[zrow(0), zrow(1), zrow(2), zrow(3),
                  pl.BlockSpec((n_pat, n_grp * n_heads, LANES), lambda b: (0, 0, 0)),
                  pl.BlockSpec((n_pat, n_heads, LANES), lambda b: (0, 0, 0))] + kv_specs + kv_specs,
        out_specs=pl.BlockSpec((None, n_heads, dh), lambda b: (b, 0, 0)),
        out_shape=jax.ShapeDtypeStruct((bsz, n_heads, dh), BF16),
        compiler_params=_params("parallel"),
        name="dilated_attn_sample",
    )(z4, z4, z4, z4, bh, bn, *k_views, *v_views).reshape(bsz, width)


def _pool_sample_kernel(u_ref, hist_ref, gate_ref, pw_ref, ps_ref, o_ref):
    grp = pw_ref.shape[1]
    n_hist = hist_ref.shape[0]
    for g, win in enumerate(POOL_WINDOWS):
        cols = slice(g * grp, (g + 1) * grp)
        x = u_ref[:, cols]
        n_old = min(win - 1, n_hist)
        wsum = x
        for r in range(n_hist - n_old, n_hist):
            wsum = wsum + hist_ref[r, :, cols]
        pooled = wsum / float(n_old + 1) - x
        y = _dot(pooled.astype(BF16), pw_ref[g].astype(BF16)) * ps_ref[:, cols]
        o_ref[:, cols] = (y * _silu(gate_ref[:, cols])).astype(o_ref.dtype)


def _pool_sample(zs, hist, pool_w, pool_scale, layer, *, u_off, gate_off):
    bsz = zs.shape[0]
    _, n_grp, grp, _ = pool_w.shape
    width = n_grp * grp
    n_hist = hist.shape[1]
    hist = jnp.transpose(hist, (1, 0, 2))
    return pl.pallas_call(
        _pool_sample_kernel,
        grid=(1,),
        in_specs=[pl.BlockSpec((bsz, width), lambda i: (0, u_off)),
                  pl.BlockSpec((n_hist, bsz, width), lambda i: (0, 0, 0)),
                  pl.BlockSpec((bsz, width), lambda i: (0, gate_off)),
                  pl.BlockSpec((None, n_grp, grp, grp), lambda i: (layer, 0, 0, 0)),
                  pl.BlockSpec((None, 1, width), lambda i: (layer, 0, 0))],
        out_specs=pl.BlockSpec((bsz, width), lambda i: (0, 0)),
        out_shape=jax.ShapeDtypeStruct((bsz, width), BF16),
        compiler_params=_params("arbitrary"),
        name="pool_mix_sample",
    )(zs, hist, zs, pool_w, pool_scale.reshape(pool_scale.shape[0], 1, width))


SB_SAMPLE_PAGES = 4


def _sb_sample_kernel(pt_ref, q_ref, g_ref, bias_ref, ut_ref, *refs):
    k_refs = refs[:SB_SAMPLE_PAGES]
    v_refs = refs[SB_SAMPLE_PAGES:2 * SB_SAMPLE_PAGES]
    o_ref, carry_sc, acc_sc = refs[2 * SB_SAMPLE_PAGES:]
    p = pl.program_id(1)
    page, n_heads, dh = k_refs[0].shape
    n_grp = page * n_heads // LANES
    scale = 1.0 / math.sqrt(dh)
    pages = range(SB_SAMPLE_PAGES)

    @pl.when(p == 0)
    def _():
        carry_sc[...] = jnp.zeros_like(carry_sc)
        acc_sc[...] = jnp.zeros_like(acc_sc)

    qb = (q_ref[...] * scale).astype(BF16)
    own = _own_head_mask(n_grp, n_heads)
    z = [_stack_groups(_dot_nt(qb, k_refs[u][...].reshape(page * n_heads, dh).astype(BF16)), n_grp)
         + bias_ref[...] for u in pages]
    lhs, log_take = [], []
    for u in pages:
        sp = _softplus(z[u])
        hi, lo = _split_hi_lo(jnp.where(own, -sp, 0.0))
        lhs.append(jnp.concatenate([hi, lo], axis=1))
        log_take.append(z[u] - sp)
    sums = [_dot(lhs[u], ut_ref[...]) for u in pages]
    after = carry_sc[...]
    offs = []
    for u in pages:
        total = sums[u][:, LANES:]
        offs_u = [None] * n_grp
        for gi in range(n_grp - 1, -1, -1):
            offs_u[gi] = after
            after = after + total[gi * n_heads:(gi + 1) * n_heads, :]
        offs.append(jnp.concatenate(offs_u, axis=0))
    carry_sc[...] = after
    acc = acc_sc[...]
    for u in pages:
        a = jnp.where(own, jnp.exp(log_take[u] + sums[u][:, :LANES] + offs[u]), 0.0)
        v2 = v_refs[u][...].reshape(page * n_heads, dh).astype(BF16)
        acc = acc + _dot(_unstack_groups(a.astype(BF16), n_grp, n_heads), v2)
    acc_sc[...] = acc

    @pl.when(p == pl.num_programs(1) - 1)
    def _():
        o_ref[...] = (acc_sc[...] * _silu(g_ref[...])).astype(o_ref.dtype)


def _sb_sample(zs, pool_k, pool_v, layer, page_table, sb_bias):
    bsz, n_pages = page_table.shape
    _, _, page, n_heads, dh = pool_k.shape
    width = n_heads * dh
    tok = _group_tokens(n_heads)
    n_grp = page // tok
    lane_tok = np.arange(LANES) // n_heads
    lane_head = np.arange(LANES) % n_heads
    later = (lane_head[:, None] == lane_head[None, :]) & (lane_tok[:, None] > lane_tok[None, :])
    half = np.concatenate([later.astype(np.float32), np.ones((LANES, LANES), np.float32)], axis=1)
    ut = jnp.asarray(np.concatenate([half, half], axis=0), BF16)
    z4 = zs[:, :4 * width].reshape(bsz, 4, n_heads, dh)
    bias = jnp.tile(jnp.broadcast_to(sb_bias.astype(F32)[:, None], (n_heads, LANES)), (n_grp, 1))

    def page_spec(u):
        return pl.BlockSpec((None, None, page, n_heads, dh),
                            lambda b, p, pt: (layer, pt[b, n_pages - 1 - (p * SB_SAMPLE_PAGES + u)], 0, 0, 0))

    assert n_pages % SB_SAMPLE_PAGES == 0
    page_specs = [page_spec(u) for u in range(SB_SAMPLE_PAGES)]
    return pl.pallas_call(
        _sb_sample_kernel,
        grid_spec=pltpu.PrefetchScalarGridSpec(
            num_scalar_prefetch=1,
            grid=(bsz, n_pages // SB_SAMPLE_PAGES),
            in_specs=[pl.BlockSpec((None, None, n_heads, dh), lambda b, p, pt: (b, 0, 0, 0)),
                      pl.BlockSpec((None, None, n_heads, dh), lambda b, p, pt: (b, 3, 0, 0)),
                      pl.BlockSpec((n_grp * n_heads, LANES), lambda b, p, pt: (0, 0)),
                      pl.BlockSpec((2 * LANES, 2 * LANES), lambda b, p, pt: (0, 0))] + page_specs + page_specs,
            out_specs=pl.BlockSpec((None, n_heads, dh), lambda b, p, pt: (b, 0, 0)),
            scratch_shapes=[pltpu.VMEM((n_heads, LANES), F32), pltpu.VMEM((n_heads, dh), F32)]),
        out_shape=jax.ShapeDtypeStruct((bsz, n_heads, dh), BF16),
        compiler_params=_params("parallel", "arbitrary"),
        name="stick_breaking_sample",
    )(page_table, z4, z4, bias, ut, *([pool_k] * SB_SAMPLE_PAGES), *([pool_v] * SB_SAMPLE_PAGES)).reshape(bsz, width)


def _mlstm_sample_kernel(q_ref, k_ref, v_ref, o_ref, g_ref, sc_ref, c0_ref, n0_ref,
                         mix_ref, c_ref, n_ref, m_ref):
    dh = c0_ref.shape[0]
    kscale = 1.0 / math.sqrt(dh)
    s = sc_ref[...]
    ii = s[:, 0:1] + s[:, 2:3]
    lf = _log_sigmoid(s[:, 1:2] + s[:, 3:4])
    m0 = s[:, 4:5]
    m_t = jnp.maximum(lf + m0, ii)
    inter = jnp.exp(lf + m0 - m_t)
    wk = jnp.exp(ii - m_t)
    q = q_ref[...]
    ks = k_ref[...] * kscale
    v = v_ref[...]
    c0 = c0_ref[...]
    n0 = n0_ref[...]
    sc = jnp.sum(q * ks, axis=0, keepdims=True) * wk
    num = inter * jnp.sum(c0 * q, axis=0, keepdims=True) + sc * v
    den = inter * jnp.sum(q * n0, axis=0, keepdims=True) + sc
    hid = num / jnp.maximum(jnp.abs(den), jnp.exp(-m_t))
    c_ref[...] = inter * c0 + (ks * wk) * v
    n_ref[...] = inter * n0 + ks * wk
    m_ref[...] = jnp.broadcast_to(m_t, m_ref.shape)
    mix_ref[...] = (hid * jax.nn.sigmoid(o_ref[...]) * _silu(g_ref[...])).astype(mix_ref.dtype)


def _mlstm_sample(zs, zg, gate_bias, c0, n0, m0, layer, *, col0, n_heads=N_HEADS_D):
    bsz = zs.shape[0]
    dh = c0.shape[-1]
    width = n_heads * dh
    grp = zs[:, col0:col0 + 5 * width].reshape(bsz, 5, n_heads, dh)
    q_col = grp[:, 0].reshape(bsz, n_heads, dh, 1)
    k_col = grp[:, 1].reshape(bsz, n_heads, dh, 1)
    v_row = grp[:, 2].reshape(bsz, n_heads, 1, dh)
    o_row = grp[:, 3].reshape(bsz, n_heads, 1, dh)
    g_row = grp[:, 4].reshape(bsz, n_heads, 1, dh)
    gb = gate_bias.astype(F32)
    scal = jnp.stack([zg[:, :n_heads], zg[:, n_heads:2 * n_heads],
                      jnp.broadcast_to(gb[0][None], (bsz, n_heads)),
                      jnp.broadcast_to(gb[1][None], (bsz, n_heads)),
                      m0.astype(F32)], axis=-1)
    scal = jnp.pad(scal, ((0, 0), (0, 0), (0, LANES - scal.shape[-1]))).reshape(bsz, n_heads, 1, LANES)

    def spec(r, c):
        return pl.BlockSpec((None, None, r, c), lambda b, h: (b, h, 0, 0))

    mix, c_n, n_n, m_n = pl.pallas_call(
        _mlstm_sample_kernel,
        grid=(bsz, n_heads),
        in_specs=[spec(dh, 1), spec(dh, 1), spec(1, dh), spec(1, dh), spec(1, dh), spec(1, LANES),
                  pl.BlockSpec((None, None, None, dh, dh), lambda b, h: (layer, b, h, 0, 0)),
                  spec(dh, 1)],
        out_specs=[spec(1, dh), spec(dh, dh), spec(dh, 1), spec(1, LANES)],
        out_shape=[jax.ShapeDtypeStruct((bsz, n_heads, 1, dh), BF16),
                   jax.ShapeDtypeStruct((bsz, n_heads, dh, dh), F32),
                   jax.ShapeDtypeStruct((bsz, n_heads, dh, 1), F32),
                   jax.ShapeDtypeStruct((bsz, n_heads, 1, LANES), F32)],
        compiler_params=_params("parallel", "parallel"),
        name="mlstm_sample",
    )(q_col, k_col, v_row, o_row, g_row, scal, c0, n0.astype(F32).reshape(bsz, n_heads, dh, 1))
    return mix.reshape(bsz, width), c_n, n_n.reshape(bsz, n_heads, dh), m_n[:, :, 0, 0]


PROJ_TM = 2048
PROJ_TN = 512


def kernel(x_prompt, x_sample, cache_a_k, cache_a_v, state_pool, cache_c_k, cache_c_v, state_mlstm_c,
           state_mlstm_n, state_mlstm_m, page_table, rel_bias, w_in_even, pool_w, pool_scale, w_out_even,
           w_in_odd, sb_bias, mlstm_gate_bias, w_out_odd, ln_g, ln_b):
    bsz, seq, d_model = x_prompt.shape
    dbs, dec_seq, _ = x_sample.shape
    assert dec_seq == 1
    depth = ln_g.shape[0]
    alpha = (2 * depth) ** 0.25
    past_len = page_table.shape[1] * PAGE_SIZE
    mix_a = d_model // 2
    n_heads = mix_a // HEAD_DIM
    mix_b = d_model - mix_a
    n_rows = bsz * seq
    even_cols = w_in_even.shape[2]
    odd_main = 9 * mix_a
    n_gate = w_in_odd.shape[2] - odd_main
    w_in_odd_nk = jnp.swapaxes(w_in_odd, 1, 2)

    xp_f = x_prompt.reshape(n_rows, d_model)
    xp_b = xp_f.astype(BF16)
    xs_f = x_sample.reshape(dbs, d_model)
    xs_b = xs_f.astype(BF16)
    band_bias = _band_bias(rel_bias)
    u_blk = 4 * mix_a // mix_b

    zp_even, zp_odd, zs_even, zs_odd = [], [], [], []
    mcp, mnp, mmp, mcs, mns, mms = [], [], [], [], [], []
    for layer in range(depth):
        j = layer // 2
        if layer % 2 == 0:
            z, zs = _proj(xp_b, xs_b, w_in_even, j, even_cols, w_is_nk=False, tm=PROJ_TM, tn=PROJ_TN)
            z3 = z.reshape(bsz, seq, -1)
            oa = _dil_prompt(z3, band_bias, n_heads)
            ob = _pool_prompt(z3, pool_w, pool_scale, j, u_off=u_blk, gate_off=u_blk + 1)
            oas = _dil_sample(zs, cache_a_k, cache_a_v, j, rel_bias, past_len)
            obs = _pool_sample(zs, state_pool[j], pool_w, pool_scale, j, u_off=u_blk, gate_off=u_blk + 1)
            yp, ys = _out_proj(oa.reshape(n_rows, mix_a), ob.reshape(n_rows, mix_b), oas, obs, w_out_even, j,
                               tm=PROJ_TM, tn=PROJ_TN)
            zp_even.append(z3)
            zs_even.append(zs)
        else:
            z, zs = _proj(xp_b, xs_b, w_in_odd_nk, j, odd_main, w_is_nk=True, tm=PROJ_TM, tn=PROJ_TN)
            zg, zgs = _gate_proj(xp_b, xs_b, w_in_odd_nk, j, odd_main, n_gate, tm=PROJ_TM)
            z3 = z.reshape(bsz, seq, -1)
            oc = _sb_prompt(z3, sb_bias[j], n_heads)
            od, c_f, n_f, m_f = _mlstm_prompt(z3, zg.reshape(bsz, seq, n_gate), mlstm_gate_bias[j], col0=4 * mix_a)
            ocs = _sb_sample(zs, cache_c_k, cache_c_v, j, page_table, sb_bias[j])
            ods, c_n, n_n, m_n = _mlstm_sample(zs, zgs, mlstm_gate_bias[j], state_mlstm_c, state_mlstm_n[j],
                                               state_mlstm_m[j], j, col0=4 * mix_a)
            yp, ys = _out_proj(oc.reshape(n_rows, mix_a), od.reshape(n_rows, mix_b), ocs, ods, w_out_odd, j,
                               tm=PROJ_TM, tn=PROJ_TN)
            zp_odd.append(z3)
            zs_odd.append(zs)
            mcp.append(c_f)
            mnp.append(n_f)
            mmp.append(m_f)
            mcs.append(c_n)
            mns.append(n_n)
            mms.append(m_n)
        xp_f, xp_b = _residual_ln(xp_f, yp, ln_g[layer], ln_b[layer], alpha=alpha, tm=256)
        xs_f, xs_b = _residual_ln(xs_f, ys, ln_g[layer], ln_b[layer], alpha=alpha, tm=dbs)

    keep_a = min(A_WINDOW, seq)
    keep_u = min(POOL_HIST, seq)

    def heads(a):
        return a.reshape(a.shape[:-1] + (n_heads, HEAD_DIM))

    def pick(zs, rows, lo, hi):
        return jnp.stack([z[..., rows, lo:hi] for z in zs])

    every = slice(None)
    new_row = np.newaxis
    a_k_prompt = heads(pick(zp_even, slice(seq - keep_a, seq), mix_a, 2 * mix_a))
    a_v_prompt = heads(pick(zp_even, slice(seq - keep_a, seq), 2 * mix_a, 3 * mix_a))
    pool_prompt = pick(zp_even, slice(seq - keep_u, seq), 4 * mix_a, 4 * mix_a + mix_b)
    a_k_sample, a_v_sample = _shift_windows(
        cache_a_k, cache_a_v, heads(pick(zs_even, every, mix_a, 2 * mix_a))[:, :, new_row],
        heads(pick(zs_even, every, 2 * mix_a, 3 * mix_a))[:, :, new_row])
    pool_sample = jnp.concatenate([state_pool[:, :, 1:],
                                   pick(zs_even, every, 4 * mix_a, 4 * mix_a + mix_b)[:, :, new_row]], axis=2)
    c_k_prompt = heads(pick(zp_odd, every, mix_a, 2 * mix_a))
    c_v_prompt = heads(pick(zp_odd, every, 2 * mix_a, 3 * mix_a))
    c_k_sample = heads(pick(zs_odd, every, mix_a, 2 * mix_a))[:, :, new_row]
    c_v_sample = heads(pick(zs_odd, every, 2 * mix_a, 3 * mix_a))[:, :, new_row]
    return (xp_f.reshape(bsz, seq, d_model), xs_f.reshape(dbs, 1, d_model),
            a_k_prompt, a_v_prompt, a_k_sample, a_v_sample, pool_prompt, pool_sample,
            c_k_prompt, c_v_prompt, c_k_sample, c_v_sample,
            jnp.stack(mcp), jnp.stack(mnp), jnp.stack(mmp), jnp.stack(mcs), jnp.stack(mns), jnp.stack(mms))
```
